```python
import functools
import jax, jax.numpy as jnp
from jax import lax
import numpy as np

D_MODEL = 2048
BATCH = 2
SEQ = 4096
DEPTH = 1
DEC_BATCH = 32
DEC_SEQ = 4
PAST_LEN = 8192
PAGE_SIZE = 128

D_MIX = D_MODEL
D_RNN = D_MIX // 2
LRU_BLOCKS = 8
LRU_BW = D_RNN // LRU_BLOCKS
CONV_W = 4
LRU_C = 8.0
HEAD_DIM = 128
N_Q_HEADS = (D_MIX - D_RNN) // HEAD_DIM
N_KV_HEADS = 4
GQA_GROUP = N_Q_HEADS // N_KV_HEADS
IDX_HEADS = 8
IDX_DIM = 64
TOPK_MAX = 256
Q_BLOCK = 128
D_FF = 5504
LN_EPS = 1e-5
ALPHA = (2.0 * DEPTH) ** 0.25
BETA = (8.0 * DEPTH) ** -0.25
ATTN_SCALE = HEAD_DIM ** -0.5
IDX_SCALE = IDX_DIM ** -0.5
IDX_W_SCALE = IDX_HEADS ** -0.5
SPLIT_SIZES = (D_RNN, D_RNN, N_Q_HEADS * HEAD_DIM, N_KV_HEADS * HEAD_DIM,
               N_KV_HEADS * HEAD_DIM, IDX_HEADS * IDX_DIM, IDX_DIM, IDX_HEADS)
D_IN = sum(SPLIT_SIZES)
SPLIT_POINTS = tuple(int(s) for s in np.cumsum(SPLIT_SIZES)[:-1])

kernel_name = 'hymba_rglru_dsa_macaron_step'


def layer_norm(x, g, b):
    xf = x.astype(jnp.float32)
    mu = jnp.mean(xf, axis=-1, keepdims=True)
    var = jnp.mean(jnp.square(xf - mu), axis=-1, keepdims=True)
    return ((xf - mu) * lax.rsqrt(var + LN_EPS) * g + b).astype(x.dtype)


def swiglu(x, w_gu, w_down):
    gate, up = jnp.split(x @ w_gu, 2, axis=-1)
    return (jax.nn.silu(gate) * up) @ w_down


def causal_conv(x, buf, w, b):
    T = x.shape[1]
    xp = jnp.concatenate([buf.astype(x.dtype), x], axis=1)
    out = b + w[0] * xp[:, 0:T]
    for j in range(1, CONV_W):
        out = out + w[j] * xp[:, j:j + T]
    return out, xp[:, -(CONV_W - 1):]


def rg_lru(xc, h0, w_a, b_a, w_i, b_i, lam):
    B, T, _ = xc.shape
    xb = xc.reshape(B, T, LRU_BLOCKS, LRU_BW)
    r = jax.nn.sigmoid((jnp.einsum('btnc,ncd->btnd', xb, w_a).reshape(B, T, D_RNN) + b_a).astype(jnp.float32))
    i = jax.nn.sigmoid((jnp.einsum('btnc,ncd->btnd', xb, w_i).reshape(B, T, D_RNN) + b_i).astype(jnp.float32))
    log_a = -LRU_C * r * jax.nn.softplus(-lam.astype(jnp.float32))
    a = jnp.exp(log_a)
    u = jnp.sqrt(-jnp.expm1(2.0 * log_a)) * i * xc.astype(jnp.float32)
    u = u.at[:, 0].add(a[:, 0] * h0.astype(jnp.float32))

    def combine(left, right):
        return (left[0] * right[0], right[0] * left[1] + right[1])

    _, h = lax.associative_scan(combine, (a, u), axis=1)
    return h, h[:, -1]


def indexer_scores(qi, wi, ki):
    s = jax.nn.relu(jnp.einsum('bqhd,bld->bqhl', qi, ki).astype(jnp.float32) * IDX_SCALE)
    return jnp.einsum('bqhl,bqh->bql', s, wi.astype(jnp.float32) * IDX_W_SCALE)


def attend_selected(q, kg, vg, valid):
    B, Q = q.shape[:2]
    qg = q.reshape(B, Q, N_KV_HEADS, GQA_GROUP, HEAD_DIM)
    logits = jnp.einsum('bqngd,bqsnd->bqngs', qg, kg).astype(jnp.float32) * ATTN_SCALE
    logits = jnp.where(valid[:, :, None, None, :], logits, -jnp.inf)
    p = jax.nn.softmax(logits, axis=-1).astype(vg.dtype)
    o = jnp.einsum('bqngs,bqsnd->bqngd', p, vg)
    return o.reshape(B, Q, N_Q_HEADS * HEAD_DIM)


def select_and_attend(q, scores, admissible, t_pos, topk, gather):
    scores = jnp.where(admissible[None], scores, -jnp.inf)
    _, idx = lax.top_k(scores, topk)
    valid = idx <= t_pos[None, :, None]
    kg, vg = gather(idx)
    return attend_selected(q, kg, vg, valid)


def prompt_sparse_attention(q, k, v, qi, ki, wi):
    B, S = q.shape[:2]
    topk = min(TOPK_MAX, S // 4)
    n_blk = S // Q_BLOCK
    key_pos = jnp.arange(S)

    def to_blocks(a):
        return jnp.moveaxis(a.reshape((B, n_blk, Q_BLOCK) + a.shape[2:]), 1, 0)

    def gather(idx):
        take = jax.vmap(lambda rows, ii: rows[ii])
        return take(k, idx), take(v, idx)

    def one_block(args):
        qb, qib, wib, blk = args
        t = blk * Q_BLOCK + jnp.arange(Q_BLOCK)
        scores = indexer_scores(qib, wib, ki)
        admissible = key_pos[None, :] <= t[:, None]
        return select_and_attend(qb, scores, admissible, t, topk, gather)

    out = lax.map(one_block, (to_blocks(q), to_blocks(qi), to_blocks(wi), jnp.arange(n_blk)))
    return jnp.moveaxis(out, 0, 1).reshape(B, S, N_Q_HEADS * HEAD_DIM)


def sample_sparse_attention(q, k, v, qi, ki, wi, cache_k, cache_v, cache_k_idx, page_table):
    B, T = q.shape[:2]
    n_past = page_table.shape[1] * PAGE_SIZE
    L = n_past + T
    topk = min(TOPK_MAX, L // 4)
    past_ki = cache_k_idx[page_table].reshape(B, n_past, IDX_DIM)
    keys_idx = jnp.concatenate([past_ki, ki.astype(past_ki.dtype)], axis=1)
    t = n_past + jnp.arange(T)
    scores = indexer_scores(qi, wi, keys_idx)
    admissible = jnp.arange(L)[None, :] <= t[:, None]

    def gather(idx):
        in_past = idx < n_past
        pidx = jnp.minimum(idx, n_past - 1)
        phys = jax.vmap(lambda pt, ii: pt[ii])(page_table, pidx // PAGE_SIZE)
        off = pidx % PAGE_SIZE
        nidx = jnp.clip(idx - n_past, 0, T - 1)
        take = jax.vmap(lambda rows, ii: rows[ii])

        def pick(pool, new):
            past_rows = pool[phys, off]
            new_rows = take(new, nidx).astype(past_rows.dtype)
            return jnp.where(in_past[..., None, None], past_rows, new_rows)

        return pick(cache_k, k), pick(cache_v, v)

    return select_and_attend(q, scores, admissible, t, topk, gather)


def decoder_layer(x, attn_fn, conv_buf, h0, ln1_g, ln1_b, ffn1_w_gu, ffn1_w_down, w_in,
                  conv_w, conv_b, lru_w_a, lru_b_a, lru_w_i, lru_b_i, lru_lambda, w_out,
                  ln2_g, ln2_b, ffn2_w_gu, ffn2_w_down, ln3_g, ln3_b):
    B, T, _ = x.shape
    x = layer_norm(ALPHA * x + 0.5 * swiglu(x, ffn1_w_gu, ffn1_w_down), ln1_g, ln1_b)
    xr, gr, q, k, v, qi, ki, wi = jnp.split(x @ w_in, SPLIT_POINTS, axis=-1)
    xc, conv_state = causal_conv(xr, conv_buf, conv_w, conv_b)
    h, h_last = rg_lru(xc, h0, lru_w_a, lru_b_a, lru_w_i, lru_b_i, lru_lambda)
    rnn_out = h.astype(x.dtype) * jax.nn.gelu(gr)
    q = q.reshape(B, T, N_Q_HEADS, HEAD_DIM)
    k = k.reshape(B, T, N_KV_HEADS, HEAD_DIM)
    v = v.reshape(B, T, N_KV_HEADS, HEAD_DIM)
    qi = qi.reshape(B, T, IDX_HEADS, IDX_DIM)
    attn_out = attn_fn(q, k, v, qi, ki, wi)
    mix = jnp.concatenate([rnn_out, attn_out.astype(x.dtype)], axis=-1) @ w_out
    x = layer_norm(ALPHA * x + mix, ln2_g, ln2_b)
    x = layer_norm(ALPHA * x + 0.5 * swiglu(x, ffn2_w_gu, ffn2_w_down), ln3_g, ln3_b)
    return x, k, v, ki, conv_state, h_last


def setup_inputs(seed: int = 0) -> dict:
    key = jax.random.key(seed)
    ks = jax.random.split(key, 32)
    n_pages = PAST_LEN // PAGE_SIZE
    n_used = DEC_BATCH * n_pages
    n_pool = n_used + n_used // 4

    def nrm(k, shape, scale):
        return jax.random.normal(k, shape, jnp.float32) * scale

    u = jax.random.uniform(ks[19], (DEPTH, D_RNN), jnp.float32, minval=0.9, maxval=0.999)
    s = u ** (1.0 / LRU_C)
    return {
        'x_prompt': nrm(ks[0], (BATCH, SEQ, D_MODEL), 1.0),
        'x_sample': nrm(ks[1], (DEC_BATCH, DEC_SEQ, D_MODEL), 1.0),
        'cache_k': nrm(ks[2], (DEPTH, n_pool, PAGE_SIZE, N_KV_HEADS, HEAD_DIM), 1.0),
        'cache_v': nrm(ks[3], (DEPTH, n_pool, PAGE_SIZE, N_KV_HEADS, HEAD_DIM), 1.0),
        'cache_k_idx': nrm(ks[4], (DEPTH, n_pool, PAGE_SIZE, IDX_DIM), 1.0),
        'state_conv': nrm(ks[5], (DEPTH, DEC_BATCH, CONV_W - 1, D_RNN), 1.0),
        'state_rnn': nrm(ks[6], (DEPTH, DEC_BATCH, D_RNN), 0.5),
        'page_table': jax.random.permutation(ks[7], n_pool)[:n_used].reshape(DEC_BATCH, n_pages).astype(jnp.int32),
        'ln1_g': 1.0 + nrm(ks[8], (DEPTH, D_MODEL), 0.01),
        'ln1_b': nrm(ks[9], (DEPTH, D_MODEL), 0.01),
        'ffn1_w_gu': nrm(ks[10], (DEPTH, D_MODEL, 2 * D_FF), D_MODEL ** -0.5),
        'ffn1_w_down': nrm(ks[11], (DEPTH, D_FF, D_MODEL), BETA * D_FF ** -0.5),
        'w_in': nrm(ks[12], (DEPTH, D_MODEL, D_IN), D_MODEL ** -0.5),
        'conv_w': nrm(ks[13], (DEPTH, CONV_W, D_RNN), CONV_W ** -0.5),
        'conv_b': nrm(ks[14], (DEPTH, D_RNN), 0.01),
        'lru_w_a': nrm(ks[15], (DEPTH, LRU_BLOCKS, LRU_BW, LRU_BW), LRU_BW ** -0.5),
        'lru_b_a': nrm(ks[16], (DEPTH, D_RNN), 0.01),
        'lru_w_i': nrm(ks[17], (DEPTH, LRU_BLOCKS, LRU_BW, LRU_BW), LRU_BW ** -0.5),
        'lru_b_i': nrm(ks[18], (DEPTH, D_RNN), 0.01),
        'lru_lambda': jnp.log(s) - jnp.log1p(-s),
        'w_out': nrm(ks[20], (DEPTH, D_MIX, D_MODEL), BETA * D_MIX ** -0.5),
        'ln2_g': 1.0 + nrm(ks[21], (DEPTH, D_MODEL), 0.01),
        'ln2_b': nrm(ks[22], (DEPTH, D_MODEL), 0.01),
        'ffn2_w_gu': nrm(ks[23], (DEPTH, D_MODEL, 2 * D_FF), D_MODEL ** -0.5),
        'ffn2_w_down': nrm(ks[24], (DEPTH, D_FF, D_MODEL), BETA * D_FF ** -0.5),
        'ln3_g': 1.0 + nrm(ks[25], (DEPTH, D_MODEL), 0.01),
        'ln3_b': nrm(ks[26], (DEPTH, D_MODEL), 0.01),
    }


def reference(x_prompt, x_sample, cache_k, cache_v, cache_k_idx, state_conv, state_rnn, page_table,
              ln1_g, ln1_b, ffn1_w_gu, ffn1_w_down, w_in, conv_w, conv_b, lru_w_a, lru_b_a,
              lru_w_i, lru_b_i, lru_lambda, w_out, ln2_g, ln2_b, ffn2_w_gu, ffn2_w_down, ln3_g, ln3_b):
    B = x_prompt.shape[0]
    y_p, y_s = x_prompt, x_sample
    kp_l, vp_l, kip_l, cp_l, hp_l = [], [], [], [], []
    ks_l, vs_l, kis_l, cs_l, hs_l = [], [], [], [], []
    for l in range(DEPTH):
        lw = dict(ln1_g=ln1_g[l], ln1_b=ln1_b[l], ffn1_w_gu=ffn1_w_gu[l], ffn1_w_down=ffn1_w_down[l],
                  w_in=w_in[l], conv_w=conv_w[l], conv_b=conv_b[l], lru_w_a=lru_w_a[l],
                  lru_b_a=lru_b_a[l], lru_w_i=lru_w_i[l], lru_b_i=lru_b_i[l],
                  lru_lambda=lru_lambda[l], w_out=w_out[l], ln2_g=ln2_g[l], ln2_b=ln2_b[l],
                  ffn2_w_gu=ffn2_w_gu[l], ffn2_w_down=ffn2_w_down[l], ln3_g=ln3_g[l], ln3_b=ln3_b[l])
        y_p, kp, vp, kip, cp, hp = decoder_layer(
            y_p, prompt_sparse_attention,
            jnp.zeros((B, CONV_W - 1, D_RNN), y_p.dtype), jnp.zeros((B, D_RNN), jnp.float32), **lw)
        sample_attn = functools.partial(sample_sparse_attention, cache_k=cache_k[l], cache_v=cache_v[l],
                                        cache_k_idx=cache_k_idx[l], page_table=page_table)
        y_s, ksm, vsm, kism, csm, hsm = decoder_layer(y_s, sample_attn, state_conv[l], state_rnn[l], **lw)
        kp_l.append(kp); vp_l.append(vp); kip_l.append(kip); cp_l.append(cp); hp_l.append(hp)
        ks_l.append(ksm); vs_l.append(vsm); kis_l.append(kism); cs_l.append(csm); hs_l.append(hsm)
    return (y_p, y_s,
            jnp.stack(kp_l), jnp.stack(vp_l), jnp.stack(kip_l), jnp.stack(cp_l), jnp.stack(hp_l),
            jnp.stack(ks_l), jnp.stack(vs_l), jnp.stack(kis_l), jnp.stack(cs_l), jnp.stack(hs_l))
```

```python
import functools

import jax
import jax.numpy as jnp
from jax import lax
from jax.experimental import pallas as pl
from jax.experimental.pallas import tpu as pltpu

F32 = jnp.float32
BF16 = jnp.bfloat16
I32 = jnp.int32

LRU_BLOCKS = 8
CONV_W = 4
LRU_C = 8.0
HEAD_DIM = 128
N_KV_HEADS = 4
GQA_GROUP = 2
N_Q_HEADS = N_KV_HEADS * GQA_GROUP
IDX_HEADS = 8
IDX_DIM = 64
TOPK_MAX = 256
Q_BLOCK = 128
PAGE_SIZE = 128
LN_EPS = 1e-5
ATTN_SCALE = HEAD_DIM ** -0.5
IDX_SCALE = IDX_DIM ** -0.5
IDX_W_SCALE = IDX_HEADS ** -0.5

LANES = 128
SUBLANES = 8
MXU_DIM = 256
VMEM_LIMIT = 56 * 1024 * 1024

INT_MIN = -2 ** 31
NEG_BIG = -1e30
KEY_CHUNK = 512
PAGES_PER_STEP = 8
IDX_PAGES_PER_STEP = 16


def _cparams(semantics):
    return pltpu.CompilerParams(dimension_semantics=semantics, vmem_limit_bytes=VMEM_LIMIT)


def _layer_norm(y, g, b):
    mu = jnp.mean(y, axis=-1, keepdims=True)
    d = y - mu
    var = jnp.mean(d * d, axis=-1, keepdims=True)
    return d * lax.rsqrt(var + LN_EPS) * g + b


def _dot(a, b):
    return jnp.dot(a, b, preferred_element_type=F32)


def _dot_nt(a, b):
    return lax.dot_general(a, b, (((1,), (1,)), ((), ())), preferred_element_type=F32)


def _ffn_kernel(x_ref, wg_ref, wu_ref, wd_ref, g_ref, b_ref, o_ref, xb_ref, acc_ref, *, alpha):
    j = pl.program_id(1)

    @pl.when(j == 0)
    def _():
        xb_ref[...] = x_ref[...].astype(BF16)
        acc_ref[...] = jnp.zeros_like(acc_ref)

    xb = xb_ref[...]
    gate = _dot(xb, wg_ref[...])
    up = _dot(xb, wu_ref[...])
    act = (gate * jax.nn.sigmoid(gate) * up).astype(BF16)
    acc_ref[...] += _dot(act, wd_ref[...])

    @pl.when(j == pl.num_programs(1) - 1)
    def _():
        y = alpha * x_ref[...] + 0.5 * acc_ref[...]
        o_ref[...] = _layer_norm(y, g_ref[...], b_ref[...])


def _ffn_ln(x, wg, wu, wd, g, b, *, alpha, tm, tf):
    rows, d = x.shape
    ffp = wg.shape[1]
    return pl.pallas_call(
        functools.partial(_ffn_kernel, alpha=alpha),
        grid=(rows // tm, ffp // tf),
        in_specs=[
            pl.BlockSpec((tm, d), lambda i, j: (i, 0)),
            pl.BlockSpec((d, tf), lambda i, j: (0, j)),
            pl.BlockSpec((d, tf), lambda i, j: (0, j)),
            pl.BlockSpec((tf, d), lambda i, j: (j, 0)),
            pl.BlockSpec((1, d), lambda i, j: (0, 0)),
            pl.BlockSpec((1, d), lambda i, j: (0, 0)),
        ],
        out_specs=pl.BlockSpec((tm, d), lambda i, j: (i, 0)),
        out_shape=jax.ShapeDtypeStruct((rows, d), F32),
        scratch_shapes=[pltpu.VMEM((tm, d), BF16), pltpu.VMEM((tm, d), F32)],
        compiler_params=_cparams(("parallel", "arbitrary")),
        name="ffn_ln",
    )(x, wg, wu, wd, g, b)


def _proj_in_kernel(x_ref, w_ref, xr_ref, gr_ref, q_ref, k_ref, v_ref, kb_ref, vb_ref, qi_ref, kw_ref,
                    *, d_rnn, d_q, d_kv, d_qi):
    xb = x_ref[...].astype(BF16)
    off = [0]

    def seg(width):
        lo = off[0]
        off[0] = lo + width
        return _dot(xb, w_ref[:, lo:lo + width])

    xr_ref[...] = seg(d_rnn)
    gr_ref[...] = seg(d_rnn)
    q_ref[...] = seg(d_q).astype(BF16)
    k = seg(d_kv)
    k_ref[...] = k
    kb_ref[...] = k.astype(BF16)
    v = seg(d_kv)
    v_ref[...] = v
    vb_ref[...] = v.astype(BF16)
    qi_ref[...] = seg(d_qi).astype(BF16)
    kw_ref[...] = seg(LANES)


def _proj_in(x, w, *, tm, d_rnn, d_q, d_kv, d_qi):
    rows, d = x.shape
    widths = (d_rnn, d_rnn, d_q, d_kv, d_kv, d_kv, d_kv, d_qi, LANES)
    dtypes = (F32, F32, BF16, F32, F32, BF16, BF16, BF16, F32)
    return pl.pallas_call(
        functools.partial(_proj_in_kernel, d_rnn=d_rnn, d_q=d_q, d_kv=d_kv, d_qi=d_qi),
        grid=(rows // tm,),
        in_specs=[
            pl.BlockSpec((tm, d), lambda i: (i, 0)),
            pl.BlockSpec(w.shape, lambda i: (0, 0), pipeline_mode=pl.Buffered(1)),
        ],
        out_specs=[pl.BlockSpec((tm, wd), lambda i: (i, 0)) for wd in widths],
        out_shape=[jax.ShapeDtypeStruct((rows, wd), dt) for wd, dt in zip(widths, dtypes)],
        compiler_params=_cparams(("parallel",)),
        name="proj_in",
    )(x, w)


def _softplus(z):
    return jnp.maximum(z, 0.0) + jnp.log1p(jnp.exp(-jnp.abs(z)))


def _lru_gates(xc, wa_ref, ba, wi_ref, bi, lam):
    bw = xc.shape[1] // LRU_BLOCKS
    sp = _softplus(-lam)
    a_parts, u_parts = [], []
    for n in range(LRU_BLOCKS):
        sl = slice(n * bw, (n + 1) * bw)
        xn = xc[:, sl]
        xb = xn.astype(BF16)
        r = jax.nn.sigmoid(_dot(xb, wa_ref[n]) + ba[:, sl])
        i = jax.nn.sigmoid(_dot(xb, wi_ref[n]) + bi[:, sl])
        log_a = -LRU_C * r * sp[:, sl]
        a = jnp.exp(log_a)
        a_parts.append(a)
        u_parts.append(jnp.sqrt(-jnp.tanh(log_a) * (a * a + 1.0)) * i * xn)
    return jnp.concatenate(a_parts, axis=1), jnp.concatenate(u_parts, axis=1)


def _lru_prompt_kernel(xr_ref, gr_ref, cw_ref, cb_ref, wa_ref, ba_ref, wi_ref, bi_ref, lam_ref,
                       o_ref, hl_ref, xp_ref, a_ref, u_ref, hs_ref, hc_ref, *, tt):
    j = pl.program_id(1)
    d = xr_ref.shape[1]

    @pl.when(j == 0)
    def _():
        xp_ref[0:SUBLANES, :] = jnp.zeros((SUBLANES, d), F32)
        hc_ref[...] = jnp.zeros_like(hc_ref)

    @pl.when(j > 0)
    def _():
        xp_ref[0:SUBLANES, :] = xp_ref[tt:tt + SUBLANES, :]

    xp_ref[SUBLANES:SUBLANES + tt, :] = xr_ref[...]
    cw = cw_ref[...]
    xc = cb_ref[...]
    for jj in range(CONV_W):
        lo = SUBLANES - (CONV_W - 1) + jj
        xc = xc + cw[jj:jj + 1, :] * xp_ref[lo:lo + tt, :]

    a, u = _lru_gates(xc, wa_ref, ba_ref[...], wi_ref, bi_ref[...], lam_ref[...])
    a_ref[...] = a
    u_ref[...] = u

    row = lax.broadcasted_iota(I32, (SUBLANES, d), 0)

    def group(g, h):
        r0 = pl.multiple_of(g * SUBLANES, SUBLANES)
        a8 = a_ref[pl.ds(r0, SUBLANES), :]
        u8 = u_ref[pl.ds(r0, SUBLANES), :]
        out = jnp.zeros((SUBLANES, d), F32)
        for jj in range(SUBLANES):
            aj = jnp.broadcast_to(a8[jj:jj + 1, :], (SUBLANES, d))
            uj = jnp.broadcast_to(u8[jj:jj + 1, :], (SUBLANES, d))
            h = aj * h + uj
            out = jnp.where(row == jj, h, out)
        hs_ref[pl.ds(r0, SUBLANES), :] = out
        return h

    h = lax.fori_loop(0, tt // SUBLANES, group, hc_ref[...])
    hc_ref[...] = h
    hl_ref[...] = h[0:1, :]
    o_ref[...] = (hs_ref[...] * jax.nn.gelu(gr_ref[...])).astype(BF16)


def _lru_prompt(xr, gr, cw, cb, wa, ba, wi, bi, lam, *, batch, tt):
    rows, d = xr.shape
    seq = rows // batch
    nt = seq // tt
    row_spec = pl.BlockSpec((tt, d), lambda b, j: (b * nt + j, 0))
    vec_spec = pl.BlockSpec((1, d), lambda b, j: (0, 0))
    w_spec = pl.BlockSpec(wa.shape, lambda b, j: (0, 0, 0))
    return pl.pallas_call(
        functools.partial(_lru_prompt_kernel, tt=tt),
        grid=(batch, nt),
        in_specs=[row_spec, row_spec, pl.BlockSpec((CONV_W, d), lambda b, j: (0, 0)), vec_spec,
                  w_spec, vec_spec, w_spec, vec_spec, vec_spec],
        out_specs=[row_spec, pl.BlockSpec((None, 1, d), lambda b, j: (b, 0, 0))],
        out_shape=[jax.ShapeDtypeStruct((rows, d), BF16), jax.ShapeDtypeStruct((batch, 1, d), F32)],
        scratch_shapes=[pltpu.VMEM((tt + SUBLANES, d), F32), pltpu.VMEM((tt, d), F32),
                        pltpu.VMEM((tt, d), F32), pltpu.VMEM((tt, d), F32), pltpu.VMEM((SUBLANES, d), F32)],
        compiler_params=_cparams(("parallel", "arbitrary")),
        name="lru_prompt",
    )(xr, gr, cw, cb, wa, ba, wi, bi, lam)


def _lru_sample_kernel(xp_ref, gr_ref, h0_ref, cw_ref, cb_ref, wa_ref, ba_ref, wi_ref, bi_ref, lam_ref,
                       o_ref, hl_ref, *, nb, nt):
    cw = cw_ref[...]
    xc = cb_ref[...] + cw[0:1, :] * xp_ref[0:nt * nb, :]
    for jj in range(1, CONV_W):
        xc = xc + cw[jj:jj + 1, :] * xp_ref[jj * nb:(jj + nt) * nb, :]
    a, u = _lru_gates(xc, wa_ref, ba_ref[...], wi_ref, bi_ref[...], lam_ref[...])
    gate = jax.nn.gelu(gr_ref[...])
    h = h0_ref[...]
    for t in range(nt):
        sl = slice(t * nb, (t + 1) * nb)
        h = a[sl, :] * h + u[sl, :]
        o_ref[sl, :] = (h * gate[sl, :]).astype(BF16)
    hl_ref[...] = h


def _lru_sample(xp, gr, h0, cw, cb, wa, ba, wi, bi, lam, *, nb, nt):
    d = gr.shape[1]
    return pl.pallas_call(
        functools.partial(_lru_sample_kernel, nb=nb, nt=nt),
        out_shape=[jax.ShapeDtypeStruct((nt * nb, d), BF16), jax.ShapeDtypeStruct((nb, d), F32)],
        compiler_params=pltpu.CompilerParams(vmem_limit_bytes=VMEM_LIMIT),
        name="lru_sample",
    )(xp, gr, h0, cw, cb, wa, ba, wi, bi, lam)


def _score_keys(score):
    bits = lax.bitcast_convert_type(score + 0.0, I32)
    return bits ^ ((bits >> 31) & jnp.int32(0x7FFFFFFF))


def _count(key_ref, nc, rows, ck, thr, strict):
    thr_b = jnp.broadcast_to(thr, (rows, LANES))

    def body(c, acc):
        kc = key_ref[c]
        for t in range(ck // LANES):
            tile = kc[:, t * LANES:(t + 1) * LANES]
            hit = (tile > thr_b) if strict else (tile >= thr_b)
            acc = acc + jnp.where(hit, 1.0, 0.0)
        return acc

    acc = lax.fori_loop(0, nc, body, jnp.zeros((rows, LANES), F32))
    return jnp.sum(acc, axis=1, keepdims=True)


def _select_topk(key_ref, tau_ref, nc, rows, ck, k_sel):
    kf = float(k_sel)

    def bit_body(it, tu):
        cand = tu | jnp.left_shift(jnp.int32(1), 31 - it)
        cnt = _count(key_ref, nc, rows, ck, cand ^ jnp.int32(INT_MIN), strict=False)
        return jnp.where(cnt >= kf, cand, tu)

    tu = lax.fori_loop(0, 32, bit_body, jnp.zeros((rows, 1), I32))
    tau = tu ^ jnp.int32(INT_MIN)
    tau_ref[...] = jnp.broadcast_to(tau, (rows, LANES))
    n_gt = _count(key_ref, nc, rows, ck, tau, strict=True)
    n_ge = _count(key_ref, nc, rows, ck, tau, strict=False)
    need = kf - n_gt

    @pl.when(jnp.max(n_ge) > kf)
    def _():
        before = (lax.broadcasted_iota(I32, (ck, ck), 0) < lax.broadcasted_iota(I32, (ck, ck), 1))
        before = jnp.where(before, 1.0, 0.0).astype(BF16)
        tau_w = jnp.broadcast_to(tau, (rows, ck))
        need_w = jnp.broadcast_to(need, (rows, ck))

        def body(c, seen):
            kc = key_ref[c]
            eq = kc == tau_w
            eqf = jnp.where(eq, 1.0, 0.0)
            rank = _dot(eqf.astype(BF16), before) + seen
            key_ref[c] = jnp.where(eq, jnp.where(rank >= need_w, jnp.int32(INT_MIN), kc), kc)
            return seen + jnp.sum(eqf, axis=1, keepdims=True)

        lax.fori_loop(0, nc, body, jnp.zeros((rows, 1), F32))


def _prompt_attn_kernel(q_ref, qi_ref, kwq_ref, kwk_ref, kb_ref, vb_ref, o_ref,
                        ki_ref, key_ref, tau_ref, qs_ref, m_ref, l_ref, acc_ref, *, k_sel):
    i = pl.program_id(1)
    ck = key_ref.shape[2]
    nc = (i * Q_BLOCK + Q_BLOCK + ck - 1) // ck

    @pl.when(i == 0)
    def _():
        ki_ref[...] = kwk_ref[:, 0:IDX_DIM].astype(BF16)

    w_idx = kwq_ref[:, IDX_DIM:IDX_DIM + IDX_HEADS] * IDX_W_SCALE * IDX_SCALE
    t_pos = i * Q_BLOCK + lax.broadcasted_iota(I32, (Q_BLOCK, ck), 0)
    col = lax.broadcasted_iota(I32, (Q_BLOCK, ck), 1)

    def score_body(c, carry):
        c0 = pl.multiple_of(c * ck, ck)
        kc = ki_ref[pl.ds(c0, ck), :]
        score = jnp.zeros((Q_BLOCK, ck), F32)
        for h in range(IDX_HEADS):
            s = _dot_nt(qi_ref[:, h * IDX_DIM:(h + 1) * IDX_DIM], kc)
            score = score + jnp.maximum(s, 0.0) * w_idx[:, h:h + 1]
        key_ref[c] = jnp.where(col + c0 <= t_pos, _score_keys(score), jnp.int32(INT_MIN))
        return carry

    lax.fori_loop(0, nc, score_body, 0)

    need_select = (i + 1) * Q_BLOCK > k_sel

    @pl.when(jnp.logical_not(need_select))
    def _():
        tau_ref[...] = jnp.full(tau_ref.shape, INT_MIN + 1, I32)

    @pl.when(need_select)
    def _():
        _select_topk(key_ref, tau_ref, nc, Q_BLOCK, ck, k_sel)

    for n in range(N_KV_HEADS):
        for g in range(GQA_GROUP):
            hq = n * GQA_GROUP + g
            qs_ref[n, g * Q_BLOCK:(g + 1) * Q_BLOCK, :] = q_ref[:, hq * HEAD_DIM:(hq + 1) * HEAD_DIM]
    m_ref[...] = jnp.full(m_ref.shape, NEG_BIG, F32)
    l_ref[...] = jnp.zeros_like(l_ref)
    acc_ref[...] = jnp.zeros_like(acc_ref)
    tau_w = jnp.broadcast_to(tau_ref[:, 0:1], (Q_BLOCK, ck))

    def attn_body(c, carry):
        c0 = pl.multiple_of(c * ck, ck)
        bias = jnp.where(key_ref[c] >= tau_w, 0.0, NEG_BIG)
        bias = jnp.concatenate([bias] * GQA_GROUP, axis=0)
        for n in range(N_KV_HEADS):
            kn = kb_ref[pl.ds(c0, ck), n * HEAD_DIM:(n + 1) * HEAD_DIM]
            vn = vb_ref[pl.ds(c0, ck), n * HEAD_DIM:(n + 1) * HEAD_DIM]
            s = _dot_nt(qs_ref[n], kn) * ATTN_SCALE + bias
            m_prev = m_ref[n][:, 0:1]
            m_new = jnp.maximum(m_prev, jnp.max(s, axis=1, keepdims=True))
            alpha = jnp.exp(m_prev - m_new)
            p = jnp.exp(s - m_new)
            l_ref[n] = alpha * l_ref[n] + jnp.sum(p, axis=1, keepdims=True)
            acc_ref[n] = alpha * acc_ref[n] + _dot(p.astype(BF16), vn)
            m_ref[n] = jnp.broadcast_to(m_new, (GQA_GROUP * Q_BLOCK, LANES))
        return carry

    lax.fori_loop(0, nc, attn_body, 0)

    for n in range(N_KV_HEADS):
        o = acc_ref[n] / l_ref[n]
        for g in range(GQA_GROUP):
            hq = n * GQA_GROUP + g
            o_ref[:, hq * HEAD_DIM:(hq + 1) * HEAD_DIM] = o[g * Q_BLOCK:(g + 1) * Q_BLOCK, :].astype(BF16)


def _prompt_attn(q, qi, kw, kb, vb, *, batch):
    rows = q.shape[0]
    seq = rows // batch
    nblk = seq // Q_BLOCK
    ck = min(KEY_CHUNK, seq)
    k_sel = min(TOPK_MAX, seq // 4)
    gq = GQA_GROUP * Q_BLOCK
    blk = lambda b, i: (b * nblk + i, 0)
    per_b = lambda b, i: (b, 0)
    return pl.pallas_call(
        functools.partial(_prompt_attn_kernel, k_sel=k_sel),
        grid=(batch, nblk),
        in_specs=[
            pl.BlockSpec((Q_BLOCK, q.shape[1]), blk),
            pl.BlockSpec((Q_BLOCK, qi.shape[1]), blk),
            pl.BlockSpec((Q_BLOCK, LANES), blk),
            pl.BlockSpec((seq, LANES), per_b),
            pl.BlockSpec((seq, kb.shape[1]), per_b),
            pl.BlockSpec((seq, vb.shape[1]), per_b),
        ],
        out_specs=pl.BlockSpec((Q_BLOCK, q.shape[1]), blk),
        out_shape=jax.ShapeDtypeStruct(q.shape, BF16),
        scratch_shapes=[
            pltpu.VMEM((seq, IDX_DIM), BF16),
            pltpu.VMEM((seq // ck, Q_BLOCK, ck), I32),
            pltpu.VMEM((Q_BLOCK, LANES), I32),
            pltpu.VMEM((N_KV_HEADS, gq, HEAD_DIM), BF16),
            pltpu.VMEM((N_KV_HEADS, gq, LANES), F32),
            pltpu.VMEM((N_KV_HEADS, gq, LANES), F32),
            pltpu.VMEM((N_KV_HEADS, gq, HEAD_DIM), F32),
        ],
        compiler_params=_cparams(("parallel", "arbitrary")),
        name="prompt_attn",
    )(q, qi, kw, kw, kb, vb)


def _sample_scores_kernel(pt_ref, qi_ref, w_ref, kin_ref, *rest, nt, npg):
    pages = rest[:npg]
    past_ref, new_ref, kc_ref = rest[npg:]
    c = pl.program_id(1)
    w = w_ref[...] * IDX_W_SCALE * IDX_SCALE
    qi = qi_ref[...]

    def scores(keys_bf16):
        s = jnp.maximum(_dot_nt(qi, keys_bf16), 0.0)
        width = s.shape[1]
        s = s * jnp.concatenate([w] * (width // LANES), axis=1)
        return jnp.sum(s.reshape(nt, IDX_HEADS, width), axis=1)

    for p in range(npg):
        kc_ref[p * PAGE_SIZE:(p + 1) * PAGE_SIZE, :] = pages[p][...].astype(BF16)
    past_ref[...] = scores(kc_ref[...])

    @pl.when(c == 0)
    def _():
        new_ref[...] = scores(kin_ref[...].astype(BF16))


def _sample_scores(page_table, qi_b, w_b, ki_new, cache_ki, *, nt):
    nb, n_pages = page_table.shape
    npg = min(IDX_PAGES_PER_STEP, n_pages)
    rows = nt * IDX_HEADS
    page_specs = [
        pl.BlockSpec((None, PAGE_SIZE, IDX_DIM), functools.partial(lambda b, c, pt, p: (pt[b, c * npg + p], 0, 0), p=p))
        for p in range(npg)
    ]
    grid_spec = pltpu.PrefetchScalarGridSpec(
        num_scalar_prefetch=1,
        grid=(nb, n_pages // npg),
        in_specs=[
            pl.BlockSpec((None, rows, IDX_DIM), lambda b, c, pt: (b, 0, 0)),
            pl.BlockSpec((None, rows, LANES), lambda b, c, pt: (b, 0, 0)),
            pl.BlockSpec((None, PAGE_SIZE, IDX_DIM), lambda b, c, pt: (b, 0, 0)),
        ] + page_specs,
        out_specs=[
            pl.BlockSpec((None, nt, npg * PAGE_SIZE), lambda b, c, pt: (b, 0, c)),
            pl.BlockSpec((None, nt, PAGE_SIZE), lambda b, c, pt: (b, 0, 0)),
        ],
        scratch_shapes=[pltpu.VMEM((npg * PAGE_SIZE, IDX_DIM), BF16)],
    )
    return pl.pallas_call(
        functools.partial(_sample_scores_kernel, nt=nt, npg=npg),
        grid_spec=grid_spec,
        out_shape=[jax.ShapeDtypeStruct((nb, nt, n_pages * PAGE_SIZE), F32),
                   jax.ShapeDtypeStruct((nb, nt, PAGE_SIZE), F32)],
        compiler_params=_cparams(("parallel", "arbitrary")),
        name="sample_scores",
    )(page_table, qi_b, w_b, ki_new, *([cache_ki] * npg))


def _sample_select_kernel(past_ref, new_ref, bias_ref, key_ref, tau_ref, *, nt, k_sel, ck):
    rows, n_past = past_ref.shape
    npc = n_past // ck
    for c in range(npc):
        key_ref[c] = _score_keys(past_ref[:, c * ck:(c + 1) * ck])
    t_row = lax.broadcasted_iota(I32, (rows, ck), 0) % nt
    col = lax.broadcasted_iota(I32, (rows, ck), 1)
    new_keys = _score_keys(jnp.concatenate([new_ref[...]] * (ck // PAGE_SIZE), axis=1))
    key_ref[npc] = jnp.where(col <= t_row, new_keys, jnp.int32(INT_MIN))
    _select_topk(key_ref, tau_ref, npc + 1, rows, ck, k_sel)
    tau_w = jnp.broadcast_to(tau_ref[:, 0:1], (rows, ck))
    for c in range(npc + 1):
        width = ck if c < npc else PAGE_SIZE
        sel = key_ref[c][:, 0:width] >= tau_w[:, 0:width]
        bias_ref[:, c * ck:c * ck + width] = jnp.where(sel, 0.0, NEG_BIG)


def _sample_select(past, new, *, nt):
    rows, n_past = past.shape
    ck = min(KEY_CHUNK, n_past)
    k_sel = min(TOPK_MAX, (n_past + nt) // 4)
    return pl.pallas_call(
        functools.partial(_sample_select_kernel, nt=nt, k_sel=k_sel, ck=ck),
        out_shape=jax.ShapeDtypeStruct((rows, n_past + PAGE_SIZE), F32),
        scratch_shapes=[pltpu.VMEM((n_past // ck + 1, rows, ck), I32), pltpu.VMEM((rows, LANES), I32)],
        compiler_params=pltpu.CompilerParams(vmem_limit_bytes=VMEM_LIMIT),
        name="sample_select",
    )(past, new)


def _sample_attn_kernel(pt_ref, q_ref, bias_ref, biasn_ref, kn_ref, vn_ref, *rest, nt, npg):
    k_pages = rest[:npg]
    v_pages = rest[npg:2 * npg]
    o_ref, kc_ref, vc_ref, m_ref, l_ref, acc_ref = rest[2 * npg:]
    c = pl.program_id(1)

    @pl.when(c == 0)
    def _():
        m_ref[...] = jnp.full(m_ref.shape, NEG_BIG, F32)
        l_ref[...] = jnp.zeros_like(l_ref)
        acc_ref[...] = jnp.zeros_like(acc_ref)

    def update(k_all, v_all, bias):
        bias = jnp.concatenate([bias] * GQA_GROUP, axis=0)
        for n in range(N_KV_HEADS):
            sl = slice(n * HEAD_DIM, (n + 1) * HEAD_DIM)
            s = _dot_nt(q_ref[n], k_all[:, sl]) * ATTN_SCALE + bias
            m_prev = m_ref[n][:, 0:1]
            m_new = jnp.maximum(m_prev, jnp.max(s, axis=1, keepdims=True))
            alpha = jnp.exp(m_prev - m_new)
            p = jnp.exp(s - m_new)
            l_ref[n] = alpha * l_ref[n] + jnp.sum(p, axis=1, keepdims=True)
            acc_ref[n] = alpha * acc_ref[n] + _dot(p.astype(BF16), v_all[:, sl])
            m_ref[n] = jnp.broadcast_to(m_new, m_ref.shape[1:])

    for p in range(npg):
        kc_ref[p * PAGE_SIZE:(p + 1) * PAGE_SIZE, :] = k_pages[p][...].astype(BF16)
        vc_ref[p * PAGE_SIZE:(p + 1) * PAGE_SIZE, :] = v_pages[p][...].astype(BF16)
    update(kc_ref[...], vc_ref[...], bias_ref[...])

    @pl.when(c == pl.num_programs(1) - 1)
    def _():
        update(kn_ref[...].astype(BF16), vn_ref[...].astype(BF16), biasn_ref[...])
        for n in range(N_KV_HEADS):
            o_ref[n] = acc_ref[n] / l_ref[n]


def _sample_attn(page_table, q_b, bias, k_new, v_new, cache_k, cache_v, *, nt):
    nb, n_pages = page_table.shape
    npg = min(PAGES_PER_STEP, n_pages)
    d_kv = k_new.shape[2]
    rows = GQA_GROUP * nt
    n_past_blocks = n_pages // npg

    def page_spec(p):
        return pl.BlockSpec((None, PAGE_SIZE, d_kv),
                            functools.partial(lambda b, c, pt, p: (pt[b, c * npg + p], 0, 0), p=p))

    per_b3 = lambda b, c, pt: (b, 0, 0)
    grid_spec = pltpu.PrefetchScalarGridSpec(
        num_scalar_prefetch=1,
        grid=(nb, n_past_blocks),
        in_specs=[
            pl.BlockSpec((None, N_KV_HEADS, rows, HEAD_DIM), lambda b, c, pt: (b, 0, 0, 0)),
            pl.BlockSpec((None, nt, npg * PAGE_SIZE), lambda b, c, pt: (b, 0, c)),
            pl.BlockSpec((None, nt, PAGE_SIZE), lambda b, c, pt: (b, 0, n_pages)),
            pl.BlockSpec((None, PAGE_SIZE, d_kv), per_b3),
            pl.BlockSpec((None, PAGE_SIZE, d_kv), per_b3),
        ] + [page_spec(p) for p in range(npg)] + [page_spec(p) for p in range(npg)],
        out_specs=pl.BlockSpec((None, N_KV_HEADS, rows, HEAD_DIM), lambda b, c, pt: (b, 0, 0, 0)),
        scratch_shapes=[
            pltpu.VMEM((npg * PAGE_SIZE, d_kv), BF16),
            pltpu.VMEM((npg * PAGE_SIZE, d_kv), BF16),
            pltpu.VMEM((N_KV_HEADS, rows, LANES), F32),
            pltpu.VMEM((N_KV_HEADS, rows, LANES), F32),
            pltpu.VMEM((N_KV_HEADS, rows, HEAD_DIM), F32),
        ],
    )
    return pl.pallas_call(
        functools.partial(_sample_attn_kernel, nt=nt, npg=npg),
        grid_spec=grid_spec,
        out_shape=jax.ShapeDtypeStruct((nb, N_KV_HEADS, rows, HEAD_DIM), F32),
        compiler_params=_cparams(("parallel", "arbitrary")),
        name="sample_attn",
    )(page_table, q_b, bias, bias, k_new, v_new, *([cache_k] * npg), *([cache_v] * npg))


def _proj_out_kernel(x_ref, r_ref, a_ref, w_ref, g_ref, b_ref, o_ref, *, alpha):
    d_rnn = r_ref.shape[1]
    mix = _dot(r_ref[...], w_ref[0:d_rnn, :]) + _dot(a_ref[...], w_ref[d_rnn:, :])
    o_ref[...] = _layer_norm(alpha * x_ref[...] + mix, g_ref[...], b_ref[...])


def _proj_out_ln(x, rnn, attn, w, g, b, *, alpha, tm):
    rows, d = x.shape
    return pl.pallas_call(
        functools.partial(_proj_out_kernel, alpha=alpha),
        grid=(rows // tm,),
        in_specs=[
            pl.BlockSpec((tm, d), lambda i: (i, 0)),
            pl.BlockSpec((tm, rnn.shape[1]), lambda i: (i, 0)),
            pl.BlockSpec((tm, attn.shape[1]), lambda i: (i, 0)),
            pl.BlockSpec(w.shape, lambda i: (0, 0)),
            pl.BlockSpec((1, d), lambda i: (0, 0)),
            pl.BlockSpec((1, d), lambda i: (0, 0)),
        ],
        out_specs=pl.BlockSpec((tm, d), lambda i: (i, 0)),
        out_shape=jax.ShapeDtypeStruct((rows, d), F32),
        compiler_params=_cparams(("parallel",)),
        name="proj_out_ln",
    )(x, rnn, attn, w, g, b)


def _row_tile(rows, target):
    tm = min(rows, target)
    while rows % tm:
        tm //= 2
    return tm


def _pad_to(a, axis, size):
    pad = [(0, 0)] * a.ndim
    pad[axis] = (0, size - a.shape[axis])
    return jnp.pad(a, pad)


def _ffn_weights(w_gu, w_down):
    d_ff = w_down.shape[0]
    tf = 2 * MXU_DIM
    ffp = -(-d_ff // tf) * tf
    wg = _pad_to(w_gu[:, :d_ff].astype(BF16), 1, ffp)
    wu = _pad_to(w_gu[:, d_ff:].astype(BF16), 1, ffp)
    wd = _pad_to(w_down.astype(BF16), 0, ffp)
    return wg, wu, wd, tf


def kernel(x_prompt, x_sample, cache_k, cache_v, cache_k_idx, state_conv, state_rnn, page_table, ln1_g, ln1_b, ffn1_w_gu, ffn1_w_down, w_in, conv_w, conv_b, lru_w_a, lru_b_a, lru_w_i, lru_b_i, lru_lambda, w_out, ln2_g, ln2_b, ffn2_w_gu, ffn2_w_down, ln3_g, ln3_b):
    depth = w_in.shape[0]
    bp, seq, d_model = x_prompt.shape
    nb, nt, _ = x_sample.shape
    d_rnn = conv_w.shape[2]
    d_q = N_Q_HEADS * HEAD_DIM
    d_kv = N_KV_HEADS * HEAD_DIM
    d_qi = IDX_HEADS * IDX_DIM
    alpha = (2.0 * depth) ** 0.25
    n_pool = cache_k.shape[1]

    xp = x_prompt.reshape(bp * seq, d_model)
    xs = x_sample.transpose(1, 0, 2).reshape(nt * nb, d_model)
    tm_p = _row_tile(bp * seq, 512)
    tm_s = _row_tile(nt * nb, 512)
    vec = lambda a: a.reshape(1, -1)

    outs = [[] for _ in range(10)]
    for l in range(depth):
        wg1, wu1, wd1, tf = _ffn_weights(ffn1_w_gu[l], ffn1_w_down[l])
        wg2, wu2, wd2, _ = _ffn_weights(ffn2_w_gu[l], ffn2_w_down[l])
        d_in = w_in.shape[2]
        w_in_b = _pad_to(w_in[l].astype(BF16), 1, d_in - IDX_DIM - IDX_HEADS + LANES)
        w_out_b = w_out[l].astype(BF16)
        wa_b = lru_w_a[l].astype(BF16)
        wi_b = lru_w_i[l].astype(BF16)
        lru_args = (conv_w[l], vec(conv_b[l]), wa_b, vec(lru_b_a[l]), wi_b, vec(lru_b_i[l]), vec(lru_lambda[l]))
        proj = functools.partial(_proj_in, d_rnn=d_rnn, d_q=d_q, d_kv=d_kv, d_qi=d_qi)

        x1p = _ffn_ln(xp, wg1, wu1, wd1, vec(ln1_g[l]), vec(ln1_b[l]), alpha=alpha, tm=tm_p, tf=tf)
        x1s = _ffn_ln(xs, wg1, wu1, wd1, vec(ln1_g[l]), vec(ln1_b[l]), alpha=alpha, tm=tm_s, tf=tf)
        xr_p, gr_p, q_p, k_p, v_p, kb_p, vb_p, qi_p, kw_p = proj(x1p, w_in_b, tm=_row_tile(bp * seq, 256))
        xr_s, gr_s, q_s, k_s, v_s, _, _, qi_s, kw_s = proj(x1s, w_in_b, tm=_row_tile(nt * nb, 256))

        rnn_p, hl_p = _lru_prompt(xr_p, gr_p, *lru_args, batch=bp, tt=_row_tile(seq, 512))
        conv_tm = state_conv[l].transpose(1, 0, 2).reshape((CONV_W - 1) * nb, d_rnn)
        xpad_s = jnp.concatenate([conv_tm, xr_s], axis=0)
        rnn_s, hl_s = _lru_sample(xpad_s, gr_s, state_rnn[l], *lru_args, nb=nb, nt=nt)

        attn_p = _prompt_attn(q_p, qi_p, kw_p, kb_p, vb_p, batch=bp)

        to_b = lambda a: a.reshape(nt, nb, -1).transpose(1, 0, 2)
        qi_b = to_b(qi_s).reshape(nb, nt * IDX_HEADS, IDX_DIM)
        kw_b = to_b(kw_s)
        w_b = jnp.broadcast_to(kw_b[:, :, IDX_DIM:IDX_DIM + IDX_HEADS].reshape(nb, nt * IDX_HEADS, 1),
                               (nb, nt * IDX_HEADS, LANES))
        ki_new = _pad_to(kw_b[:, :, :IDX_DIM], 1, PAGE_SIZE)
        past_sc, new_sc = _sample_scores(page_table, qi_b, w_b, ki_new, cache_k_idx[l], nt=nt)
        n_past = past_sc.shape[2]
        bias = _sample_select(past_sc.reshape(nb * nt, n_past), new_sc.reshape(nb * nt, PAGE_SIZE), nt=nt)
        bias = bias.reshape(nb, nt, n_past + PAGE_SIZE)
        q_b = to_b(q_s).reshape(nb, nt, N_KV_HEADS, GQA_GROUP, HEAD_DIM)
        q_b = q_b.transpose(0, 2, 3, 1, 4).reshape(nb, N_KV_HEADS, GQA_GROUP * nt, HEAD_DIM)
        k_new = _pad_to(to_b(k_s), 1, PAGE_SIZE)
        v_new = _pad_to(to_b(v_s), 1, PAGE_SIZE)
        o_b = _sample_attn(page_table, q_b, bias, k_new, v_new,
                           cache_k[l].reshape(n_pool, PAGE_SIZE, d_kv), cache_v[l].reshape(n_pool, PAGE_SIZE, d_kv),
                           nt=nt)
        attn_s = o_b.reshape(nb, N_KV_HEADS, GQA_GROUP, nt, HEAD_DIM).transpose(3, 0, 1, 2, 4)
        attn_s = attn_s.reshape(nt * nb, d_q).astype(BF16)

        x2p = _proj_out_ln(x1p, rnn_p, attn_p, w_out_b, vec(ln2_g[l]), vec(ln2_b[l]), alpha=alpha, tm=tm_p)
        x2s = _proj_out_ln(x1s, rnn_s, attn_s, w_out_b, vec(ln2_g[l]), vec(ln2_b[l]), alpha=alpha, tm=tm_s)
        xp = _ffn_ln(x2p, wg2, wu2, wd2, vec(ln3_g[l]), vec(ln3_b[l]), alpha=alpha, tm=tm_p, tf=tf)
        xs = _ffn_ln(x2s, wg2, wu2, wd2, vec(ln3_g[l]), vec(ln3_b[l]), alpha=alpha, tm=tm_s, tf=tf)

        xr_p3 = xr_p.reshape(bp, seq, d_rnn)
        conv_p = xr_p3[:, -(CONV_W - 1):]
        conv_s = xpad_s.reshape(CONV_W - 1 + nt, nb, d_rnn)[-(CONV_W - 1):].transpose(1, 0, 2)
        layer_out = (
            k_p.reshape(bp, seq, N_KV_HEADS, HEAD_DIM), v_p.reshape(bp, seq, N_KV_HEADS, HEAD_DIM),
            kw_p[:, :IDX_DIM].reshape(bp, seq, IDX_DIM), conv_p, hl_p.reshape(bp, d_rnn),
            to_b(k_s).reshape(nb, nt, N_KV_HEADS, HEAD_DIM), to_b(v_s).reshape(nb, nt, N_KV_HEADS, HEAD_DIM),
            kw_b[:, :, :IDX_DIM], conv_s, hl_s,
        )
        for acc, val in zip(outs, layer_out):
            acc.append(val)

    y_p = xp.reshape(bp, seq, d_model)
    y_s = xs.reshape(nt, nb, d_model).transpose(1, 0, 2)
    return (y_p, y_s) + tuple(jnp.stack(o) for o in outs)
```

```python
import functools

import jax
import jax.numpy as jnp
from jax import lax
from jax.experimental import pallas as pl
from jax.experimental.pallas import tpu as pltpu

F32 = jnp.float32
BF16 = jnp.bfloat16
I32 = jnp.int32

LRU_BLOCKS = 8
CONV_W = 4
LRU_C = 8.0
HEAD_DIM = 128
N_KV_HEADS = 4
GQA_GROUP = 2
N_Q_HEADS = N_KV_HEADS * GQA_GROUP
IDX_HEADS = 8
IDX_DIM = 64
TOPK_MAX = 256
Q_BLOCK = 128
PAGE_SIZE = 128
LN_EPS = 1e-5
ATTN_SCALE = HEAD_DIM ** -0.5
SOFTMAX_LOG2_SCALE = ATTN_SCALE * 1.4426950408889634
IDX_SCALE = IDX_DIM ** -0.5
IDX_W_SCALE = IDX_HEADS ** -0.5

LANES = 128
SUBLANES = 8
MXU_DIM = 256
VMEM_LIMIT = 56 * 1024 * 1024

INT_MIN = -2 ** 31
FLT_LOWEST = -3.4028234663852886e38
NEG_BIG = -1e30
KEY_CHUNK = 512
PAGES_PER_STEP = 8
IDX_PAGES_PER_STEP = 16


def _cparams(semantics):
    return pltpu.CompilerParams(dimension_semantics=semantics, vmem_limit_bytes=VMEM_LIMIT)


def _layer_norm(y, g, b):
    mu = jnp.mean(y, axis=-1, keepdims=True)
    d = y - mu
    var = jnp.mean(d * d, axis=-1, keepdims=True)
    return d * lax.rsqrt(var + LN_EPS) * g + b


def _dot(a, b):
    return jnp.dot(a, b, preferred_element_type=F32)


def _dot_nt(a, b):
    return lax.dot_general(a, b, (((1,), (1,)), ((), ())), preferred_element_type=F32)


def _log2(n):
    assert n > 0 and n & (n - 1) == 0, n
    return n.bit_length() - 1


def _ffn_kernel(x_ref, wg_ref, wu_ref, wd_ref, g_ref, b_ref, o_ref, xb_ref, acc_ref, *, alpha):
    j = pl.program_id(1)

    @pl.when(j == 0)
    def _():
        xb_ref[...] = x_ref[...].astype(BF16)
        acc_ref[...] = jnp.zeros_like(acc_ref)

    xb = xb_ref[...]
    gate = _dot(xb, wg_ref[...])
    up = _dot(xb, wu_ref[...])
    act = (gate * jax.nn.sigmoid(gate) * up).astype(BF16)
    acc_ref[...] += _dot(act, wd_ref[...])

    @pl.when(j == pl.num_programs(1) - 1)
    def _():
        y = alpha * x_ref[...] + 0.5 * acc_ref[...]
        o_ref[...] = _layer_norm(y, g_ref[...], b_ref[...])


def _ffn_ln(x, wg, wu, wd, g, b, *, alpha, tm, tf):
    rows, d = x.shape
    ffp = wg.shape[1]
    return pl.pallas_call(
        functools.partial(_ffn_kernel, alpha=alpha),
        grid=(rows // tm, ffp // tf),
        in_specs=[
            pl.BlockSpec((tm, d), lambda i, j: (i, 0)),
            pl.BlockSpec((d, tf), lambda i, j: (0, j)),
            pl.BlockSpec((d, tf), lambda i, j: (0, j)),
            pl.BlockSpec((tf, d), lambda i, j: (j, 0)),
            pl.BlockSpec((1, d), lambda i, j: (0, 0)),
            pl.BlockSpec((1, d), lambda i, j: (0, 0)),
        ],
        out_specs=pl.BlockSpec((tm, d), lambda i, j: (i, 0)),
        out_shape=jax.ShapeDtypeStruct((rows, d), F32),
        scratch_shapes=[pltpu.VMEM((tm, d), BF16), pltpu.VMEM((tm, d), F32)],
        compiler_params=_cparams(("parallel", "arbitrary")),
        name="ffn_ln",
    )(x, wg, wu, wd, g, b)


def _proj_in_kernel(x_ref, w_ref, xr_ref, gr_ref, q_ref, k_ref, v_ref, kb_ref, vb_ref, qi_ref, kw_ref,
                    *, d_rnn, d_q, d_kv, d_qi):
    xb = x_ref[...].astype(BF16)
    off = [0]

    def seg(width):
        lo = off[0]
        off[0] = lo + width
        return _dot(xb, w_ref[:, lo:lo + width])

    xr_ref[...] = seg(d_rnn)
    gr_ref[...] = seg(d_rnn)
    q_ref[...] = seg(d_q).astype(BF16)
    k = seg(d_kv)
    k_ref[...] = k
    kb_ref[...] = k.astype(BF16)
    v = seg(d_kv)
    v_ref[...] = v
    vb_ref[...] = v.astype(BF16)
    qi_ref[...] = seg(d_qi).astype(BF16)
    kw_ref[...] = seg(LANES)


def _proj_in(x, w, *, tm, d_rnn, d_q, d_kv, d_qi):
    rows, d = x.shape
    widths = (d_rnn, d_rnn, d_q, d_kv, d_kv, d_kv, d_kv, d_qi, LANES)
    dtypes = (F32, F32, BF16, F32, F32, BF16, BF16, BF16, F32)
    return pl.pallas_call(
        functools.partial(_proj_in_kernel, d_rnn=d_rnn, d_q=d_q, d_kv=d_kv, d_qi=d_qi),
        grid=(rows // tm,),
        in_specs=[
            pl.BlockSpec((tm, d), lambda i: (i, 0)),
            pl.BlockSpec(w.shape, lambda i: (0, 0), pipeline_mode=pl.Buffered(1)),
        ],
        out_specs=[pl.BlockSpec((tm, wd), lambda i: (i, 0)) for wd in widths],
        out_shape=[jax.ShapeDtypeStruct((rows, wd), dt) for wd, dt in zip(widths, dtypes)],
        compiler_params=_cparams(("parallel",)),
        name="proj_in",
    )(x, w)


def _softplus(z):
    return jnp.maximum(z, 0.0) + jnp.log1p(jnp.exp(-jnp.abs(z)))


def _lru_gates(xc, wa_ref, ba, wi_ref, bi, lam):
    bw = xc.shape[1] // LRU_BLOCKS
    sp = _softplus(-lam)
    a_parts, u_parts = [], []
    for n in range(LRU_BLOCKS):
        sl = slice(n * bw, (n + 1) * bw)
        xn = xc[:, sl]
        xb = xn.astype(BF16)
        r = jax.nn.sigmoid(_dot(xb, wa_ref[n]) + ba[:, sl])
        i = jax.nn.sigmoid(_dot(xb, wi_ref[n]) + bi[:, sl])
        log_a = -LRU_C * r * sp[:, sl]
        a = jnp.exp(log_a)
        a_parts.append(a)
        u_parts.append(jnp.sqrt(-jnp.tanh(log_a) * (a * a + 1.0)) * i * xn)
    return jnp.concatenate(a_parts, axis=1), jnp.concatenate(u_parts, axis=1)


def _lru_prompt_kernel(xr_ref, gr_ref, cw_ref, cb_ref, wa_ref, ba_ref, wi_ref, bi_ref, lam_ref,
                       o_ref, hl_ref, xp_ref, a_ref, u_ref, hs_ref, hc_ref, *, tt):
    j = pl.program_id(1)
    d = xr_ref.shape[1]

    @pl.when(j == 0)
    def _():
        xp_ref[0:SUBLANES, :] = jnp.zeros((SUBLANES, d), F32)
        hc_ref[...] = jnp.zeros_like(hc_ref)

    @pl.when(j > 0)
    def _():
        xp_ref[0:SUBLANES, :] = xp_ref[tt:tt + SUBLANES, :]

    xp_ref[SUBLANES:SUBLANES + tt, :] = xr_ref[...]
    cw = cw_ref[...]
    xc = cb_ref[...]
    for jj in range(CONV_W):
        lo = SUBLANES - (CONV_W - 1) + jj
        xc = xc + cw[jj:jj + 1, :] * xp_ref[lo:lo + tt, :]

    a, u = _lru_gates(xc, wa_ref, ba_ref[...], wi_ref, bi_ref[...], lam_ref[...])
    a_ref[...] = a
    u_ref[...] = u

    row = lax.broadcasted_iota(I32, (SUBLANES, d), 0)

    def group(g, h):
        r0 = pl.multiple_of(g * SUBLANES, SUBLANES)
        a8 = a_ref[pl.ds(r0, SUBLANES), :]
        u8 = u_ref[pl.ds(r0, SUBLANES), :]
        out = jnp.zeros((SUBLANES, d), F32)
        for jj in range(SUBLANES):
            aj = jnp.broadcast_to(a8[jj:jj + 1, :], (SUBLANES, d))
            uj = jnp.broadcast_to(u8[jj:jj + 1, :], (SUBLANES, d))
            h = aj * h + uj
            out = jnp.where(row == jj, h, out)
        hs_ref[pl.ds(r0, SUBLANES), :] = out
        return h

    h = lax.fori_loop(0, tt // SUBLANES, group, hc_ref[...])
    hc_ref[...] = h
    hl_ref[...] = h[0:1, :]
    o_ref[...] = (hs_ref[...] * jax.nn.gelu(gr_ref[...])).astype(BF16)


def _lru_prompt(xr, gr, cw, cb, wa, ba, wi, bi, lam, *, batch, tt):
    rows, d = xr.shape
    seq = rows // batch
    nt = seq // tt
    row_spec = pl.BlockSpec((tt, d), lambda b, j: (b * nt + j, 0))
    vec_spec = pl.BlockSpec((1, d), lambda b, j: (0, 0))
    w_spec = pl.BlockSpec(wa.shape, lambda b, j: (0, 0, 0))
    return pl.pallas_call(
        functools.partial(_lru_prompt_kernel, tt=tt),
        grid=(batch, nt),
        in_specs=[row_spec, row_spec, pl.BlockSpec((CONV_W, d), lambda b, j: (0, 0)), vec_spec,
                  w_spec, vec_spec, w_spec, vec_spec, vec_spec],
        out_specs=[row_spec, pl.BlockSpec((None, 1, d), lambda b, j: (b, 0, 0))],
        out_shape=[jax.ShapeDtypeStruct((rows, d), BF16), jax.ShapeDtypeStruct((batch, 1, d), F32)],
        scratch_shapes=[pltpu.VMEM((tt + SUBLANES, d), F32), pltpu.VMEM((tt, d), F32),
                        pltpu.VMEM((tt, d), F32), pltpu.VMEM((tt, d), F32), pltpu.VMEM((SUBLANES, d), F32)],
        compiler_params=_cparams(("parallel", "arbitrary")),
        name="lru_prompt",
    )(xr, gr, cw, cb, wa, ba, wi, bi, lam)


def _lru_sample_kernel(xp_ref, gr_ref, h0_ref, cw_ref, cb_ref, wa_ref, ba_ref, wi_ref, bi_ref, lam_ref,
                       o_ref, hl_ref, *, nb, nt):
    cw = cw_ref[...]
    xc = cb_ref[...] + cw[0:1, :] * xp_ref[0:nt * nb, :]
    for jj in range(1, CONV_W):
        xc = xc + cw[jj:jj + 1, :] * xp_ref[jj * nb:(jj + nt) * nb, :]
    a, u = _lru_gates(xc, wa_ref, ba_ref[...], wi_ref, bi_ref[...], lam_ref[...])
    gate = jax.nn.gelu(gr_ref[...])
    h = h0_ref[...]
    for t in range(nt):
        sl = slice(t * nb, (t + 1) * nb)
        h = a[sl, :] * h + u[sl, :]
        o_ref[sl, :] = (h * gate[sl, :]).astype(BF16)
    hl_ref[...] = h


def _lru_sample(xp, gr, h0, cw, cb, wa, ba, wi, bi, lam, *, nb, nt):
    d = gr.shape[1]
    return pl.pallas_call(
        functools.partial(_lru_sample_kernel, nb=nb, nt=nt),
        out_shape=[jax.ShapeDtypeStruct((nt * nb, d), BF16), jax.ShapeDtypeStruct((nb, d), F32)],
        compiler_params=pltpu.CompilerParams(vmem_limit_bytes=VMEM_LIMIT),
        name="lru_sample",
    )(xp, gr, h0, cw, cb, wa, ba, wi, bi, lam)


def _rank_to_float(u):
    key = u ^ jnp.int32(INT_MIN)
    return lax.bitcast_convert_type(key ^ ((key >> 31) & jnp.int32(0x7FFFFFFF)), F32)


def _count(sc_ref, nc, rows, ck, thr, strict):
    thr_b = jnp.broadcast_to(thr, (rows, LANES))

    def body(c, acc):
        sc = sc_ref[c]
        for t in range(ck // LANES):
            tile = sc[:, t * LANES:(t + 1) * LANES]
            hit = (tile > thr_b) if strict else (tile >= thr_b)
            acc = acc + jnp.where(hit, 1.0, 0.0)
        return acc

    acc = lax.fori_loop(0, nc, body, jnp.zeros((rows, LANES), F32))
    return jnp.sum(acc, axis=1, keepdims=True)


def _select_topk(sc_ref, tau_ref, nc, rows, ck, k_sel):
    kf = float(k_sel)

    def bit_body(it, u):
        cand = u | jnp.left_shift(jnp.int32(1), 31 - it)
        cnt = _count(sc_ref, nc, rows, ck, _rank_to_float(cand), strict=False)
        return jnp.where(cnt >= kf, cand, u)

    tau = _rank_to_float(lax.fori_loop(0, 32, bit_body, jnp.zeros((rows, 1), I32)))
    tau_ref[...] = jnp.broadcast_to(tau, (rows, LANES))
    n_gt = _count(sc_ref, nc, rows, ck, tau, strict=True)
    n_ge = _count(sc_ref, nc, rows, ck, tau, strict=False)
    need = kf - n_gt

    @pl.when(jnp.max(n_ge) > kf)
    def _():
        before = (lax.broadcasted_iota(I32, (ck, ck), 0) < lax.broadcasted_iota(I32, (ck, ck), 1))
        before = jnp.where(before, 1.0, 0.0).astype(BF16)
        tau_w = jnp.broadcast_to(tau, (rows, ck))
        need_w = jnp.broadcast_to(need, (rows, ck))

        def body(c, seen):
            sc = sc_ref[c]
            eq = sc == tau_w
            eqf = jnp.where(eq, 1.0, 0.0)
            rank = _dot(eqf.astype(BF16), before) + seen
            sc_ref[c] = jnp.where(eq, jnp.where(rank >= need_w, -jnp.inf, sc), sc)
            return seen + jnp.sum(eqf, axis=1, keepdims=True)

        lax.fori_loop(0, nc, body, jnp.zeros((rows, 1), F32))


def _prompt_attn_kernel(q_ref, qi_ref, kwq_ref, kwk_ref, kb_ref, vb_ref, o_ref,
                        ki_ref, sc_ref, tau_ref, qs_ref, m_ref, l_ref, acc_ref, *, k_sel):
    i = pl.program_id(1)
    ck = sc_ref.shape[2]
    nc = (i * Q_BLOCK + Q_BLOCK + ck - 1) // ck

    @pl.when(i == 0)
    def _():
        ki_ref[...] = kwk_ref[:, 0:IDX_DIM].astype(BF16)

    w_idx = kwq_ref[:, IDX_DIM:IDX_DIM + IDX_HEADS] * IDX_W_SCALE * IDX_SCALE
    t_pos = i * Q_BLOCK + lax.broadcasted_iota(I32, (Q_BLOCK, ck), 0)
    col = lax.broadcasted_iota(I32, (Q_BLOCK, ck), 1)

    def score_body(c, carry):
        c0 = pl.multiple_of(c * ck, ck)
        kc = ki_ref[pl.ds(c0, ck), :]
        score = jnp.zeros((Q_BLOCK, ck), F32)
        for h in range(IDX_HEADS):
            s = _dot_nt(qi_ref[:, h * IDX_DIM:(h + 1) * IDX_DIM], kc)
            score = score + jnp.maximum(s, 0.0) * w_idx[:, h:h + 1]
        sc_ref[c] = jnp.where(col + c0 <= t_pos, score, -jnp.inf)
        return carry

    lax.fori_loop(0, nc, score_body, 0)

    need_select = (i + 1) * Q_BLOCK > k_sel

    @pl.when(jnp.logical_not(need_select))
    def _():
        tau_ref[...] = jnp.full(tau_ref.shape, FLT_LOWEST, F32)

    @pl.when(need_select)
    def _():
        _select_topk(sc_ref, tau_ref, nc, Q_BLOCK, ck, k_sel)

    for n in range(N_KV_HEADS):
        for g in range(GQA_GROUP):
            hq = n * GQA_GROUP + g
            qs_ref[n, g * Q_BLOCK:(g + 1) * Q_BLOCK, :] = q_ref[:, hq * HEAD_DIM:(hq + 1) * HEAD_DIM]
    m_ref[...] = jnp.full(m_ref.shape, NEG_BIG, F32)
    l_ref[...] = jnp.zeros_like(l_ref)
    acc_ref[...] = jnp.zeros_like(acc_ref)
    tau_w = jnp.broadcast_to(tau_ref[:, 0:1], (Q_BLOCK, ck))

    def attn_body(c, carry):
        c0 = pl.multiple_of(c * ck, ck)
        bias = jnp.where(sc_ref[c] >= tau_w, 0.0, NEG_BIG)
        bias = jnp.concatenate([bias] * GQA_GROUP, axis=0)
        for n in range(N_KV_HEADS):
            kn = kb_ref[pl.ds(c0, ck), n * HEAD_DIM:(n + 1) * HEAD_DIM]
            vn = vb_ref[pl.ds(c0, ck), n * HEAD_DIM:(n + 1) * HEAD_DIM]
            s = _dot_nt(qs_ref[n], kn) + bias
            m_prev = m_ref[n][:, 0:1]
            m_new = jnp.maximum(m_prev, jnp.max(s, axis=1, keepdims=True))
            alpha = jnp.exp2((m_prev - m_new) * SOFTMAX_LOG2_SCALE)
            p = jnp.exp2((s - m_new) * SOFTMAX_LOG2_SCALE)
            l_ref[n] = alpha * l_ref[n] + jnp.sum(p, axis=1, keepdims=True)
            acc_ref[n] = alpha * acc_ref[n] + _dot(p.astype(BF16), vn)
            m_ref[n] = jnp.broadcast_to(m_new, (GQA_GROUP * Q_BLOCK, LANES))
        return carry

    lax.fori_loop(0, nc, attn_body, 0)

    for n in range(N_KV_HEADS):
        o = acc_ref[n] / l_ref[n]
        for g in range(GQA_GROUP):
            hq = n * GQA_GROUP + g
            o_ref[:, hq * HEAD_DIM:(hq + 1) * HEAD_DIM] = o[g * Q_BLOCK:(g + 1) * Q_BLOCK, :].astype(BF16)


def _prompt_attn(q, qi, kw, kb, vb, *, batch):
    rows = q.shape[0]
    seq = rows // batch
    nblk = seq // Q_BLOCK
    ck = min(KEY_CHUNK, seq)
    k_sel = min(TOPK_MAX, seq // 4)
    gq = GQA_GROUP * Q_BLOCK
    blk = lambda b, i: (b * nblk + i, 0)
    per_b = lambda b, i: (b, 0)
    return pl.pallas_call(
        functools.partial(_prompt_attn_kernel, k_sel=k_sel),
        grid=(batch, nblk),
        in_specs=[
            pl.BlockSpec((Q_BLOCK, q.shape[1]), blk),
            pl.BlockSpec((Q_BLOCK, qi.shape[1]), blk),
            pl.BlockSpec((Q_BLOCK, LANES), blk),
            pl.BlockSpec((seq, LANES), per_b),
            pl.BlockSpec((seq, kb.shape[1]), per_b),
            pl.BlockSpec((seq, vb.shape[1]), per_b),
        ],
        out_specs=pl.BlockSpec((Q_BLOCK, q.shape[1]), blk),
        out_shape=jax.ShapeDtypeStruct(q.shape, BF16),
        scratch_shapes=[
            pltpu.VMEM((seq, IDX_DIM), BF16),
            pltpu.VMEM((seq // ck, Q_BLOCK, ck), F32),
            pltpu.VMEM((Q_BLOCK, LANES), F32),
            pltpu.VMEM((N_KV_HEADS, gq, HEAD_DIM), BF16),
            pltpu.VMEM((N_KV_HEADS, gq, LANES), F32),
            pltpu.VMEM((N_KV_HEADS, gq, LANES), F32),
            pltpu.VMEM((N_KV_HEADS, gq, HEAD_DIM), F32),
        ],
        compiler_params=_cparams(("parallel", "arbitrary")),
        name="prompt_attn",
    )(q, qi, kw, kw, kb, vb)


def _sample_scores_kernel(pt_ref, qi_ref, w_ref, kin_ref, *rest, nt, npg):
    pages = rest[:npg]
    past_ref, new_ref, kc_ref = rest[npg:]
    c = pl.program_id(1)
    w = w_ref[...] * IDX_W_SCALE * IDX_SCALE
    qi = qi_ref[...]

    def scores(keys_bf16):
        s = jnp.maximum(_dot_nt(qi, keys_bf16), 0.0)
        width = s.shape[1]
        s = s * jnp.concatenate([w] * (width // LANES), axis=1)
        return jnp.sum(s.reshape(nt, IDX_HEADS, width), axis=1)

    for p in range(npg):
        kc_ref[p * PAGE_SIZE:(p + 1) * PAGE_SIZE, :] = pages[p][...].astype(BF16)
    past_ref[...] = scores(kc_ref[...])

    @pl.when(c == 0)
    def _():
        new_ref[...] = scores(kin_ref[...].astype(BF16))


def _sample_scores(page_table, qi_b, w_b, ki_new, cache_ki, *, layer, nt):
    nb, n_pages = page_table.shape
    npg = min(IDX_PAGES_PER_STEP, n_pages)
    rows = nt * IDX_HEADS
    page_specs = [
        pl.BlockSpec((None, None, PAGE_SIZE, IDX_DIM),
                     functools.partial(lambda b, c, pt, p: (layer, pt[b, c * npg + p], 0, 0), p=p))
        for p in range(npg)
    ]
    grid_spec = pltpu.PrefetchScalarGridSpec(
        num_scalar_prefetch=1,
        grid=(nb, n_pages // npg),
        in_specs=[
            pl.BlockSpec((None, rows, IDX_DIM), lambda b, c, pt: (b, 0, 0)),
            pl.BlockSpec((None, rows, LANES), lambda b, c, pt: (b, 0, 0)),
            pl.BlockSpec((None, PAGE_SIZE, IDX_DIM), lambda b, c, pt: (b, 0, 0)),
        ] + page_specs,
        out_specs=[
            pl.BlockSpec((None, nt, npg * PAGE_SIZE), lambda b, c, pt: (b, 0, c)),
            pl.BlockSpec((None, nt, PAGE_SIZE), lambda b, c, pt: (b, 0, 0)),
        ],
        scratch_shapes=[pltpu.VMEM((npg * PAGE_SIZE, IDX_DIM), BF16)],
    )
    return pl.pallas_call(
        functools.partial(_sample_scores_kernel, nt=nt, npg=npg),
        grid_spec=grid_spec,
        out_shape=[jax.ShapeDtypeStruct((nb, nt, n_pages * PAGE_SIZE), F32),
                   jax.ShapeDtypeStruct((nb, nt, PAGE_SIZE), F32)],
        compiler_params=_cparams(("parallel", "arbitrary")),
        name="sample_scores",
    )(page_table, qi_b, w_b, ki_new, *([cache_ki] * npg))


def _sample_select_kernel(past_ref, new_ref, bias_ref, sc_ref, tau_ref, *, nt, k_sel, ck):
    rows, n_past = past_ref.shape
    npc = n_past // ck
    for c in range(npc):
        sc_ref[c] = past_ref[:, c * ck:(c + 1) * ck]
    t_row = lax.broadcasted_iota(I32, (rows, ck), 0) & (nt - 1)
    col = lax.broadcasted_iota(I32, (rows, ck), 1)
    new_sc = jnp.concatenate([new_ref[...]] * (ck // PAGE_SIZE), axis=1)
    sc_ref[npc] = jnp.where(col <= t_row, new_sc, -jnp.inf)
    _select_topk(sc_ref, tau_ref, npc + 1, rows, ck, k_sel)
    tau_b = tau_ref[...]
    d = (lax.broadcasted_iota(I32, (LANES, LANES * N_KV_HEADS), 1)
         - N_KV_HEADS * lax.broadcasted_iota(I32, (LANES, LANES * N_KV_HEADS), 0))
    expand = jnp.where(d >= 0, jnp.where(d < N_KV_HEADS, 1.0, 0.0), 0.0).astype(BF16)
    for c in range(npc + 1):
        for t in range(ck // LANES if c < npc else PAGE_SIZE // LANES):
            sel = jnp.where(sc_ref[c][:, t * LANES:(t + 1) * LANES] >= tau_b, 1.0, 0.0).astype(BF16)
            lo = (c * ck + t * LANES) * N_KV_HEADS
            bias_ref[:, lo:lo + LANES * N_KV_HEADS] = (_dot(sel, expand) - 1.0) * (-NEG_BIG)


def _sample_select(past, new, *, nt):
    rows, n_past = past.shape
    assert nt & (nt - 1) == 0
    ck = min(KEY_CHUNK, n_past)
    k_sel = min(TOPK_MAX, (n_past + nt) // 4)
    return pl.pallas_call(
        functools.partial(_sample_select_kernel, nt=nt, k_sel=k_sel, ck=ck),
        out_shape=jax.ShapeDtypeStruct((rows, (n_past + PAGE_SIZE) * N_KV_HEADS), F32),
        scratch_shapes=[pltpu.VMEM((n_past // ck + 1, rows, ck), F32), pltpu.VMEM((rows, LANES), F32)],
        compiler_params=pltpu.CompilerParams(vmem_limit_bytes=VMEM_LIMIT),
        name="sample_select",
    )(past, new)


def _sample_attn_kernel(pt_ref, q_ref, bias_ref, biasn_ref, kn_ref, vn_ref, *rest, nt, npg):
    k_pages = rest[:npg]
    v_pages = rest[npg:2 * npg]
    o_ref, kc_ref, vc_ref, m_ref, l_ref, acc_ref = rest[2 * npg:]
    c = pl.program_id(1)
    rows = q_ref.shape[0]
    page_rows = PAGE_SIZE * N_KV_HEADS

    @pl.when(c == 0)
    def _():
        m_ref[...] = jnp.full(m_ref.shape, NEG_BIG, F32)
        l_ref[...] = jnp.zeros_like(l_ref)
        acc_ref[...] = jnp.zeros_like(acc_ref)

    row_head = lax.shift_right_logical(lax.broadcasted_iota(I32, (rows, page_rows), 0), _log2(rows // N_KV_HEADS))
    col_head = lax.broadcasted_iota(I32, (rows, page_rows), 1) & (N_KV_HEADS - 1)
    head_bias = jnp.where(row_head == col_head, 0.0, NEG_BIG)

    def update(k2, v2, bias):
        width = k2.shape[0]
        b = jnp.concatenate([bias] * (rows // nt), axis=0) + jnp.concatenate([head_bias] * (width // page_rows), axis=1)
        s = _dot_nt(q_ref[...], k2) * ATTN_SCALE + b
        m_prev = m_ref[:, 0:1]
        m_new = jnp.maximum(m_prev, jnp.max(s, axis=1, keepdims=True))
        alpha = jnp.exp(m_prev - m_new)
        p = jnp.exp(s - m_new)
        l_ref[...] = alpha * l_ref[...] + jnp.sum(p, axis=1, keepdims=True)
        acc_ref[...] = alpha * acc_ref[...] + _dot(p.astype(BF16), v2)
        m_ref[...] = jnp.broadcast_to(m_new, m_ref.shape)

    for p in range(npg):
        kc_ref[p * page_rows:(p + 1) * page_rows, :] = k_pages[p][...].reshape(page_rows, HEAD_DIM).astype(BF16)
        vc_ref[p * page_rows:(p + 1) * page_rows, :] = v_pages[p][...].reshape(page_rows, HEAD_DIM).astype(BF16)
    update(kc_ref[...], vc_ref[...], bias_ref[...])

    @pl.when(c == pl.num_programs(1) - 1)
    def _():
        update(kn_ref[...].astype(BF16), vn_ref[...].astype(BF16), biasn_ref[...])
        o_ref[...] = acc_ref[...] / l_ref[...]


def _sample_attn(page_table, q_b, bias, k_new, v_new, cache_k, cache_v, *, layer, nt):
    nb, n_pages = page_table.shape
    npg = min(PAGES_PER_STEP, n_pages)
    rows = q_b.shape[1]
    page_rows = PAGE_SIZE * N_KV_HEADS
    assert N_KV_HEADS & (N_KV_HEADS - 1) == 0

    def page_spec(p):
        return pl.BlockSpec((None, None, PAGE_SIZE, N_KV_HEADS, HEAD_DIM),
                            functools.partial(lambda b, c, pt, p: (layer, pt[b, c * npg + p], 0, 0, 0), p=p))

    per_b3 = lambda b, c, pt: (b, 0, 0)
    grid_spec = pltpu.PrefetchScalarGridSpec(
        num_scalar_prefetch=1,
        grid=(nb, n_pages // npg),
        in_specs=[
            pl.BlockSpec((None, rows, HEAD_DIM), per_b3),
            pl.BlockSpec((None, nt, npg * page_rows), lambda b, c, pt: (b, 0, c)),
            pl.BlockSpec((None, nt, page_rows), lambda b, c, pt: (b, 0, n_pages)),
            pl.BlockSpec((None, page_rows, HEAD_DIM), per_b3),
            pl.BlockSpec((None, page_rows, HEAD_DIM), per_b3),
        ] + [page_spec(p) for p in range(npg)] + [page_spec(p) for p in range(npg)],
        out_specs=pl.BlockSpec((None, rows, HEAD_DIM), per_b3),
        scratch_shapes=[
            pltpu.VMEM((npg * page_rows, HEAD_DIM), BF16),
            pltpu.VMEM((npg * page_rows, HEAD_DIM), BF16),
            pltpu.VMEM((rows, LANES), F32),
            pltpu.VMEM((rows, LANES), F32),
            pltpu.VMEM((rows, HEAD_DIM), F32),
        ],
    )
    return pl.pallas_call(
        functools.partial(_sample_attn_kernel, nt=nt, npg=npg),
        grid_spec=grid_spec,
        out_shape=jax.ShapeDtypeStruct((nb, rows, HEAD_DIM), F32),
        compiler_params=_cparams(("parallel", "arbitrary")),
        name="sample_attn",
    )(page_table, q_b, bias, bias, k_new, v_new, *([cache_k] * npg), *([cache_v] * npg))


def _proj_out_kernel(x_ref, r_ref, a_ref, w_ref, g_ref, b_ref, o_ref, *, alpha):
    d_rnn = r_ref.shape[1]
    mix = _dot(r_ref[...], w_ref[0:d_rnn, :]) + _dot(a_ref[...], w_ref[d_rnn:, :])
    o_ref[...] = _layer_norm(alpha * x_ref[...] + mix, g_ref[...], b_ref[...])


def _proj_out_ln(x, rnn, attn, w, g, b, *, alpha, tm):
    rows, d = x.shape
    return pl.pallas_call(
        functools.partial(_proj_out_kernel, alpha=alpha),
        grid=(rows // tm,),
        in_specs=[
            pl.BlockSpec((tm, d), lambda i: (i, 0)),
            pl.BlockSpec((tm, rnn.shape[1]), lambda i: (i, 0)),
            pl.BlockSpec((tm, attn.shape[1]), lambda i: (i, 0)),
            pl.BlockSpec(w.shape, lambda i: (0, 0)),
            pl.BlockSpec((1, d), lambda i: (0, 0)),
            pl.BlockSpec((1, d), lambda i: (0, 0)),
        ],
        out_specs=pl.BlockSpec((tm, d), lambda i: (i, 0)),
        out_shape=jax.ShapeDtypeStruct((rows, d), F32),
        compiler_params=_cparams(("parallel",)),
        name="proj_out_ln",
    )(x, rnn, attn, w, g, b)


def _row_tile(rows, target):
    tm = min(rows, target)
    while rows % tm:
        tm //= 2
    return tm


def _pad_to(a, axis, size):
    pad = [(0, 0)] * a.ndim
    pad[axis] = (0, size - a.shape[axis])
    return jnp.pad(a, pad)


def _ffn_weights(w_gu, w_down):
    d_ff = w_down.shape[0]
    tf = 2 * MXU_DIM
    ffp = -(-d_ff // tf) * tf
    wg = _pad_to(w_gu[:, :d_ff].astype(BF16), 1, ffp)
    wu = _pad_to(w_gu[:, d_ff:].astype(BF16), 1, ffp)
    wd = _pad_to(w_down.astype(BF16), 0, ffp)
    return wg, wu, wd, tf


def kernel(x_prompt, x_sample, cache_k, cache_v, cache_k_idx, state_conv, state_rnn, page_table, ln1_g, ln1_b, ffn1_w_gu, ffn1_w_down, w_in, conv_w, conv_b, lru_w_a, lru_b_a, lru_w_i, lru_b_i, lru_lambda, w_out, ln2_g, ln2_b, ffn2_w_gu, ffn2_w_down, ln3_g, ln3_b):
    depth = w_in.shape[0]
    bp, seq, d_model = x_prompt.shape
    nb, nt, _ = x_sample.shape
    d_rnn = conv_w.shape[2]
    d_q = N_Q_HEADS * HEAD_DIM
    d_kv = N_KV_HEADS * HEAD_DIM
    d_qi = IDX_HEADS * IDX_DIM
    alpha = (2.0 * depth) ** 0.25

    xp = x_prompt.reshape(bp * seq, d_model)
    xs = x_sample.transpose(1, 0, 2).reshape(nt * nb, d_model)
    tm_p = _row_tile(bp * seq, 512)
    tm_s = _row_tile(nt * nb, 512)
    vec = lambda a: a.reshape(1, -1)

    outs = [[] for _ in range(10)]
    for l in range(depth):
        wg1, wu1, wd1, tf = _ffn_weights(ffn1_w_gu[l], ffn1_w_down[l])
        wg2, wu2, wd2, _ = _ffn_weights(ffn2_w_gu[l], ffn2_w_down[l])
        d_in = w_in.shape[2]
        w_in_b = _pad_to(w_in[l].astype(BF16), 1, d_in - IDX_DIM - IDX_HEADS + LANES)
        w_out_b = w_out[l].astype(BF16)
        wa_b = lru_w_a[l].astype(BF16)
        wi_b = lru_w_i[l].astype(BF16)
        lru_args = (conv_w[l], vec(conv_b[l]), wa_b, vec(lru_b_a[l]), wi_b, vec(lru_b_i[l]), vec(lru_lambda[l]))
        proj = functools.partial(_proj_in, d_rnn=d_rnn, d_q=d_q, d_kv=d_kv, d_qi=d_qi)

        x1p = _ffn_ln(xp, wg1, wu1, wd1, vec(ln1_g[l]), vec(ln1_b[l]), alpha=alpha, tm=tm_p, tf=tf)
        x1s = _ffn_ln(xs, wg1, wu1, wd1, vec(ln1_g[l]), vec(ln1_b[l]), alpha=alpha, tm=tm_s, tf=tf)
        xr_p, gr_p, q_p, k_p, v_p, kb_p, vb_p, qi_p, kw_p = proj(x1p, w_in_b, tm=_row_tile(bp * seq, 256))
        xr_s, gr_s, q_s, k_s, v_s, _, _, qi_s, kw_s = proj(x1s, w_in_b, tm=_row_tile(nt * nb, 256))

        rnn_p, hl_p = _lru_prompt(xr_p, gr_p, *lru_args, batch=bp, tt=_row_tile(seq, 512))
        conv_tm = state_conv[l].transpose(1, 0, 2).reshape((CONV_W - 1) * nb, d_rnn)
        xpad_s = jnp.concatenate([conv_tm, xr_s], axis=0)
        rnn_s, hl_s = _lru_sample(xpad_s, gr_s, state_rnn[l], *lru_args, nb=nb, nt=nt)

        attn_p = _prompt_attn(q_p, qi_p, kw_p, kb_p, vb_p, batch=bp)

        to_b = lambda a: a.reshape(nt, nb, -1).transpose(1, 0, 2)
        qi_b = to_b(qi_s).reshape(nb, nt * IDX_HEADS, IDX_DIM)
        kw_b = to_b(kw_s)
        w_b = jnp.broadcast_to(kw_b[:, :, IDX_DIM:IDX_DIM + IDX_HEADS].reshape(nb, nt * IDX_HEADS, 1),
                               (nb, nt * IDX_HEADS, LANES))
        ki_new = _pad_to(kw_b[:, :, :IDX_DIM], 1, PAGE_SIZE)
        past_sc, new_sc = _sample_scores(page_table, qi_b, w_b, ki_new, cache_k_idx, layer=l, nt=nt)
        n_past = past_sc.shape[2]
        bias = _sample_select(past_sc.reshape(nb * nt, n_past), new_sc.reshape(nb * nt, PAGE_SIZE), nt=nt)
        bias = bias.reshape(nb, nt, (n_past + PAGE_SIZE) * N_KV_HEADS)
        q_b = to_b(q_s).reshape(nb, nt, N_KV_HEADS, GQA_GROUP, HEAD_DIM)
        q_b = q_b.transpose(0, 2, 3, 1, 4).reshape(nb, N_Q_HEADS * nt, HEAD_DIM)
        new_page = lambda a: _pad_to(to_b(a).reshape(nb, nt * N_KV_HEADS, HEAD_DIM), 1, PAGE_SIZE * N_KV_HEADS)
        o_b = _sample_attn(page_table, q_b, bias, new_page(k_s), new_page(v_s), cache_k, cache_v, layer=l, nt=nt)
        attn_s = o_b.reshape(nb, N_KV_HEADS, GQA_GROUP, nt, HEAD_DIM).transpose(3, 0, 1, 2, 4)
        attn_s = attn_s.reshape(nt * nb, d_q).astype(BF16)

        x2p = _proj_out_ln(x1p, rnn_p, attn_p, w_out_b, vec(ln2_g[l]), vec(ln2_b[l]), alpha=alpha, tm=tm_p)
        x2s = _proj_out_ln(x1s, rnn_s, attn_s, w_out_b, vec(ln2_g[l]), vec(ln2_b[l]), alpha=alpha, tm=tm_s)
        xp = _ffn_ln(x2p, wg2, wu2, wd2, vec(ln3_g[l]), vec(ln3_b[l]), alpha=alpha, tm=tm_p, tf=tf)
        xs = _ffn_ln(x2s, wg2, wu2, wd2, vec(ln3_g[l]), vec(ln3_b[l]), alpha=alpha, tm=tm_s, tf=tf)

        xr_p3 = xr_p.reshape(bp, seq, d_rnn)
        conv_p = xr_p3[:, -(CONV_W - 1):]
        conv_s = xpad_s.reshape(CONV_W - 1 + nt, nb, d_rnn)[-(CONV_W - 1):].transpose(1, 0, 2)
        layer_out = (
            k_p.reshape(bp, seq, N_KV_HEADS, HEAD_DIM), v_p.reshape(bp, seq, N_KV_HEADS, HEAD_DIM),
            kw_p[:, :IDX_DIM].reshape(bp, seq, IDX_DIM), conv_p, hl_p.reshape(bp, d_rnn),
            to_b(k_s).reshape(nb, nt, N_KV_HEADS, HEAD_DIM), to_b(v_s).reshape(nb, nt, N_KV_HEADS, HEAD_DIM),
            kw_b[:, :, :IDX_DIM], conv_s, hl_s,
        )
        for acc, val in zip(outs, layer_out):
            acc.append(val)

    y_p = xp.reshape(bp, seq, d_model)
    y_s = xs.reshape(nt, nb, d_model).transpose(1, 0, 2)
    return (y_p, y_s) + tuple(jnp.stack(o) for o in outs)
```

```python
import functools

import jax
import jax.numpy as jnp
from jax import lax
from jax.experimental import pallas as pl
from jax.experimental.pallas import tpu as pltpu

F32 = jnp.float32
BF16 = jnp.bfloat16
I32 = jnp.int32

LRU_BLOCKS = 8
CONV_W = 4
LRU_C = 8.0
HEAD_DIM = 128
N_KV_HEADS = 4
GQA_GROUP = 2
N_Q_HEADS = N_KV_HEADS * GQA_GROUP
IDX_HEADS = 8
IDX_DIM = 64
TOPK_MAX = 256
Q_BLOCK = 128
PAGE_SIZE = 128
LN_EPS = 1e-5
ATTN_SCALE = HEAD_DIM ** -0.5
SOFTMAX_LOG2_SCALE = ATTN_SCALE * 1.4426950408889634
IDX_SCALE = IDX_DIM ** -0.5
IDX_W_SCALE = IDX_HEADS ** -0.5

LANES = 128
SUBLANES = 8
MXU_DIM = 256
VMEM_LIMIT = 56 * 1024 * 1024

INT_MIN = -2 ** 31
FLT_LOWEST = -3.4028234663852886e38
NEG_BIG = -1e30
KEY_CHUNK = 512
PAGES_PER_STEP = 16
IDX_PAGES_PER_STEP = 16


def _cparams(semantics):
    return pltpu.CompilerParams(dimension_semantics=semantics, vmem_limit_bytes=VMEM_LIMIT)


def _layer_norm(y, g, b):
    mu = jnp.mean(y, axis=-1, keepdims=True)
    d = y - mu
    var = jnp.mean(d * d, axis=-1, keepdims=True)
    return d * lax.rsqrt(var + LN_EPS) * g + b


def _dot(a, b):
    return jnp.dot(a, b, preferred_element_type=F32)


def _dot_nt(a, b):
    return lax.dot_general(a, b, (((1,), (1,)), ((), ())), preferred_element_type=F32)


def _log2(n):
    assert n > 0 and n & (n - 1) == 0, n
    return n.bit_length() - 1


def _ffn_kernel(x_ref, wg_ref, wu_ref, wd_ref, g_ref, b_ref, o_ref, xb_ref, acc_ref, *, alpha):
    j = pl.program_id(1)

    @pl.when(j == 0)
    def _():
        xb_ref[...] = x_ref[...].astype(BF16)
        acc_ref[...] = jnp.zeros_like(acc_ref)

    xb = xb_ref[...]
    gate = _dot(xb, wg_ref[...])
    up = _dot(xb, wu_ref[...])
    act = (gate * jax.nn.sigmoid(gate) * up).astype(BF16)
    acc_ref[...] += _dot(act, wd_ref[...])

    @pl.when(j == pl.num_programs(1) - 1)
    def _():
        y = alpha * x_ref[...] + 0.5 * acc_ref[...]
        o_ref[...] = _layer_norm(y, g_ref[...], b_ref[...])


def _ffn_ln(x, wg, wu, wd, g, b, *, alpha, tm, tf):
    rows, d = x.shape
    ffp = wg.shape[1]
    return pl.pallas_call(
        functools.partial(_ffn_kernel, alpha=alpha),
        grid=(rows // tm, ffp // tf),
        in_specs=[
            pl.BlockSpec((tm, d), lambda i, j: (i, 0)),
            pl.BlockSpec((d, tf), lambda i, j: (0, j)),
            pl.BlockSpec((d, tf), lambda i, j: (0, j)),
            pl.BlockSpec((tf, d), lambda i, j: (j, 0)),
            pl.BlockSpec((1, d), lambda i, j: (0, 0)),
            pl.BlockSpec((1, d), lambda i, j: (0, 0)),
        ],
        out_specs=pl.BlockSpec((tm, d), lambda i, j: (i, 0)),
        out_shape=jax.ShapeDtypeStruct((rows, d), F32),
        scratch_shapes=[pltpu.VMEM((tm, d), BF16), pltpu.VMEM((tm, d), F32)],
        compiler_params=_cparams(("parallel", "arbitrary")),
        name="ffn_ln",
    )(x, wg, wu, wd, g, b)


def _proj_in_kernel(x_ref, w_ref, xr_ref, gr_ref, q_ref, k_ref, v_ref, kb_ref, vb_ref, qi_ref, kw_ref,
                    *, d_rnn, d_q, d_kv, d_qi):
    xb = x_ref[...].astype(BF16)
    off = [0]

    def seg(width):
        lo = off[0]
        off[0] = lo + width
        return _dot(xb, w_ref[:, lo:lo + width])

    xr_ref[...] = seg(d_rnn)
    gr_ref[...] = seg(d_rnn)
    q_ref[...] = seg(d_q).astype(BF16)
    k = seg(d_kv)
    k_ref[...] = k
    kb_ref[...] = k.astype(BF16)
    v = seg(d_kv)
    v_ref[...] = v
    vb_ref[...] = v.astype(BF16)
    qi_ref[...] = seg(d_qi).astype(BF16)
    kw_ref[...] = seg(LANES)


def _proj_in(x, w, *, tm, d_rnn, d_q, d_kv, d_qi):
    rows, d = x.shape
    widths = (d_rnn, d_rnn, d_q, d_kv, d_kv, d_kv, d_kv, d_qi, LANES)
    dtypes = (F32, F32, BF16, F32, F32, BF16, BF16, BF16, F32)
    return pl.pallas_call(
        functools.partial(_proj_in_kernel, d_rnn=d_rnn, d_q=d_q, d_kv=d_kv, d_qi=d_qi),
        grid=(rows // tm,),
        in_specs=[
            pl.BlockSpec((tm, d), lambda i: (i, 0)),
            pl.BlockSpec(w.shape, lambda i: (0, 0), pipeline_mode=pl.Buffered(1)),
        ],
        out_specs=[pl.BlockSpec((tm, wd), lambda i: (i, 0)) for wd in widths],
        out_shape=[jax.ShapeDtypeStruct((rows, wd), dt) for wd, dt in zip(widths, dtypes)],
        compiler_params=_cparams(("parallel",)),
        name="proj_in",
    )(x, w)


def _softplus(z):
    return jnp.maximum(z, 0.0) + jnp.log1p(jnp.exp(-jnp.abs(z)))


def _lru_gates(xc, wa_ref, ba, wi_ref, bi, lam):
    bw = xc.shape[1] // LRU_BLOCKS
    sp = _softplus(-lam)
    a_parts, u_parts = [], []
    for n in range(LRU_BLOCKS):
        sl = slice(n * bw, (n + 1) * bw)
        xn = xc[:, sl]
        xb = xn.astype(BF16)
        r = jax.nn.sigmoid(_dot(xb, wa_ref[n]) + ba[:, sl])
        i = jax.nn.sigmoid(_dot(xb, wi_ref[n]) + bi[:, sl])
        log_a = -LRU_C * r * sp[:, sl]
        a = jnp.exp(log_a)
        a_parts.append(a)
        u_parts.append(jnp.sqrt(-jnp.tanh(log_a) * (a * a + 1.0)) * i * xn)
    return jnp.concatenate(a_parts, axis=1), jnp.concatenate(u_parts, axis=1)


def _lru_prompt_kernel(xr_ref, gr_ref, cw_ref, cb_ref, wa_ref, ba_ref, wi_ref, bi_ref, lam_ref,
                       o_ref, hl_ref, xp_ref, a_ref, u_ref, hs_ref, hc_ref, *, tt):
    j = pl.program_id(1)
    d = xr_ref.shape[1]

    @pl.when(j == 0)
    def _():
        xp_ref[0:SUBLANES, :] = jnp.zeros((SUBLANES, d), F32)
        hc_ref[...] = jnp.zeros_like(hc_ref)

    @pl.when(j > 0)
    def _():
        xp_ref[0:SUBLANES, :] = xp_ref[tt:tt + SUBLANES, :]

    xp_ref[SUBLANES:SUBLANES + tt, :] = xr_ref[...]
    cw = cw_ref[...]
    xc = cb_ref[...]
    for jj in range(CONV_W):
        lo = SUBLANES - (CONV_W - 1) + jj
        xc = xc + cw[jj:jj + 1, :] * xp_ref[lo:lo + tt, :]

    a, u = _lru_gates(xc, wa_ref, ba_ref[...], wi_ref, bi_ref[...], lam_ref[...])
    a_ref[...] = a
    u_ref[...] = u

    row = lax.broadcasted_iota(I32, (SUBLANES, d), 0)

    def group(g, h):
        r0 = pl.multiple_of(g * SUBLANES, SUBLANES)
        a8 = a_ref[pl.ds(r0, SUBLANES), :]
        u8 = u_ref[pl.ds(r0, SUBLANES), :]
        out = jnp.zeros((SUBLANES, d), F32)
        for jj in range(SUBLANES):
            aj = jnp.broadcast_to(a8[jj:jj + 1, :], (SUBLANES, d))
            uj = jnp.broadcast_to(u8[jj:jj + 1, :], (SUBLANES, d))
            h = aj * h + uj
            out = jnp.where(row == jj, h, out)
        hs_ref[pl.ds(r0, SUBLANES), :] = out
        return h

    h = lax.fori_loop(0, tt // SUBLANES, group, hc_ref[...])
    hc_ref[...] = h
    hl_ref[...] = h[0:1, :]
    o_ref[...] = (hs_ref[...] * jax.nn.gelu(gr_ref[...])).astype(BF16)


def _lru_prompt(xr, gr, cw, cb, wa, ba, wi, bi, lam, *, batch, tt):
    rows, d = xr.shape
    seq = rows // batch
    nt = seq // tt
    row_spec = pl.BlockSpec((tt, d), lambda b, j: (b * nt + j, 0))
    vec_spec = pl.BlockSpec((1, d), lambda b, j: (0, 0))
    w_spec = pl.BlockSpec(wa.shape, lambda b, j: (0, 0, 0))
    return pl.pallas_call(
        functools.partial(_lru_prompt_kernel, tt=tt),
        grid=(batch, nt),
        in_specs=[row_spec, row_spec, pl.BlockSpec((CONV_W, d), lambda b, j: (0, 0)), vec_spec,
                  w_spec, vec_spec, w_spec, vec_spec, vec_spec],
        out_specs=[row_spec, pl.BlockSpec((None, 1, d), lambda b, j: (b, 0, 0))],
        out_shape=[jax.ShapeDtypeStruct((rows, d), BF16), jax.ShapeDtypeStruct((batch, 1, d), F32)],
        scratch_shapes=[pltpu.VMEM((tt + SUBLANES, d), F32), pltpu.VMEM((tt, d), F32),
                        pltpu.VMEM((tt, d), F32), pltpu.VMEM((tt, d), F32), pltpu.VMEM((SUBLANES, d), F32)],
        compiler_params=_cparams(("parallel", "arbitrary")),
        name="lru_prompt",
    )(xr, gr, cw, cb, wa, ba, wi, bi, lam)


def _lru_sample_kernel(xp_ref, gr_ref, h0_ref, cw_ref, cb_ref, wa_ref, ba_ref, wi_ref, bi_ref, lam_ref,
                       o_ref, hl_ref, *, nb, nt):
    cw = cw_ref[...]
    xc = cb_ref[...] + cw[0:1, :] * xp_ref[0:nt * nb, :]
    for jj in range(1, CONV_W):
        xc = xc + cw[jj:jj + 1, :] * xp_ref[jj * nb:(jj + nt) * nb, :]
    a, u = _lru_gates(xc, wa_ref, ba_ref[...], wi_ref, bi_ref[...], lam_ref[...])
    gate = jax.nn.gelu(gr_ref[...])
    h = h0_ref[...]
    for t in range(nt):
        sl = slice(t * nb, (t + 1) * nb)
        h = a[sl, :] * h + u[sl, :]
        o_ref[sl, :] = (h * gate[sl, :]).astype(BF16)
    hl_ref[...] = h


def _lru_sample(xp, gr, h0, cw, cb, wa, ba, wi, bi, lam, *, nb, nt):
    d = gr.shape[1]
    return pl.pallas_call(
        functools.partial(_lru_sample_kernel, nb=nb, nt=nt),
        out_shape=[jax.ShapeDtypeStruct((nt * nb, d), BF16), jax.ShapeDtypeStruct((nb, d), F32)],
        compiler_params=pltpu.CompilerParams(vmem_limit_bytes=VMEM_LIMIT),
        name="lru_sample",
    )(xp, gr, h0, cw, cb, wa, ba, wi, bi, lam)


def _rank_to_float(u):
    key = u ^ jnp.int32(INT_MIN)
    return lax.bitcast_convert_type(key ^ ((key >> 31) & jnp.int32(0x7FFFFFFF)), F32)


def _count(sc_ref, nc, rows, ck, thr, strict):
    thr_b = jnp.broadcast_to(thr, (rows, LANES))

    def body(c, acc):
        sc = sc_ref[c]
        for t in range(ck // LANES):
            tile = sc[:, t * LANES:(t + 1) * LANES]
            hit = (tile > thr_b) if strict else (tile >= thr_b)
            acc = acc + jnp.where(hit, 1.0, 0.0)
        return acc

    acc = lax.fori_loop(0, nc, body, jnp.zeros((rows, LANES), F32))
    return jnp.sum(acc, axis=1, keepdims=True)


def _select_topk(sc_ref, tau_ref, nc, rows, ck, k_sel):
    kf = float(k_sel)

    def bit_body(it, u):
        cand = u | jnp.left_shift(jnp.int32(1), 31 - it)
        cnt = _count(sc_ref, nc, rows, ck, _rank_to_float(cand), strict=False)
        return jnp.where(cnt >= kf, cand, u)

    tau = _rank_to_float(lax.fori_loop(0, 32, bit_body, jnp.zeros((rows, 1), I32)))
    tau_ref[...] = jnp.broadcast_to(tau, (rows, LANES))
    n_gt = _count(sc_ref, nc, rows, ck, tau, strict=True)
    n_ge = _count(sc_ref, nc, rows, ck, tau, strict=False)
    need = kf - n_gt

    @pl.when(jnp.max(n_ge) > kf)
    def _():
        before = (lax.broadcasted_iota(I32, (ck, ck), 0) < lax.broadcasted_iota(I32, (ck, ck), 1))
        before = jnp.where(before, 1.0, 0.0).astype(BF16)
        tau_w = jnp.broadcast_to(tau, (rows, ck))
        need_w = jnp.broadcast_to(need, (rows, ck))

        def body(c, seen):
            sc = sc_ref[c]
            eq = sc == tau_w
            eqf = jnp.where(eq, 1.0, 0.0)
            rank = _dot(eqf.astype(BF16), before) + seen
            sc_ref[c] = jnp.where(eq, jnp.where(rank >= need_w, -jnp.inf, sc), sc)
            return seen + jnp.sum(eqf, axis=1, keepdims=True)

        lax.fori_loop(0, nc, body, jnp.zeros((rows, 1), F32))


def _prompt_attn_kernel(q_ref, qi_ref, kwq_ref, kwk_ref, kb_ref, vb_ref, o_ref,
                        ki_ref, sc_ref, tau_ref, qs_ref, m_ref, l_ref, acc_ref, *, k_sel):
    i = pl.program_id(1)
    ck = sc_ref.shape[2]
    nc = (i * Q_BLOCK + Q_BLOCK + ck - 1) // ck

    @pl.when(i == 0)
    def _():
        ki_ref[...] = kwk_ref[:, 0:IDX_DIM].astype(BF16)

    w_idx = kwq_ref[:, IDX_DIM:IDX_DIM + IDX_HEADS] * IDX_W_SCALE * IDX_SCALE
    t_pos = i * Q_BLOCK + lax.broadcasted_iota(I32, (Q_BLOCK, ck), 0)
    col = lax.broadcasted_iota(I32, (Q_BLOCK, ck), 1)

    def score_body(c, carry):
        c0 = pl.multiple_of(c * ck, ck)
        kc = ki_ref[pl.ds(c0, ck), :]
        score = jnp.zeros((Q_BLOCK, ck), F32)
        for h in range(IDX_HEADS):
            s = _dot_nt(qi_ref[:, h * IDX_DIM:(h + 1) * IDX_DIM], kc)
            score = score + jnp.maximum(s, 0.0) * w_idx[:, h:h + 1]
        sc_ref[c] = jnp.where(col + c0 <= t_pos, score, -jnp.inf)
        return carry

    lax.fori_loop(0, nc, score_body, 0)

    need_select = (i + 1) * Q_BLOCK > k_sel

    @pl.when(jnp.logical_not(need_select))
    def _():
        tau_ref[...] = jnp.full(tau_ref.shape, FLT_LOWEST, F32)

    @pl.when(need_select)
    def _():
        _select_topk(sc_ref, tau_ref, nc, Q_BLOCK, ck, k_sel)

    for n in range(N_KV_HEADS):
        for g in range(GQA_GROUP):
            hq = n * GQA_GROUP + g
            qs_ref[n, g * Q_BLOCK:(g + 1) * Q_BLOCK, :] = q_ref[:, hq * HEAD_DIM:(hq + 1) * HEAD_DIM]
    m_ref[...] = jnp.full(m_ref.shape, NEG_BIG, F32)
    l_ref[...] = jnp.zeros_like(l_ref)
    acc_ref[...] = jnp.zeros_like(acc_ref)
    tau_w = jnp.broadcast_to(tau_ref[:, 0:1], (Q_BLOCK, ck))

    def attn_body(c, carry):
        c0 = pl.multiple_of(c * ck, ck)
        bias = jnp.where(sc_ref[c] >= tau_w, 0.0, NEG_BIG)
        bias = jnp.concatenate([bias] * GQA_GROUP, axis=0)
        for n in range(N_KV_HEADS):
            kn = kb_ref[pl.ds(c0, ck), n * HEAD_DIM:(n + 1) * HEAD_DIM]
            vn = vb_ref[pl.ds(c0, ck), n * HEAD_DIM:(n + 1) * HEAD_DIM]
            s = _dot_nt(qs_ref[n], kn) + bias
            tiles = [s[:, t * LANES:(t + 1) * LANES] for t in range(ck // LANES)]
            m_prev = m_ref[n]
            m_new = jnp.maximum(m_prev, jnp.max(functools.reduce(jnp.maximum, tiles), axis=1, keepdims=True))
            alpha = jnp.exp2((m_prev - m_new) * SOFTMAX_LOG2_SCALE)
            p_tiles = [jnp.exp2((t - m_new) * SOFTMAX_LOG2_SCALE) for t in tiles]
            l_ref[n] = alpha * l_ref[n] + functools.reduce(jnp.add, p_tiles)
            p = jnp.concatenate(p_tiles, axis=1).astype(BF16)
            acc_ref[n] = alpha * acc_ref[n] + _dot(p, vn)
            m_ref[n] = m_new
        return carry

    lax.fori_loop(0, nc, attn_body, 0)

    for n in range(N_KV_HEADS):
        o = acc_ref[n] / jnp.sum(l_ref[n], axis=1, keepdims=True)
        for g in range(GQA_GROUP):
            hq = n * GQA_GROUP + g
            o_ref[:, hq * HEAD_DIM:(hq + 1) * HEAD_DIM] = o[g * Q_BLOCK:(g + 1) * Q_BLOCK, :].astype(BF16)


def _prompt_attn(q, qi, kw, kb, vb, *, batch):
    rows = q.shape[0]
    seq = rows // batch
    nblk = seq // Q_BLOCK
    ck = min(KEY_CHUNK, seq)
    k_sel = min(TOPK_MAX, seq // 4)
    gq = GQA_GROUP * Q_BLOCK
    blk = lambda b, i: (b * nblk + i, 0)
    per_b = lambda b, i: (b, 0)
    return pl.pallas_call(
        functools.partial(_prompt_attn_kernel, k_sel=k_sel),
        grid=(batch, nblk),
        in_specs=[
            pl.BlockSpec((Q_BLOCK, q.shape[1]), blk),
            pl.BlockSpec((Q_BLOCK, qi.shape[1]), blk),
            pl.BlockSpec((Q_BLOCK, LANES), blk),
            pl.BlockSpec((seq, LANES), per_b),
            pl.BlockSpec((seq, kb.shape[1]), per_b),
            pl.BlockSpec((seq, vb.shape[1]), per_b),
        ],
        out_specs=pl.BlockSpec((Q_BLOCK, q.shape[1]), blk),
        out_shape=jax.ShapeDtypeStruct(q.shape, BF16),
        scratch_shapes=[
            pltpu.VMEM((seq, IDX_DIM), BF16),
            pltpu.VMEM((seq // ck, Q_BLOCK, ck), F32),
            pltpu.VMEM((Q_BLOCK, LANES), F32),
            pltpu.VMEM((N_KV_HEADS, gq, HEAD_DIM), BF16),
            pltpu.VMEM((N_KV_HEADS, gq, LANES), F32),
            pltpu.VMEM((N_KV_HEADS, gq, LANES), F32),
            pltpu.VMEM((N_KV_HEADS, gq, HEAD_DIM), F32),
        ],
        compiler_params=_cparams(("parallel", "arbitrary")),
        name="prompt_attn",
    )(q, qi, kw, kw, kb, vb)


def _sample_scores_kernel(pt_ref, qi_ref, w_ref, kin_ref, *rest, nt, npg):
    pages = rest[:npg]
    past_ref, new_ref, kc_ref = rest[npg:]
    c = pl.program_id(1)
    w = w_ref[...] * IDX_W_SCALE * IDX_SCALE
    qi = qi_ref[...]

    def scores(s):
        width = s.shape[1]
        s = jnp.maximum(s, 0.0) * jnp.concatenate([w] * (width // LANES), axis=1)
        return jnp.sum(s.reshape(nt, IDX_HEADS, width), axis=1)

    for p in range(npg):
        kc_ref[:, p * PAGE_SIZE:(p + 1) * PAGE_SIZE] = pages[p][...].astype(BF16)
    past_ref[...] = scores(_dot(qi, kc_ref[...]))

    @pl.when(c == 0)
    def _():
        new_ref[...] = scores(_dot_nt(qi, kin_ref[...].astype(BF16)))


def _sample_scores(page_table, qi_b, w_b, ki_new, cache_ki, *, layer, nt):
    nb, n_pages = page_table.shape
    npg = min(IDX_PAGES_PER_STEP, n_pages)
    rows = nt * IDX_HEADS
    page_specs = [
        pl.BlockSpec((None, None, IDX_DIM, PAGE_SIZE),
                     functools.partial(lambda b, c, pt, p: (layer, pt[b, c * npg + p], 0, 0), p=p))
        for p in range(npg)
    ]
    grid_spec = pltpu.PrefetchScalarGridSpec(
        num_scalar_prefetch=1,
        grid=(nb, n_pages // npg),
        in_specs=[
            pl.BlockSpec((None, rows, IDX_DIM), lambda b, c, pt: (b, 0, 0)),
            pl.BlockSpec((None, rows, LANES), lambda b, c, pt: (b, 0, 0)),
            pl.BlockSpec((None, PAGE_SIZE, IDX_DIM), lambda b, c, pt: (b, 0, 0)),
        ] + page_specs,
        out_specs=[
            pl.BlockSpec((None, nt, npg * PAGE_SIZE), lambda b, c, pt: (b, 0, c)),
            pl.BlockSpec((None, nt, PAGE_SIZE), lambda b, c, pt: (b, 0, 0)),
        ],
        scratch_shapes=[pltpu.VMEM((IDX_DIM, npg * PAGE_SIZE), BF16)],
    )
    return pl.pallas_call(
        functools.partial(_sample_scores_kernel, nt=nt, npg=npg),
        grid_spec=grid_spec,
        out_shape=[jax.ShapeDtypeStruct((nb, nt, n_pages * PAGE_SIZE), F32),
                   jax.ShapeDtypeStruct((nb, nt, PAGE_SIZE), F32)],
        compiler_params=_cparams(("parallel", "arbitrary")),
        name="sample_scores",
    )(page_table, qi_b, w_b, ki_new, *([cache_ki] * npg))


def _sample_select_kernel(past_ref, new_ref, bias_ref, sc_ref, tau_ref, *, nt, k_sel, ck):
    rows, n_past = past_ref.shape
    npc = n_past // ck
    for c in range(npc):
        sc_ref[c] = past_ref[:, c * ck:(c + 1) * ck]
    t_row = lax.broadcasted_iota(I32, (rows, ck), 0) & (nt - 1)
    col = lax.broadcasted_iota(I32, (rows, ck), 1)
    new_sc = jnp.concatenate([new_ref[...]] * (ck // PAGE_SIZE), axis=1)
    sc_ref[npc] = jnp.where(col <= t_row, new_sc, -jnp.inf)
    _select_topk(sc_ref, tau_ref, npc + 1, rows, ck, k_sel)
    tau_b = tau_ref[...]
    d = (lax.broadcasted_iota(I32, (LANES, LANES * N_KV_HEADS), 1)
         - N_KV_HEADS * lax.broadcasted_iota(I32, (LANES, LANES * N_KV_HEADS), 0))
    expand = jnp.where(d >= 0, jnp.where(d < N_KV_HEADS, 1.0, 0.0), 0.0).astype(BF16)
    for c in range(npc + 1):
        for t in range(ck // LANES if c < npc else PAGE_SIZE // LANES):
            sel = jnp.where(sc_ref[c][:, t * LANES:(t + 1) * LANES] >= tau_b, 1.0, 0.0).astype(BF16)
            lo = (c * ck + t * LANES) * N_KV_HEADS
            bias_ref[:, lo:lo + LANES * N_KV_HEADS] = (_dot(sel, expand) - 1.0) * (-NEG_BIG)


def _sample_select(past, new, *, nt):
    rows, n_past = past.shape
    assert nt & (nt - 1) == 0
    ck = min(KEY_CHUNK, n_past)
    k_sel = min(TOPK_MAX, (n_past + nt) // 4)
    return pl.pallas_call(
        functools.partial(_sample_select_kernel, nt=nt, k_sel=k_sel, ck=ck),
        out_shape=jax.ShapeDtypeStruct((rows, (n_past + PAGE_SIZE) * N_KV_HEADS), F32),
        scratch_shapes=[pltpu.VMEM((n_past // ck + 1, rows, ck), F32), pltpu.VMEM((rows, LANES), F32)],
        compiler_params=pltpu.CompilerParams(vmem_limit_bytes=VMEM_LIMIT),
        name="sample_select",
    )(past, new)


def _sample_attn_kernel(pt_ref, q_ref, bias_ref, biasn_ref, kn_ref, vn_ref, *rest, nt, npg):
    k_pages = rest[:npg]
    v_pages = rest[npg:2 * npg]
    o_ref, kc_ref, vc_ref, m_ref, l_ref, acc_ref = rest[2 * npg:]
    c = pl.program_id(1)
    rows = q_ref.shape[0]
    page_rows = PAGE_SIZE * N_KV_HEADS

    @pl.when(c == 0)
    def _():
        m_ref[...] = jnp.full(m_ref.shape, NEG_BIG, F32)
        l_ref[...] = jnp.zeros_like(l_ref)
        acc_ref[...] = jnp.zeros_like(acc_ref)

    row_head = lax.shift_right_logical(lax.broadcasted_iota(I32, (rows, page_rows), 0), _log2(rows // N_KV_HEADS))
    col_head = lax.broadcasted_iota(I32, (rows, page_rows), 1) & (N_KV_HEADS - 1)
    head_bias = jnp.where(row_head == col_head, 0.0, NEG_BIG)

    def update(k2, v2, bias):
        width = k2.shape[0]
        b = jnp.concatenate([bias] * (rows // nt), axis=0) + jnp.concatenate([head_bias] * (width // page_rows), axis=1)
        s = _dot_nt(q_ref[...], k2) * ATTN_SCALE + b
        m_prev = m_ref[:, 0:1]
        m_new = jnp.maximum(m_prev, jnp.max(s, axis=1, keepdims=True))
        alpha = jnp.exp(m_prev - m_new)
        p = jnp.exp(s - m_new)
        l_ref[...] = alpha * l_ref[...] + jnp.sum(p, axis=1, keepdims=True)
        acc_ref[...] = alpha * acc_ref[...] + _dot(p.astype(BF16), v2)
        m_ref[...] = jnp.broadcast_to(m_new, m_ref.shape)

    for p in range(npg):
        kc_ref[p * page_rows:(p + 1) * page_rows, :] = k_pages[p][...].reshape(page_rows, HEAD_DIM).astype(BF16)
        vc_ref[p * page_rows:(p + 1) * page_rows, :] = v_pages[p][...].reshape(page_rows, HEAD_DIM).astype(BF16)
    update(kc_ref[...], vc_ref[...], bias_ref[...])

    @pl.when(c == pl.num_programs(1) - 1)
    def _():
        update(kn_ref[...].astype(BF16), vn_ref[...].astype(BF16), biasn_ref[...])
        o_ref[...] = acc_ref[...] / l_ref[...]


def _sample_attn(page_table, q_b, bias, k_new, v_new, cache_k, cache_v, *, layer, nt):
    nb, n_pages = page_table.shape
    npg = min(PAGES_PER_STEP, n_pages)
    rows = q_b.shape[1]
    page_rows = PAGE_SIZE * N_KV_HEADS
    assert N_KV_HEADS & (N_KV_HEADS - 1) == 0

    def page_spec(p):
        return pl.BlockSpec((None, None, PAGE_SIZE, N_KV_HEADS, HEAD_DIM),
                            functools.partial(lambda b, c, pt, p: (layer, pt[b, c * npg + p], 0, 0, 0), p=p))

    per_b3 = lambda b, c, pt: (b, 0, 0)
    grid_spec = pltpu.PrefetchScalarGridSpec(
        num_scalar_prefetch=1,
        grid=(nb, n_pages // npg),
        in_specs=[
            pl.BlockSpec((None, rows, HEAD_DIM), per_b3),
            pl.BlockSpec((None, nt, npg * page_rows), lambda b, c, pt: (b, 0, c)),
            pl.BlockSpec((None, nt, page_rows), lambda b, c, pt: (b, 0, n_pages)),
            pl.BlockSpec((None, page_rows, HEAD_DIM), per_b3),
            pl.BlockSpec((None, page_rows, HEAD_DIM), per_b3),
        ] + [page_spec(p) for p in range(npg)] + [page_spec(p) for p in range(npg)],
        out_specs=pl.BlockSpec((None, rows, HEAD_DIM), per_b3),
        scratch_shapes=[
            pltpu.VMEM((npg * page_rows, HEAD_DIM), BF16),
            pltpu.VMEM((npg * page_rows, HEAD_DIM), BF16),
            pltpu.VMEM((rows, LANES), F32),
            pltpu.VMEM((rows, LANES), F32),
            pltpu.VMEM((rows, HEAD_DIM), F32),
        ],
    )
    return pl.pallas_call(
        functools.partial(_sample_attn_kernel, nt=nt, npg=npg),
        grid_spec=grid_spec,
        out_shape=jax.ShapeDtypeStruct((nb, rows, HEAD_DIM), F32),
        compiler_params=_cparams(("parallel", "arbitrary")),
        name="sample_attn",
    )(page_table, q_b, bias, bias, k_new, v_new, *([cache_k] * npg), *([cache_v] * npg))


def _proj_out_kernel(x_ref, r_ref, a_ref, w_ref, g_ref, b_ref, o_ref, *, alpha):
    d_rnn = r_ref.shape[1]
    mix = _dot(r_ref[...], w_ref[0:d_rnn, :]) + _dot(a_ref[...], w_ref[d_rnn:, :])
    o_ref[...] = _layer_norm(alpha * x_ref[...] + mix, g_ref[...], b_ref[...])


def _proj_out_ln(x, rnn, attn, w, g, b, *, alpha, tm):
    rows, d = x.shape
    return pl.pallas_call(
        functools.partial(_proj_out_kernel, alpha=alpha),
        grid=(rows // tm,),
        in_specs=[
            pl.BlockSpec((tm, d), lambda i: (i, 0)),
            pl.BlockSpec((tm, rnn.shape[1]), lambda i: (i, 0)),
            pl.BlockSpec((tm, attn.shape[1]), lambda i: (i, 0)),
            pl.BlockSpec(w.shape, lambda i: (0, 0)),
            pl.BlockSpec((1, d), lambda i: (0, 0)),
            pl.BlockSpec((1, d), lambda i: (0, 0)),
        ],
        out_specs=pl.BlockSpec((tm, d), lambda i: (i, 0)),
        out_shape=jax.ShapeDtypeStruct((rows, d), F32),
        compiler_params=_cparams(("parallel",)),
        name="proj_out_ln",
    )(x, rnn, attn, w, g, b)


def _row_tile(rows, target):
    tm = min(rows, target)
    while rows % tm:
        tm //= 2
    return tm


def _pad_to(a, axis, size):
    pad = [(0, 0)] * a.ndim
    pad[axis] = (0, size - a.shape[axis])
    return jnp.pad(a, pad)


def _ffn_weights(w_gu, w_down):
    d_ff = w_down.shape[0]
    tf = 2 * MXU_DIM
    ffp = -(-d_ff // tf) * tf
    wg = _pad_to(w_gu[:, :d_ff].astype(BF16), 1, ffp)
    wu = _pad_to(w_gu[:, d_ff:].astype(BF16), 1, ffp)
    wd = _pad_to(w_down.astype(BF16), 0, ffp)
    return wg, wu, wd, tf


def kernel(x_prompt, x_sample, cache_k, cache_v, cache_k_idx, state_conv, state_rnn, page_table, ln1_g, ln1_b, ffn1_w_gu, ffn1_w_down, w_in, conv_w, conv_b, lru_w_a, lru_b_a, lru_w_i, lru_b_i, lru_lambda, w_out, ln2_g, ln2_b, ffn2_w_gu, ffn2_w_down, ln3_g, ln3_b):
    depth = w_in.shape[0]
    bp, seq, d_model = x_prompt.shape
    nb, nt, _ = x_sample.shape
    d_rnn = conv_w.shape[2]
    d_q = N_Q_HEADS * HEAD_DIM
    d_kv = N_KV_HEADS * HEAD_DIM
    d_qi = IDX_HEADS * IDX_DIM
    alpha = (2.0 * depth) ** 0.25

    xp = x_prompt.reshape(bp * seq, d_model)
    xs = x_sample.transpose(1, 0, 2).reshape(nt * nb, d_model)
    tm_p = _row_tile(bp * seq, 512)
    tm_s = _row_tile(nt * nb, 512)
    vec = lambda a: a.reshape(1, -1)

    outs = [[] for _ in range(10)]
    for l in range(depth):
        wg1, wu1, wd1, tf = _ffn_weights(ffn1_w_gu[l], ffn1_w_down[l])
        wg2, wu2, wd2, _ = _ffn_weights(ffn2_w_gu[l], ffn2_w_down[l])
        d_in = w_in.shape[2]
        w_in_b = _pad_to(w_in[l].astype(BF16), 1, d_in - IDX_DIM - IDX_HEADS + LANES)
        w_out_b = w_out[l].astype(BF16)
        wa_b = lru_w_a[l].astype(BF16)
        wi_b = lru_w_i[l].astype(BF16)
        lru_args = (conv_w[l], vec(conv_b[l]), wa_b, vec(lru_b_a[l]), wi_b, vec(lru_b_i[l]), vec(lru_lambda[l]))
        proj = functools.partial(_proj_in, d_rnn=d_rnn, d_q=d_q, d_kv=d_kv, d_qi=d_qi)

        x1p = _ffn_ln(xp, wg1, wu1, wd1, vec(ln1_g[l]), vec(ln1_b[l]), alpha=alpha, tm=tm_p, tf=tf)
        x1s = _ffn_ln(xs, wg1, wu1, wd1, vec(ln1_g[l]), vec(ln1_b[l]), alpha=alpha, tm=tm_s, tf=tf)
        xr_p, gr_p, q_p, k_p, v_p, kb_p, vb_p, qi_p, kw_p = proj(x1p, w_in_b, tm=_row_tile(bp * seq, 256))
        xr_s, gr_s, q_s, k_s, v_s, _, _, qi_s, kw_s = proj(x1s, w_in_b, tm=_row_tile(nt * nb, 256))

        rnn_p, hl_p = _lru_prompt(xr_p, gr_p, *lru_args, batch=bp, tt=_row_tile(seq, 512))
        conv_tm = state_conv[l].transpose(1, 0, 2).reshape((CONV_W - 1) * nb, d_rnn)
        xpad_s = jnp.concatenate([conv_tm, xr_s], axis=0)
        rnn_s, hl_s = _lru_sample(xpad_s, gr_s, state_rnn[l], *lru_args, nb=nb, nt=nt)

        attn_p = _prompt_attn(q_p, qi_p, kw_p, kb_p, vb_p, batch=bp)

        to_b = lambda a: a.reshape(nt, nb, -1).transpose(1, 0, 2)
        qi_b = to_b(qi_s).reshape(nb, nt * IDX_HEADS, IDX_DIM)
        kw_b = to_b(kw_s)
        w_b = jnp.broadcast_to(kw_b[:, :, IDX_DIM:IDX_DIM + IDX_HEADS].reshape(nb, nt * IDX_HEADS, 1),
                               (nb, nt * IDX_HEADS, LANES))
        ki_new = _pad_to(kw_b[:, :, :IDX_DIM], 1, PAGE_SIZE)
        cache_ki_t = jnp.swapaxes(cache_k_idx, 2, 3)
        past_sc, new_sc = _sample_scores(page_table, qi_b, w_b, ki_new, cache_ki_t, layer=l, nt=nt)
        n_past = past_sc.shape[2]
        bias = _sample_select(past_sc.reshape(nb * nt, n_past), new_sc.reshape(nb * nt, PAGE_SIZE), nt=nt)
        bias = bias.reshape(nb, nt, (n_past + PAGE_SIZE) * N_KV_HEADS)
        q_b = to_b(q_s).reshape(nb, nt, N_KV_HEADS, GQA_GROUP, HEAD_DIM)
        q_b = q_b.transpose(0, 2, 3, 1, 4).reshape(nb, N_Q_HEADS * nt, HEAD_DIM)
        new_page = lambda a: _pad_to(to_b(a).reshape(nb, nt * N_KV_HEADS, HEAD_DIM), 1, PAGE_SIZE * N_KV_HEADS)
        o_b = _sample_attn(page_table, q_b, bias, new_page(k_s), new_page(v_s), cache_k, cache_v, layer=l, nt=nt)
        attn_s = o_b.reshape(nb, N_KV_HEADS, GQA_GROUP, nt, HEAD_DIM).transpose(3, 0, 1, 2, 4)
        attn_s = attn_s.reshape(nt * nb, d_q).astype(BF16)

        x2p = _proj_out_ln(x1p, rnn_p, attn_p, w_out_b, vec(ln2_g[l]), vec(ln2_b[l]), alpha=alpha, tm=tm_p)
        x2s = _proj_out_ln(x1s, rnn_s, attn_s, w_out_b, vec(ln2_g[l]), vec(ln2_b[l]), alpha=alpha, tm=tm_s)
        xp = _ffn_ln(x2p, wg2, wu2, wd2, vec(ln3_g[l]), vec(ln3_b[l]), alpha=alpha, tm=tm_p, tf=tf)
        xs = _ffn_ln(x2s, wg2, wu2, wd2, vec(ln3_g[l]), vec(ln3_b[l]), alpha=alpha, tm=tm_s, tf=tf)

        xr_p3 = xr_p.reshape(bp, seq, d_rnn)
        conv_p = xr_p3[:, -(CONV_W - 1):]
        conv_s = xpad_s.reshape(CONV_W - 1 + nt, nb, d_rnn)[-(CONV_W - 1):].transpose(1, 0, 2)
        layer_out = (
            k_p.reshape(bp, seq, N_KV_HEADS, HEAD_DIM), v_p.reshape(bp, seq, N_KV_HEADS, HEAD_DIM),
            kw_p[:, :IDX_DIM].reshape(bp, seq, IDX_DIM), conv_p, hl_p.reshape(bp, d_rnn),
            to_b(k_s).reshape(nb, nt, N_KV_HEADS, HEAD_DIM), to_b(v_s).reshape(nb, nt, N_KV_HEADS, HEAD_DIM),
            kw_b[:, :, :IDX_DIM], conv_s, hl_s,
        )
        for acc, val in zip(outs, layer_out):
            acc.append(val)

    y_p = xp.reshape(bp, seq, d_model)
    y_s = xs.reshape(nt, nb, d_model).transpose(1, 0, 2)
    return (y_p, y_s) + tuple(jnp.stack(o) for o in outs)
```

```python
import functools

import jax
import jax.numpy as jnp
from jax import lax
from jax.experimental import pallas as pl
from jax.experimental.pallas import tpu as pltpu

F32 = jnp.float32
BF16 = jnp.bfloat16
I32 = jnp.int32

LRU_BLOCKS = 8
CONV_W = 4
LRU_C = 8.0
HEAD_DIM = 128
N_KV_HEADS = 4
GQA_GROUP = 2
N_Q_HEADS = N_KV_HEADS * GQA_GROUP
IDX_HEADS = 8
IDX_DIM = 64
TOPK_MAX = 256
QUERY_ROWS = (256, 128)
COUNT_ROWS = 128
PAGE_SIZE = 128
LN_EPS = 1e-5
ATTN_SCALE = HEAD_DIM ** -0.5
SOFTMAX_LOG2_SCALE = ATTN_SCALE * 1.4426950408889634
IDX_SCALE = IDX_DIM ** -0.5
IDX_W_SCALE = IDX_HEADS ** -0.5

LANES = 128
SUBLANES = 8
MXU_DIM = 256
VMEM_LIMIT = 56 * 1024 * 1024

INT_MIN = -2 ** 31
FLT_LOWEST = -3.4028234663852886e38
NEG_BIG = -1e30
KEY_CHUNK = 512
PAGES_PER_STEP = 16
IDX_PAGES_PER_STEP = 16


def _cparams(semantics):
    return pltpu.CompilerParams(dimension_semantics=semantics, vmem_limit_bytes=VMEM_LIMIT)


def _layer_norm(y, g, b):
    mu = jnp.mean(y, axis=-1, keepdims=True)
    d = y - mu
    var = jnp.mean(d * d, axis=-1, keepdims=True)
    return d * lax.rsqrt(var + LN_EPS) * g + b


def _dot(a, b):
    return jnp.dot(a, b, preferred_element_type=F32)


def _dot_nt(a, b):
    return lax.dot_general(a, b, (((1,), (1,)), ((), ())), preferred_element_type=F32)


def _log2(n):
    assert n > 0 and n & (n - 1) == 0, n
    return n.bit_length() - 1


def _ffn_kernel(x_ref, wg_ref, wu_ref, wd_ref, g_ref, b_ref, o_ref, xb_ref, acc_ref, *, alpha):
    j = pl.program_id(1)

    @pl.when(j == 0)
    def _():
        xb_ref[...] = x_ref[...].astype(BF16)
        acc_ref[...] = jnp.zeros_like(acc_ref)

    xb = xb_ref[...]
    gate = _dot(xb, wg_ref[...])
    up = _dot(xb, wu_ref[...])
    act = (gate * jax.nn.sigmoid(gate) * up).astype(BF16)
    acc_ref[...] += _dot(act, wd_ref[...])

    @pl.when(j == pl.num_programs(1) - 1)
    def _():
        y = alpha * x_ref[...] + 0.5 * acc_ref[...]
        o_ref[...] = _layer_norm(y, g_ref[...], b_ref[...])


def _ffn_ln(x, wg, wu, wd, g, b, *, alpha, tm, tf):
    rows, d = x.shape
    ffp = wg.shape[1]
    return pl.pallas_call(
        functools.partial(_ffn_kernel, alpha=alpha),
        grid=(rows // tm, ffp // tf),
        in_specs=[
            pl.BlockSpec((tm, d), lambda i, j: (i, 0)),
            pl.BlockSpec((d, tf), lambda i, j: (0, j)),
            pl.BlockSpec((d, tf), lambda i, j: (0, j)),
            pl.BlockSpec((tf, d), lambda i, j: (j, 0)),
            pl.BlockSpec((1, d), lambda i, j: (0, 0)),
            pl.BlockSpec((1, d), lambda i, j: (0, 0)),
        ],
        out_specs=pl.BlockSpec((tm, d), lambda i, j: (i, 0)),
        out_shape=jax.ShapeDtypeStruct((rows, d), F32),
        scratch_shapes=[pltpu.VMEM((tm, d), BF16), pltpu.VMEM((tm, d), F32)],
        compiler_params=_cparams(("parallel", "arbitrary")),
        name="ffn_ln",
    )(x, wg, wu, wd, g, b)


def _proj_in_kernel(x_ref, w_ref, xr_ref, gr_ref, q_ref, k_ref, v_ref, kb_ref, vb_ref, qi_ref, kw_ref,
                    *, d_rnn, d_q, d_kv, d_qi):
    xb = x_ref[...].astype(BF16)
    off = [0]

    def seg(width):
        lo = off[0]
        off[0] = lo + width
        return _dot(xb, w_ref[:, lo:lo + width])

    xr_ref[...] = seg(d_rnn)
    gr_ref[...] = seg(d_rnn)
    q_ref[...] = seg(d_q).astype(BF16)
    k = seg(d_kv)
    k_ref[...] = k
    kb_ref[...] = k.astype(BF16)
    v = seg(d_kv)
    v_ref[...] = v
    vb_ref[...] = v.astype(BF16)
    qi_ref[...] = seg(d_qi).astype(BF16)
    kw_ref[...] = seg(LANES)


def _proj_in(x, w, *, tm, d_rnn, d_q, d_kv, d_qi):
    rows, d = x.shape
    widths = (d_rnn, d_rnn, d_q, d_kv, d_kv, d_kv, d_kv, d_qi, LANES)
    dtypes = (F32, F32, BF16, F32, F32, BF16, BF16, BF16, F32)
    return pl.pallas_call(
        functools.partial(_proj_in_kernel, d_rnn=d_rnn, d_q=d_q, d_kv=d_kv, d_qi=d_qi),
        grid=(rows // tm,),
        in_specs=[
            pl.BlockSpec((tm, d), lambda i: (i, 0)),
            pl.BlockSpec(w.shape, lambda i: (0, 0), pipeline_mode=pl.Buffered(1)),
        ],
        out_specs=[pl.BlockSpec((tm, wd), lambda i: (i, 0)) for wd in widths],
        out_shape=[jax.ShapeDtypeStruct((rows, wd), dt) for wd, dt in zip(widths, dtypes)],
        compiler_params=_cparams(("parallel",)),
        name="proj_in",
    )(x, w)


def _softplus(z):
    return jnp.maximum(z, 0.0) + jnp.log1p(jnp.exp(-jnp.abs(z)))


def _lru_gates(xc, wa_ref, ba, wi_ref, bi, lam):
    bw = xc.shape[1] // LRU_BLOCKS
    sp = _softplus(-lam)
    a_parts, u_parts = [], []
    for n in range(LRU_BLOCKS):
        sl = slice(n * bw, (n + 1) * bw)
        xn = xc[:, sl]
        xb = xn.astype(BF16)
        r = jax.nn.sigmoid(_dot(xb, wa_ref[n]) + ba[:, sl])
        i = jax.nn.sigmoid(_dot(xb, wi_ref[n]) + bi[:, sl])
        log_a = -LRU_C * r * sp[:, sl]
        a = jnp.exp(log_a)
        a_parts.append(a)
        u_parts.append(jnp.sqrt(-jnp.tanh(log_a) * (a * a + 1.0)) * i * xn)
    return jnp.concatenate(a_parts, axis=1), jnp.concatenate(u_parts, axis=1)


def _lru_prompt_kernel(xr_ref, gr_ref, cw_ref, cb_ref, wa_ref, ba_ref, wi_ref, bi_ref, lam_ref,
                       o_ref, hl_ref, xp_ref, a_ref, u_ref, hs_ref, hc_ref, *, tt):
    j = pl.program_id(1)
    d = xr_ref.shape[1]

    @pl.when(j == 0)
    def _():
        xp_ref[0:SUBLANES, :] = jnp.zeros((SUBLANES, d), F32)
        hc_ref[...] = jnp.zeros_like(hc_ref)

    @pl.when(j > 0)
    def _():
        xp_ref[0:SUBLANES, :] = xp_ref[tt:tt + SUBLANES, :]

    xp_ref[SUBLANES:SUBLANES + tt, :] = xr_ref[...]
    cw = cw_ref[...]
    xc = cb_ref[...]
    for jj in range(CONV_W):
        lo = SUBLANES - (CONV_W - 1) + jj
        xc = xc + cw[jj:jj + 1, :] * xp_ref[lo:lo + tt, :]

    a, u = _lru_gates(xc, wa_ref, ba_ref[...], wi_ref, bi_ref[...], lam_ref[...])
    a_ref[...] = a
    u_ref[...] = u

    row = lax.broadcasted_iota(I32, (SUBLANES, d), 0)

    def group(g, h):
        r0 = pl.multiple_of(g * SUBLANES, SUBLANES)
        a8 = a_ref[pl.ds(r0, SUBLANES), :]
        u8 = u_ref[pl.ds(r0, SUBLANES), :]
        out = jnp.zeros((SUBLANES, d), F32)
        for jj in range(SUBLANES):
            aj = jnp.broadcast_to(a8[jj:jj + 1, :], (SUBLANES, d))
            uj = jnp.broadcast_to(u8[jj:jj + 1, :], (SUBLANES, d))
            h = aj * h + uj
            out = jnp.where(row == jj, h, out)
        hs_ref[pl.ds(r0, SUBLANES), :] = out
        return h

    h = lax.fori_loop(0, tt // SUBLANES, group, hc_ref[...])
    hc_ref[...] = h
    hl_ref[...] = h[0:1, :]
    o_ref[...] = (hs_ref[...] * jax.nn.gelu(gr_ref[...])).astype(BF16)


def _lru_prompt(xr, gr, cw, cb, wa, ba, wi, bi, lam, *, batch, tt):
    rows, d = xr.shape
    seq = rows // batch
    nt = seq // tt
    row_spec = pl.BlockSpec((tt, d), lambda b, j: (b * nt + j, 0))
    vec_spec = pl.BlockSpec((1, d), lambda b, j: (0, 0))
    w_spec = pl.BlockSpec(wa.shape, lambda b, j: (0, 0, 0))
    return pl.pallas_call(
        functools.partial(_lru_prompt_kernel, tt=tt),
        grid=(batch, nt),
        in_specs=[row_spec, row_spec, pl.BlockSpec((CONV_W, d), lambda b, j: (0, 0)), vec_spec,
                  w_spec, vec_spec, w_spec, vec_spec, vec_spec],
        out_specs=[row_spec, pl.BlockSpec((None, 1, d), lambda b, j: (b, 0, 0))],
        out_shape=[jax.ShapeDtypeStruct((rows, d), BF16), jax.ShapeDtypeStruct((batch, 1, d), F32)],
        scratch_shapes=[pltpu.VMEM((tt + SUBLANES, d), F32), pltpu.VMEM((tt, d), F32),
                        pltpu.VMEM((tt, d), F32), pltpu.VMEM((tt, d), F32), pltpu.VMEM((SUBLANES, d), F32)],
        compiler_params=_cparams(("parallel", "arbitrary")),
        name="lru_prompt",
    )(xr, gr, cw, cb, wa, ba, wi, bi, lam)


def _lru_sample_kernel(xp_ref, gr_ref, h0_ref, cw_ref, cb_ref, wa_ref, ba_ref, wi_ref, bi_ref, lam_ref,
                       o_ref, hl_ref, *, nb, nt):
    cw = cw_ref[...]
    xc = cb_ref[...] + cw[0:1, :] * xp_ref[0:nt * nb, :]
    for jj in range(1, CONV_W):
        xc = xc + cw[jj:jj + 1, :] * xp_ref[jj * nb:(jj + nt) * nb, :]
    a, u = _lru_gates(xc, wa_ref, ba_ref[...], wi_ref, bi_ref[...], lam_ref[...])
    gate = jax.nn.gelu(gr_ref[...])
    h = h0_ref[...]
    for t in range(nt):
        sl = slice(t * nb, (t + 1) * nb)
        h = a[sl, :] * h + u[sl, :]
        o_ref[sl, :] = (h * gate[sl, :]).astype(BF16)
    hl_ref[...] = h


def _lru_sample(xp, gr, h0, cw, cb, wa, ba, wi, bi, lam, *, nb, nt):
    d = gr.shape[1]
    return pl.pallas_call(
        functools.partial(_lru_sample_kernel, nb=nb, nt=nt),
        out_shape=[jax.ShapeDtypeStruct((nt * nb, d), BF16), jax.ShapeDtypeStruct((nb, d), F32)],
        compiler_params=pltpu.CompilerParams(vmem_limit_bytes=VMEM_LIMIT),
        name="lru_sample",
    )(xp, gr, h0, cw, cb, wa, ba, wi, bi, lam)


def _rank_to_float(u):
    key = u ^ jnp.int32(INT_MIN)
    return lax.bitcast_convert_type(key ^ ((key >> 31) & jnp.int32(0x7FFFFFFF)), F32)


def _count(sc_ref, nc, rows, ck, thr, strict):
    accs = []
    for r0 in range(0, rows, COUNT_ROWS):
        nr = min(COUNT_ROWS, rows - r0)
        thr_b = jnp.broadcast_to(thr[r0:r0 + nr], (nr, LANES))

        def body(c, acc, r0=r0, nr=nr, thr_b=thr_b):
            for t in range(ck // LANES):
                tile = sc_ref[c, r0:r0 + nr, t * LANES:(t + 1) * LANES]
                hit = (tile > thr_b) if strict else (tile >= thr_b)
                acc = acc + jnp.where(hit, 1.0, 0.0)
            return acc

        accs.append(lax.fori_loop(0, nc, body, jnp.zeros((nr, LANES), F32)))
    acc = accs[0] if len(accs) == 1 else jnp.concatenate(accs, axis=0)
    return jnp.sum(acc, axis=1, keepdims=True)


def _select_topk(sc_ref, tau_ref, nc, rows, ck, k_sel):
    kf = float(k_sel)

    def bit_body(it, carry):
        u, n_ge = carry
        cand = u | jnp.left_shift(jnp.int32(1), 31 - it)
        cnt = _count(sc_ref, nc, rows, ck, _rank_to_float(cand), strict=False)
        take = cnt >= kf
        return jnp.where(take, cand, u), jnp.where(take, cnt, n_ge)

    u, n_ge = lax.fori_loop(0, 32, bit_body, (jnp.zeros((rows, 1), I32), jnp.full((rows, 1), kf, F32)))
    tau = _rank_to_float(u)
    tau_ref[...] = jnp.broadcast_to(tau, (rows, LANES))

    @pl.when(jnp.max(n_ge) > kf)
    def _():
        before = (lax.broadcasted_iota(I32, (ck, ck), 0) < lax.broadcasted_iota(I32, (ck, ck), 1))
        before = jnp.where(before, 1.0, 0.0).astype(BF16)
        tau_w = jnp.broadcast_to(tau, (rows, ck))
        need = kf - _count(sc_ref, nc, rows, ck, tau, strict=True)
        need_w = jnp.broadcast_to(need, (rows, ck))

        def body(c, seen):
            sc = sc_ref[c]
            eq = sc == tau_w
            eqf = jnp.where(eq, 1.0, 0.0)
            rank = _dot(eqf.astype(BF16), before) + seen
            sc_ref[c] = jnp.where(eq, jnp.where(rank >= need_w, -jnp.inf, sc), sc)
            return seen + jnp.sum(eqf, axis=1, keepdims=True)

        lax.fori_loop(0, nc, body, jnp.zeros((rows, 1), F32))


def _prompt_attn_kernel(q_ref, qi_ref, kwq_ref, kwk_ref, kb_ref, vb_ref, o_ref,
                        ki_ref, sc_ref, tau_ref, qs_ref, m_ref, l_ref, acc_ref, *, k_sel, qb):
    i = pl.program_id(1)
    ck = sc_ref.shape[2]
    nc = (i * qb + qb + ck - 1) // ck

    @pl.when(i == 0)
    def _():
        ki_ref[...] = kwk_ref[:, 0:IDX_DIM].astype(BF16)

    w_idx = kwq_ref[:, IDX_DIM:IDX_DIM + IDX_HEADS] * IDX_W_SCALE * IDX_SCALE
    t_pos = i * qb + lax.broadcasted_iota(I32, (qb, ck), 0)
    col = lax.broadcasted_iota(I32, (qb, ck), 1)

    def score_body(c, carry):
        c0 = pl.multiple_of(c * ck, ck)
        kc = ki_ref[pl.ds(c0, ck), :]
        score = jnp.zeros((qb, ck), F32)
        for h in range(IDX_HEADS):
            s = _dot_nt(qi_ref[:, h * IDX_DIM:(h + 1) * IDX_DIM], kc)
            score = score + jnp.maximum(s, 0.0) * w_idx[:, h:h + 1]
        sc_ref[c] = jnp.where(col + c0 <= t_pos, score, -jnp.inf)
        return carry

    lax.fori_loop(0, nc, score_body, 0)

    need_select = (i + 1) * qb > k_sel

    @pl.when(jnp.logical_not(need_select))
    def _():
        tau_ref[...] = jnp.full(tau_ref.shape, FLT_LOWEST, F32)

    @pl.when(need_select)
    def _():
        _select_topk(sc_ref, tau_ref, nc, qb, ck, k_sel)

    for n in range(N_KV_HEADS):
        for g in range(GQA_GROUP):
            hq = n * GQA_GROUP + g
            qs_ref[n, g * qb:(g + 1) * qb, :] = q_ref[:, hq * HEAD_DIM:(hq + 1) * HEAD_DIM]
    m_ref[...] = jnp.full(m_ref.shape, NEG_BIG, F32)
    l_ref[...] = jnp.zeros_like(l_ref)
    acc_ref[...] = jnp.zeros_like(acc_ref)
    tau_w = jnp.broadcast_to(tau_ref[:, 0:1], (qb, ck))

    def attn_body(c, carry):
        c0 = pl.multiple_of(c * ck, ck)
        bias = jnp.where(sc_ref[c] >= tau_w, 0.0, NEG_BIG)
        bias = jnp.concatenate([bias] * GQA_GROUP, axis=0)
        for n in range(N_KV_HEADS):
            kn = kb_ref[pl.ds(c0, ck), n * HEAD_DIM:(n + 1) * HEAD_DIM]
            vn = vb_ref[pl.ds(c0, ck), n * HEAD_DIM:(n + 1) * HEAD_DIM]
            s = _dot_nt(qs_ref[n], kn) + bias
            tiles = [s[:, t * LANES:(t + 1) * LANES] for t in range(ck // LANES)]
            m_prev = m_ref[n]
            m_new = jnp.maximum(m_prev, jnp.max(functools.reduce(jnp.maximum, tiles), axis=1, keepdims=True))
            alpha = jnp.exp2((m_prev - m_new) * SOFTMAX_LOG2_SCALE)
            p_tiles = [jnp.exp2((t - m_new) * SOFTMAX_LOG2_SCALE) for t in tiles]
            l_ref[n] = alpha * l_ref[n] + functools.reduce(jnp.add, p_tiles)
            p = jnp.concatenate(p_tiles, axis=1).astype(BF16)
            acc_ref[n] = alpha * acc_ref[n] + _dot(p, vn)
            m_ref[n] = m_new
        return carry

    lax.fori_loop(0, nc, attn_body, 0)

    for n in range(N_KV_HEADS):
        o = acc_ref[n] / jnp.sum(l_ref[n], axis=1, keepdims=True)
        for g in range(GQA_GROUP):
            hq = n * GQA_GROUP + g
            o_ref[:, hq * HEAD_DIM:(hq + 1) * HEAD_DIM] = o[g * qb:(g + 1) * qb, :].astype(BF16)


def _prompt_attn(q, qi, kw, kb, vb, *, batch):
    rows = q.shape[0]
    seq = rows // batch
    ck = min(KEY_CHUNK, seq)
    k_sel = min(TOPK_MAX, seq // 4)
    qb = next(r for r in QUERY_ROWS if k_sel % r == 0 and seq % r == 0)
    nblk = seq // qb
    gq = GQA_GROUP * qb
    blk = lambda b, i: (b * nblk + i, 0)
    per_b = lambda b, i: (b, 0)
    return pl.pallas_call(
        functools.partial(_prompt_attn_kernel, k_sel=k_sel, qb=qb),
        grid=(batch, nblk),
        in_specs=[
            pl.BlockSpec((qb, q.shape[1]), blk),
            pl.BlockSpec((qb, qi.shape[1]), blk),
            pl.BlockSpec((qb, LANES), blk),
            pl.BlockSpec((seq, LANES), per_b),
            pl.BlockSpec((seq, kb.shape[1]), per_b),
            pl.BlockSpec((seq, vb.shape[1]), per_b),
        ],
        out_specs=pl.BlockSpec((qb, q.shape[1]), blk),
        out_shape=jax.ShapeDtypeStruct(q.shape, BF16),
        scratch_shapes=[
            pltpu.VMEM((seq, IDX_DIM), BF16),
            pltpu.VMEM((seq // ck, qb, ck), F32),
            pltpu.VMEM((qb, LANES), F32),
            pltpu.VMEM((N_KV_HEADS, gq, HEAD_DIM), BF16),
            pltpu.VMEM((N_KV_HEADS, gq, LANES), F32),
            pltpu.VMEM((N_KV_HEADS, gq, LANES), F32),
            pltpu.VMEM((N_KV_HEADS, gq, HEAD_DIM), F32),
        ],
        compiler_params=_cparams(("parallel", "arbitrary")),
        name="prompt_attn",
    )(q, qi, kw, kw, kb, vb)


def _sample_scores_kernel(pt_ref, qi_ref, w_ref, kin_ref, *rest, nt, npg):
    pages = rest[:npg]
    past_ref, new_ref, kc_ref = rest[npg:]
    c = pl.program_id(1)
    w = w_ref[...] * IDX_W_SCALE * IDX_SCALE
    qi = qi_ref[...]

    def scores(s):
        width = s.shape[1]
        s = jnp.maximum(s, 0.0) * jnp.concatenate([w] * (width // LANES), axis=1)
        return jnp.sum(s.reshape(nt, IDX_HEADS, width), axis=1)

    for p in range(npg):
        kc_ref[:, p * PAGE_SIZE:(p + 1) * PAGE_SIZE] = pages[p][...].astype(BF16)
    past_ref[...] = scores(_dot(qi, kc_ref[...]))

    @pl.when(c == 0)
    def _():
        new_ref[...] = scores(_dot_nt(qi, kin_ref[...].astype(BF16)))


def _sample_scores(page_table, qi_b, w_b, ki_new, cache_ki, *, layer, nt):
    nb, n_pages = page_table.shape
    npg = min(IDX_PAGES_PER_STEP, n_pages)
    rows = nt * IDX_HEADS
    page_specs = [
        pl.BlockSpec((None, None, IDX_DIM, PAGE_SIZE),
                     functools.partial(lambda b, c, pt, p: (layer, pt[b, c * npg + p], 0, 0), p=p))
        for p in range(npg)
    ]
    grid_spec = pltpu.PrefetchScalarGridSpec(
        num_scalar_prefetch=1,
        grid=(nb, n_pages // npg),
        in_specs=[
            pl.BlockSpec((None, rows, IDX_DIM), lambda b, c, pt: (b, 0, 0)),
            pl.BlockSpec((None, rows, LANES), lambda b, c, pt: (b, 0, 0)),
            pl.BlockSpec((None, PAGE_SIZE, IDX_DIM), lambda b, c, pt: (b, 0, 0)),
        ] + page_specs,
        out_specs=[
            pl.BlockSpec((None, nt, npg * PAGE_SIZE), lambda b, c, pt: (b, 0, c)),
            pl.BlockSpec((None, nt, PAGE_SIZE), lambda b, c, pt: (b, 0, 0)),
        ],
        scratch_shapes=[pltpu.VMEM((IDX_DIM, npg * PAGE_SIZE), BF16)],
    )
    return pl.pallas_call(
        functools.partial(_sample_scores_kernel, nt=nt, npg=npg),
        grid_spec=grid_spec,
        out_shape=[jax.ShapeDtypeStruct((nb, nt, n_pages * PAGE_SIZE), F32),
                   jax.ShapeDtypeStruct((nb, nt, PAGE_SIZE), F32)],
        compiler_params=_cparams(("parallel", "arbitrary")),
        name="sample_scores",
    )(page_table, qi_b, w_b, ki_new, *([cache_ki] * npg))


def _sample_select_kernel(past_ref, new_ref, bias_ref, sc_ref, tau_ref, *, nt, k_sel, ck):
    rows, n_past = past_ref.shape
    npc = n_past // ck
    for c in range(npc):
        sc_ref[c] = past_ref[:, c * ck:(c + 1) * ck]
    t_row = lax.broadcasted_iota(I32, (rows, ck), 0) & (nt - 1)
    col = lax.broadcasted_iota(I32, (rows, ck), 1)
    new_sc = jnp.concatenate([new_ref[...]] * (ck // PAGE_SIZE), axis=1)
    sc_ref[npc] = jnp.where(col <= t_row, new_sc, -jnp.inf)
    _select_topk(sc_ref, tau_ref, npc + 1, rows, ck, k_sel)
    tau_b = tau_ref[...]
    d = (lax.broadcasted_iota(I32, (LANES, LANES * N_KV_HEADS), 1)
         - N_KV_HEADS * lax.broadcasted_iota(I32, (LANES, LANES * N_KV_HEADS), 0))
    expand = jnp.where(d >= 0, jnp.where(d < N_KV_HEADS, 1.0, 0.0), 0.0).astype(BF16)
    for c in range(npc + 1):
        for t in range(ck // LANES if c < npc else PAGE_SIZE // LANES):
            sel = jnp.where(sc_ref[c][:, t * LANES:(t + 1) * LANES] >= tau_b, 1.0, 0.0).astype(BF16)
            lo = (c * ck + t * LANES) * N_KV_HEADS
            bias_ref[:, lo:lo + LANES * N_KV_HEADS] = (_dot(sel, expand) - 1.0) * (-NEG_BIG)


def _sample_select(past, new, *, nt):
    rows, n_past = past.shape
    assert nt & (nt - 1) == 0
    ck = min(KEY_CHUNK, n_past)
    k_sel = min(TOPK_MAX, (n_past + nt) // 4)
    return pl.pallas_call(
        functools.partial(_sample_select_kernel, nt=nt, k_sel=k_sel, ck=ck),
        out_shape=jax.ShapeDtypeStruct((rows, (n_past + PAGE_SIZE) * N_KV_HEADS), F32),
        scratch_shapes=[pltpu.VMEM((n_past // ck + 1, rows, ck), F32), pltpu.VMEM((rows, LANES), F32)],
        compiler_params=pltpu.CompilerParams(vmem_limit_bytes=VMEM_LIMIT),
        name="sample_select",
    )(past, new)


def _sample_attn_kernel(pt_ref, q_ref, bias_ref, biasn_ref, kn_ref, vn_ref, *rest, nt, npg):
    k_pages = rest[:npg]
    v_pages = rest[npg:2 * npg]
    o_ref, kc_ref, vc_ref, m_ref, l_ref, acc_ref = rest[2 * npg:]
    c = pl.program_id(1)
    rows = q_ref.shape[0]
    page_rows = PAGE_SIZE * N_KV_HEADS

    @pl.when(c == 0)
    def _():
        m_ref[...] = jnp.full(m_ref.shape, NEG_BIG, F32)
        l_ref[...] = jnp.zeros_like(l_ref)
        acc_ref[...] = jnp.zeros_like(acc_ref)

    row_head = lax.shift_right_logical(lax.broadcasted_iota(I32, (rows, page_rows), 0), _log2(rows // N_KV_HEADS))
    col_head = lax.broadcasted_iota(I32, (rows, page_rows), 1) & (N_KV_HEADS - 1)
    head_bias = jnp.where(row_head == col_head, 0.0, NEG_BIG)

    def update(k2, v2, bias):
        s = _dot_nt(q_ref[...], k2)
        bias = jnp.concatenate([bias] * (rows // nt), axis=0)
        lanes = lambda a, t: a[:, t * LANES:(t + 1) * LANES]
        per_page = page_rows // LANES
        tiles = [lanes(s, t) + (lanes(bias, t) + lanes(head_bias, t % per_page)) for t in range(k2.shape[0] // LANES)]
        m_prev = m_ref[...]
        m_new = jnp.maximum(m_prev, jnp.max(functools.reduce(jnp.maximum, tiles), axis=1, keepdims=True))
        alpha = jnp.exp2((m_prev - m_new) * SOFTMAX_LOG2_SCALE)
        p_tiles = [jnp.exp2((t - m_new) * SOFTMAX_LOG2_SCALE) for t in tiles]
        l_ref[...] = alpha * l_ref[...] + functools.reduce(jnp.add, p_tiles)
        acc_ref[...] = alpha * acc_ref[...] + _dot(jnp.concatenate(p_tiles, axis=1).astype(BF16), v2)
        m_ref[...] = m_new

    for p in range(npg):
        kc_ref[p * page_rows:(p + 1) * page_rows, :] = k_pages[p][...].reshape(page_rows, HEAD_DIM).astype(BF16)
        vc_ref[p * page_rows:(p + 1) * page_rows, :] = v_pages[p][...].reshape(page_rows, HEAD_DIM).astype(BF16)
    update(kc_ref[...], vc_ref[...], bias_ref[...])

    @pl.when(c == pl.num_programs(1) - 1)
    def _():
        update(kn_ref[...].astype(BF16), vn_ref[...].astype(BF16), biasn_ref[...])
        o_ref[...] = acc_ref[...] / jnp.sum(l_ref[...], axis=1, keepdims=True)


def _sample_attn(page_table, q_b, bias, k_new, v_new, cache_k, cache_v, *, layer, nt):
    nb, n_pages = page_table.shape
    npg = min(PAGES_PER_STEP, n_pages)
    rows = q_b.shape[1]
    page_rows = PAGE_SIZE * N_KV_HEADS
    assert N_KV_HEADS & (N_KV_HEADS - 1) == 0

    def page_spec(p):
        return pl.BlockSpec((None, None, PAGE_SIZE, N_KV_HEADS, HEAD_DIM),
                            functools.partial(lambda b, c, pt, p: (layer, pt[b, c * npg + p], 0, 0, 0), p=p))

    per_b3 = lambda b, c, pt: (b, 0, 0)
    grid_spec = pltpu.PrefetchScalarGridSpec(
        num_scalar_prefetch=1,
        grid=(nb, n_pages // npg),
        in_specs=[
            pl.BlockSpec((None, rows, HEAD_DIM), per_b3),
            pl.BlockSpec((None, nt, npg * page_rows), lambda b, c, pt: (b, 0, c)),
            pl.BlockSpec((None, nt, page_rows), lambda b, c, pt: (b, 0, n_pages)),
            pl.BlockSpec((None, page_rows, HEAD_DIM), per_b3),
            pl.BlockSpec((None, page_rows, HEAD_DIM), per_b3),
        ] + [page_spec(p) for p in range(npg)] + [page_spec(p) for p in range(npg)],
        out_specs=pl.BlockSpec((None, rows, HEAD_DIM), per_b3),
        scratch_shapes=[
            pltpu.VMEM((npg * page_rows, HEAD_DIM), BF16),
            pltpu.VMEM((npg * page_rows, HEAD_DIM), BF16),
            pltpu.VMEM((rows, LANES), F32),
            pltpu.VMEM((rows, LANES), F32),
            pltpu.VMEM((rows, HEAD_DIM), F32),
        ],
    )
    return pl.pallas_call(
        functools.partial(_sample_attn_kernel, nt=nt, npg=npg),
        grid_spec=grid_spec,
        out_shape=jax.ShapeDtypeStruct((nb, rows, HEAD_DIM), F32),
        compiler_params=_cparams(("parallel", "arbitrary")),
        name="sample_attn",
    )(page_table, q_b, bias, bias, k_new, v_new, *([cache_k] * npg), *([cache_v] * npg))


def _proj_out_kernel(x_ref, r_ref, a_ref, w_ref, g_ref, b_ref, o_ref, *, alpha):
    d_rnn = r_ref.shape[1]
    mix = _dot(r_ref[...], w_ref[0:d_rnn, :]) + _dot(a_ref[...], w_ref[d_rnn:, :])
    o_ref[...] = _layer_norm(alpha * x_ref[...] + mix, g_ref[...], b_ref[...])


def _proj_out_ln(x, rnn, attn, w, g, b, *, alpha, tm):
    rows, d = x.shape
    return pl.pallas_call(
        functools.partial(_proj_out_kernel, alpha=alpha),
        grid=(rows // tm,),
        in_specs=[
            pl.BlockSpec((tm, d), lambda i: (i, 0)),
            pl.BlockSpec((tm, rnn.shape[1]), lambda i: (i, 0)),
            pl.BlockSpec((tm, attn.shape[1]), lambda i: (i, 0)),
            pl.BlockSpec(w.shape, lambda i: (0, 0)),
            pl.BlockSpec((1, d), lambda i: (0, 0)),
            pl.BlockSpec((1, d), lambda i: (0, 0)),
        ],
        out_specs=pl.BlockSpec((tm, d), lambda i: (i, 0)),
        out_shape=jax.ShapeDtypeStruct((rows, d), F32),
        compiler_params=_cparams(("parallel",)),
        name="proj_out_ln",
    )(x, rnn, attn, w, g, b)


def _row_tile(rows, target):
    tm = min(rows, target)
    while rows % tm:
        tm //= 2
    return tm


def _pad_to(a, axis, size):
    pad = [(0, 0)] * a.ndim
    pad[axis] = (0, size - a.shape[axis])
    return jnp.pad(a, pad)


def _ffn_weights(w_gu, w_down):
    d_ff = w_down.shape[0]
    tf = 2 * MXU_DIM
    ffp = -(-d_ff // tf) * tf
    wg = _pad_to(w_gu[:, :d_ff].astype(BF16), 1, ffp)
    wu = _pad_to(w_gu[:, d_ff:].astype(BF16), 1, ffp)
    wd = _pad_to(w_down.astype(BF16), 0, ffp)
    return wg, wu, wd, tf


def kernel(x_prompt, x_sample, cache_k, cache_v, cache_k_idx, state_conv, state_rnn, page_table, ln1_g, ln1_b, ffn1_w_gu, ffn1_w_down, w_in, conv_w, conv_b, lru_w_a, lru_b_a, lru_w_i, lru_b_i, lru_lambda, w_out, ln2_g, ln2_b, ffn2_w_gu, ffn2_w_down, ln3_g, ln3_b):
    depth = w_in.shape[0]
    bp, seq, d_model = x_prompt.shape
    nb, nt, _ = x_sample.shape
    d_rnn = conv_w.shape[2]
    d_q = N_Q_HEADS * HEAD_DIM
    d_kv = N_KV_HEADS * HEAD_DIM
    d_qi = IDX_HEADS * IDX_DIM
    alpha = (2.0 * depth) ** 0.25

    xp = x_prompt.reshape(bp * seq, d_model)
    xs = x_sample.transpose(1, 0, 2).reshape(nt * nb, d_model)
    tm_p = _row_tile(bp * seq, 512)
    tm_s = _row_tile(nt * nb, 512)
    vec = lambda a: a.reshape(1, -1)

    outs = [[] for _ in range(10)]
    for l in range(depth):
        wg1, wu1, wd1, tf = _ffn_weights(ffn1_w_gu[l], ffn1_w_down[l])
        wg2, wu2, wd2, _ = _ffn_weights(ffn2_w_gu[l], ffn2_w_down[l])
        d_in = w_in.shape[2]
        w_in_b = _pad_to(w_in[l].astype(BF16), 1, d_in - IDX_DIM - IDX_HEADS + LANES)
        w_out_b = w_out[l].astype(BF16)
        wa_b = lru_w_a[l].astype(BF16)
        wi_b = lru_w_i[l].astype(BF16)
        lru_args = (conv_w[l], vec(conv_b[l]), wa_b, vec(lru_b_a[l]), wi_b, vec(lru_b_i[l]), vec(lru_lambda[l]))
        proj = functools.partial(_proj_in, d_rnn=d_rnn, d_q=d_q, d_kv=d_kv, d_qi=d_qi)

        x1p = _ffn_ln(xp, wg1, wu1, wd1, vec(ln1_g[l]), vec(ln1_b[l]), alpha=alpha, tm=tm_p, tf=tf)
        x1s = _ffn_ln(xs, wg1, wu1, wd1, vec(ln1_g[l]), vec(ln1_b[l]), alpha=alpha, tm=tm_s, tf=tf)
        xr_p, gr_p, q_p, k_p, v_p, kb_p, vb_p, qi_p, kw_p = proj(x1p, w_in_b, tm=_row_tile(bp * seq, 256))
        xr_s, gr_s, q_s, k_s, v_s, _, _, qi_s, kw_s = proj(x1s, w_in_b, tm=_row_tile(nt * nb, 256))

        rnn_p, hl_p = _lru_prompt(xr_p, gr_p, *lru_args, batch=bp, tt=_row_tile(seq, 512))
        conv_tm = state_conv[l].transpose(1, 0, 2).reshape((CONV_W - 1) * nb, d_rnn)
        xpad_s = jnp.concatenate([conv_tm, xr_s], axis=0)
        rnn_s, hl_s = _lru_sample(xpad_s, gr_s, state_rnn[l], *lru_args, nb=nb, nt=nt)

        attn_p = _prompt_attn(q_p, qi_p, kw_p, kb_p, vb_p, batch=bp)

        to_b = lambda a: a.reshape(nt, nb, -1).transpose(1, 0, 2)
        qi_b = to_b(qi_s).reshape(nb, nt * IDX_HEADS, IDX_DIM)
        kw_b = to_b(kw_s)
        w_b = jnp.broadcast_to(kw_b[:, :, IDX_DIM:IDX_DIM + IDX_HEADS].reshape(nb, nt * IDX_HEADS, 1),
                               (nb, nt * IDX_HEADS, LANES))
        ki_new = _pad_to(kw_b[:, :, :IDX_DIM], 1, PAGE_SIZE)
        cache_ki_t = jnp.swapaxes(cache_k_idx, 2, 3)
        past_sc, new_sc = _sample_scores(page_table, qi_b, w_b, ki_new, cache_ki_t, layer=l, nt=nt)
        n_past = past_sc.shape[2]
        bias = _sample_select(past_sc.reshape(nb * nt, n_past), new_sc.reshape(nb * nt, PAGE_SIZE), nt=nt)
        bias = bias.reshape(nb, nt, (n_past + PAGE_SIZE) * N_KV_HEADS)
        q_b = to_b(q_s).reshape(nb, nt, N_KV_HEADS, GQA_GROUP, HEAD_DIM)
        q_b = q_b.transpose(0, 2, 3, 1, 4).reshape(nb, N_Q_HEADS * nt, HEAD_DIM)
        new_page = lambda a: _pad_to(to_b(a).reshape(nb, nt * N_KV_HEADS, HEAD_DIM), 1, PAGE_SIZE * N_KV_HEADS)
        o_b = _sample_attn(page_table, q_b, bias, new_page(k_s), new_page(v_s), cache_k, cache_v, layer=l, nt=nt)
        attn_s = o_b.reshape(nb, N_KV_HEADS, GQA_GROUP, nt, HEAD_DIM).transpose(3, 0, 1, 2, 4)
        attn_s = attn_s.reshape(nt * nb, d_q).astype(BF16)

        x2p = _proj_out_ln(x1p, rnn_p, attn_p, w_out_b, vec(ln2_g[l]), vec(ln2_b[l]), alpha=alpha, tm=tm_p)
        x2s = _proj_out_ln(x1s, rnn_s, attn_s, w_out_b, vec(ln2_g[l]), vec(ln2_b[l]), alpha=alpha, tm=tm_s)
        xp = _ffn_ln(x2p, wg2, wu2, wd2, vec(ln3_g[l]), vec(ln3_b[l]), alpha=alpha, tm=tm_p, tf=tf)
        xs = _ffn_ln(x2s, wg2, wu2, wd2, vec(ln3_g[l]), vec(ln3_b[l]), alpha=alpha, tm=tm_s, tf=tf)

        xr_p3 = xr_p.reshape(bp, seq, d_rnn)
        conv_p = xr_p3[:, -(CONV_W - 1):]
        conv_s = xpad_s.reshape(CONV_W - 1 + nt, nb, d_rnn)[-(CONV_W - 1):].transpose(1, 0, 2)
        layer_out = (
            k_p.reshape(bp, seq, N_KV_HEADS, HEAD_DIM), v_p.reshape(bp, seq, N_KV_HEADS, HEAD_DIM),
            kw_p[:, :IDX_DIM].reshape(bp, seq, IDX_DIM), conv_p, hl_p.reshape(bp, d_rnn),
            to_b(k_s).reshape(nb, nt, N_KV_HEADS, HEAD_DIM), to_b(v_s).reshape(nb, nt, N_KV_HEADS, HEAD_DIM),
            kw_b[:, :, :IDX_DIM], conv_s, hl_s,
        )
        for acc, val in zip(outs, layer_out):
            acc.append(val)

    y_p = xp.reshape(bp, seq, d_model)
    y_s = xs.reshape(nt, nb, d_model).transpose(1, 0, 2)
    return (y_p, y_s) + tuple(jnp.stack(o) for o in outs)
```

```python
import functools

import jax
import jax.numpy as jnp
from jax import lax
from jax.experimental import pallas as pl
from jax.experimental.pallas import tpu as pltpu

F32 = jnp.float32
BF16 = jnp.bfloat16
I32 = jnp.int32

LRU_BLOCKS = 8
CONV_W = 4
LRU_C = 8.0
HEAD_DIM = 128
N_KV_HEADS = 4
GQA_GROUP = 2
N_Q_HEADS = N_KV_HEADS * GQA_GROUP
IDX_HEADS = 8
IDX_DIM = 64
TOPK_MAX = 256
QUERY_ROWS = (256, 128)
COUNT_ROWS = 128
PAGE_SIZE = 128
LN_EPS = 1e-5
ATTN_SCALE = HEAD_DIM ** -0.5
SOFTMAX_LOG2_SCALE = ATTN_SCALE * 1.4426950408889634
IDX_SCALE = IDX_DIM ** -0.5
IDX_W_SCALE = IDX_HEADS ** -0.5

LANES = 128
SUBLANES = 8
MXU_DIM = 256
VMEM_LIMIT = 56 * 1024 * 1024

INT_MIN = -2 ** 31
FLT_LOWEST = -3.4028234663852886e38
NEG_BIG = -1e30
KEY_CHUNK = 512
PAGES_PER_STEP = 16
IDX_PAGES_PER_STEP = 16


def _cparams(semantics):
    return pltpu.CompilerParams(dimension_semantics=semantics, vmem_limit_bytes=VMEM_LIMIT)


def _layer_norm(y, g, b):
    mu = jnp.mean(y, axis=-1, keepdims=True)
    d = y - mu
    var = jnp.mean(d * d, axis=-1, keepdims=True)
    return d * lax.rsqrt(var + LN_EPS) * g + b


def _dot(a, b):
    return jnp.dot(a, b, preferred_element_type=F32)


def _dot_nt(a, b):
    return lax.dot_general(a, b, (((1,), (1,)), ((), ())), preferred_element_type=F32)


def _log2(n):
    assert n > 0 and n & (n - 1) == 0, n
    return n.bit_length() - 1


def _ffn_kernel(x_ref, wg_ref, wu_ref, wd_ref, g_ref, b_ref, o_ref, xb_ref, acc_ref, *, alpha, tail):
    j = pl.program_id(1)

    @pl.when(j == 0)
    def _():
        xb_ref[...] = x_ref[...].astype(BF16)
        acc_ref[...] = jnp.zeros_like(acc_ref)

    last = pl.num_programs(1) - 1

    def chunk(width):
        xb = xb_ref[...]
        gate = _dot(xb, wg_ref[:, 0:width])
        up = _dot(xb, wu_ref[:, 0:width])
        act = (gate * jax.nn.sigmoid(gate) * up).astype(BF16)
        acc_ref[...] += _dot(act, wd_ref[0:width, :])

    if tail == wg_ref.shape[1]:
        chunk(tail)
    else:
        pl.when(j < last)(lambda: chunk(wg_ref.shape[1]))
        pl.when(j == last)(lambda: chunk(tail))

    @pl.when(j == last)
    def _():
        y = alpha * x_ref[...] + 0.5 * acc_ref[...]
        o_ref[...] = _layer_norm(y, g_ref[...], b_ref[...])


def _ffn_ln(x, wg, wu, wd, g, b, *, alpha, tm, tf):
    rows, d = x.shape
    d_ff = wg.shape[1]
    nj = pl.cdiv(d_ff, tf)
    return pl.pallas_call(
        functools.partial(_ffn_kernel, alpha=alpha, tail=d_ff - (nj - 1) * tf),
        grid=(rows // tm, nj),
        in_specs=[
            pl.BlockSpec((tm, d), lambda i, j: (i, 0)),
            pl.BlockSpec((d, tf), lambda i, j: (0, j)),
            pl.BlockSpec((d, tf), lambda i, j: (0, j)),
            pl.BlockSpec((tf, d), lambda i, j: (j, 0)),
            pl.BlockSpec((1, d), lambda i, j: (0, 0)),
            pl.BlockSpec((1, d), lambda i, j: (0, 0)),
        ],
        out_specs=pl.BlockSpec((tm, d), lambda i, j: (i, 0)),
        out_shape=jax.ShapeDtypeStruct((rows, d), F32),
        scratch_shapes=[pltpu.VMEM((tm, d), BF16), pltpu.VMEM((tm, d), F32)],
        compiler_params=_cparams(("parallel", "arbitrary")),
        name="ffn_ln",
    )(x, wg, wu, wd, g, b)


def _proj_in_kernel(x_ref, w_ref, xr_ref, gr_ref, q_ref, k_ref, v_ref, kb_ref, vb_ref, qi_ref, kw_ref,
                    *, d_rnn, d_q, d_kv, d_qi):
    xb = x_ref[...].astype(BF16)
    off = [0]

    def seg(width):
        lo = off[0]
        off[0] = lo + width
        return _dot(xb, w_ref[:, lo:lo + width])

    xr_ref[...] = seg(d_rnn)
    gr_ref[...] = seg(d_rnn)
    q_ref[...] = seg(d_q).astype(BF16)
    k = seg(d_kv)
    k_ref[...] = k
    kb_ref[...] = k.astype(BF16)
    v = seg(d_kv)
    v_ref[...] = v
    vb_ref[...] = v.astype(BF16)
    qi_ref[...] = seg(d_qi).astype(BF16)
    kw_ref[...] = seg(LANES)


def _proj_in(x, w, *, tm, d_rnn, d_q, d_kv, d_qi):
    rows, d = x.shape
    widths = (d_rnn, d_rnn, d_q, d_kv, d_kv, d_kv, d_kv, d_qi, LANES)
    dtypes = (F32, F32, BF16, F32, F32, BF16, BF16, BF16, F32)
    return pl.pallas_call(
        functools.partial(_proj_in_kernel, d_rnn=d_rnn, d_q=d_q, d_kv=d_kv, d_qi=d_qi),
        grid=(rows // tm,),
        in_specs=[
            pl.BlockSpec((tm, d), lambda i: (i, 0)),
            pl.BlockSpec(w.shape, lambda i: (0, 0), pipeline_mode=pl.Buffered(1)),
        ],
        out_specs=[pl.BlockSpec((tm, wd), lambda i: (i, 0)) for wd in widths],
        out_shape=[jax.ShapeDtypeStruct((rows, wd), dt) for wd, dt in zip(widths, dtypes)],
        compiler_params=_cparams(("parallel",)),
        name="proj_in",
    )(x, w)


def _softplus(z):
    return jnp.maximum(z, 0.0) + jnp.log1p(jnp.exp(-jnp.abs(z)))


def _lru_gates(xc, wa_ref, ba, wi_ref, bi, lam):
    bw = xc.shape[1] // LRU_BLOCKS
    sp = _softplus(-lam)
    a_parts, u_parts = [], []
    for n in range(LRU_BLOCKS):
        sl = slice(n * bw, (n + 1) * bw)
        xn = xc[:, sl]
        xb = xn.astype(BF16)
        r = jax.nn.sigmoid(_dot(xb, wa_ref[n]) + ba[:, sl])
        i = jax.nn.sigmoid(_dot(xb, wi_ref[n]) + bi[:, sl])
        log_a = -LRU_C * r * sp[:, sl]
        a = jnp.exp(log_a)
        a_parts.append(a)
        u_parts.append(jnp.sqrt(-jnp.tanh(log_a) * (a * a + 1.0)) * i * xn)
    return jnp.concatenate(a_parts, axis=1), jnp.concatenate(u_parts, axis=1)


def _lru_prompt_kernel(xr_ref, gr_ref, cw_ref, cb_ref, wa_ref, ba_ref, wi_ref, bi_ref, lam_ref,
                       o_ref, hl_ref, xp_ref, a_ref, u_ref, hs_ref, hc_ref, *, tt):
    j = pl.program_id(1)
    d = xr_ref.shape[1]

    @pl.when(j == 0)
    def _():
        xp_ref[0:SUBLANES, :] = jnp.zeros((SUBLANES, d), F32)
        hc_ref[...] = jnp.zeros_like(hc_ref)

    @pl.when(j > 0)
    def _():
        xp_ref[0:SUBLANES, :] = xp_ref[tt:tt + SUBLANES, :]

    xp_ref[SUBLANES:SUBLANES + tt, :] = xr_ref[...]
    cw = cw_ref[...]
    xc = cb_ref[...]
    for jj in range(CONV_W):
        lo = SUBLANES - (CONV_W - 1) + jj
        xc = xc + cw[jj:jj + 1, :] * xp_ref[lo:lo + tt, :]

    a, u = _lru_gates(xc, wa_ref, ba_ref[...], wi_ref, bi_ref[...], lam_ref[...])
    a_ref[...] = a
    u_ref[...] = u

    row = lax.broadcasted_iota(I32, (SUBLANES, d), 0)

    def group(g, h):
        r0 = pl.multiple_of(g * SUBLANES, SUBLANES)
        a8 = a_ref[pl.ds(r0, SUBLANES), :]
        u8 = u_ref[pl.ds(r0, SUBLANES), :]
        out = jnp.zeros((SUBLANES, d), F32)
        for jj in range(SUBLANES):
            aj = jnp.broadcast_to(a8[jj:jj + 1, :], (SUBLANES, d))
            uj = jnp.broadcast_to(u8[jj:jj + 1, :], (SUBLANES, d))
            h = aj * h + uj
            out = jnp.where(row == jj, h, out)
        hs_ref[pl.ds(r0, SUBLANES), :] = out
        return h

    h = lax.fori_loop(0, tt // SUBLANES, group, hc_ref[...])
    hc_ref[...] = h
    hl_ref[...] = h[0:1, :]
    o_ref[...] = (hs_ref[...] * jax.nn.gelu(gr_ref[...])).astype(BF16)


def _lru_prompt(xr, gr, cw, cb, wa, ba, wi, bi, lam, *, batch, tt):
    rows, d = xr.shape
    seq = rows // batch
    nt = seq // tt
    row_spec = pl.BlockSpec((tt, d), lambda b, j: (b * nt + j, 0))
    vec_spec = pl.BlockSpec((1, d), lambda b, j: (0, 0))
    w_spec = pl.BlockSpec(wa.shape, lambda b, j: (0, 0, 0))
    return pl.pallas_call(
        functools.partial(_lru_prompt_kernel, tt=tt),
        grid=(batch, nt),
        in_specs=[row_spec, row_spec, pl.BlockSpec((CONV_W, d), lambda b, j: (0, 0)), vec_spec,
                  w_spec, vec_spec, w_spec, vec_spec, vec_spec],
        out_specs=[row_spec, pl.BlockSpec((None, 1, d), lambda b, j: (b, 0, 0))],
        out_shape=[jax.ShapeDtypeStruct((rows, d), BF16), jax.ShapeDtypeStruct((batch, 1, d), F32)],
        scratch_shapes=[pltpu.VMEM((tt + SUBLANES, d), F32), pltpu.VMEM((tt, d), F32),
                        pltpu.VMEM((tt, d), F32), pltpu.VMEM((tt, d), F32), pltpu.VMEM((SUBLANES, d), F32)],
        compiler_params=_cparams(("parallel", "arbitrary")),
        name="lru_prompt",
    )(xr, gr, cw, cb, wa, ba, wi, bi, lam)


def _lru_sample_kernel(xp_ref, gr_ref, h0_ref, cw_ref, cb_ref, wa_ref, ba_ref, wi_ref, bi_ref, lam_ref,
                       o_ref, hl_ref, *, nb, nt):
    cw = cw_ref[...]
    xc = cb_ref[...] + cw[0:1, :] * xp_ref[0:nt * nb, :]
    for jj in range(1, CONV_W):
        xc = xc + cw[jj:jj + 1, :] * xp_ref[jj * nb:(jj + nt) * nb, :]
    a, u = _lru_gates(xc, wa_ref, ba_ref[...], wi_ref, bi_ref[...], lam_ref[...])
    gate = jax.nn.gelu(gr_ref[...])
    h = h0_ref[...]
    for t in range(nt):
        sl = slice(t * nb, (t + 1) * nb)
        h = a[sl, :] * h + u[sl, :]
        o_ref[sl, :] = (h * gate[sl, :]).astype(BF16)
    hl_ref[...] = h


def _lru_sample(xp, gr, h0, cw, cb, wa, ba, wi, bi, lam, *, nb, nt):
    d = gr.shape[1]
    return pl.pallas_call(
        functools.partial(_lru_sample_kernel, nb=nb, nt=nt),
        out_shape=[jax.ShapeDtypeStruct((nt * nb, d), BF16), jax.ShapeDtypeStruct((nb, d), F32)],
        compiler_params=pltpu.CompilerParams(vmem_limit_bytes=VMEM_LIMIT),
        name="lru_sample",
    )(xp, gr, h0, cw, cb, wa, ba, wi, bi, lam)


def _rank_to_float(u):
    key = u ^ jnp.int32(INT_MIN)
    return lax.bitcast_convert_type(key ^ ((key >> 31) & jnp.int32(0x7FFFFFFF)), F32)


def _count(sc_ref, nc, rows, ck, thr, strict):
    accs = []
    for r0 in range(0, rows, COUNT_ROWS):
        nr = min(COUNT_ROWS, rows - r0)
        thr_b = jnp.broadcast_to(thr[r0:r0 + nr], (nr, LANES))

        def body(c, acc, r0=r0, nr=nr, thr_b=thr_b):
            for t in range(ck // LANES):
                tile = sc_ref[c, r0:r0 + nr, t * LANES:(t + 1) * LANES]
                hit = (tile > thr_b) if strict else (tile >= thr_b)
                acc = acc + jnp.where(hit, 1.0, 0.0)
            return acc

        accs.append(lax.fori_loop(0, nc, body, jnp.zeros((nr, LANES), F32)))
    acc = accs[0] if len(accs) == 1 else jnp.concatenate(accs, axis=0)
    return jnp.sum(acc, axis=1, keepdims=True)


def _select_topk(sc_ref, tau_ref, nc, rows, ck, k_sel):
    kf = float(k_sel)

    def bit_body(it, carry):
        u, n_ge = carry
        cand = u | jnp.left_shift(jnp.int32(1), 31 - it)
        cnt = _count(sc_ref, nc, rows, ck, _rank_to_float(cand), strict=False)
        take = cnt >= kf
        return jnp.where(take, cand, u), jnp.where(take, cnt, n_ge)

    u, n_ge = lax.fori_loop(0, 32, bit_body, (jnp.zeros((rows, 1), I32), jnp.full((rows, 1), kf, F32)))
    tau = _rank_to_float(u)
    tau_ref[...] = jnp.broadcast_to(tau, (rows, LANES))

    @pl.when(jnp.max(n_ge) > kf)
    def _():
        before = (lax.broadcasted_iota(I32, (ck, ck), 0) < lax.broadcasted_iota(I32, (ck, ck), 1))
        before = jnp.where(before, 1.0, 0.0).astype(BF16)
        tau_w = jnp.broadcast_to(tau, (rows, ck))
        need = kf - _count(sc_ref, nc, rows, ck, tau, strict=True)
        need_w = jnp.broadcast_to(need, (rows, ck))

        def body(c, seen):
            sc = sc_ref[c]
            eq = sc == tau_w
            eqf = jnp.where(eq, 1.0, 0.0)
            rank = _dot(eqf.astype(BF16), before) + seen
            sc_ref[c] = jnp.where(eq, jnp.where(rank >= need_w, -jnp.inf, sc), sc)
            return seen + jnp.sum(eqf, axis=1, keepdims=True)

        lax.fori_loop(0, nc, body, jnp.zeros((rows, 1), F32))


def _prompt_attn_kernel(q_ref, qi_ref, kwq_ref, kwk_ref, kb_ref, vb_ref, o_ref,
                        ki_ref, sc_ref, tau_ref, qs_ref, m_ref, l_ref, acc_ref, *, k_sel, qb):
    i = pl.program_id(1)
    ck = sc_ref.shape[2]
    nc = (i * qb + qb + ck - 1) // ck

    @pl.when(i == 0)
    def _():
        ki_ref[...] = kwk_ref[:, 0:IDX_DIM].astype(BF16)

    w_idx = kwq_ref[:, IDX_DIM:IDX_DIM + IDX_HEADS] * IDX_W_SCALE * IDX_SCALE
    t_pos = i * qb + lax.broadcasted_iota(I32, (qb, ck), 0)
    col = lax.broadcasted_iota(I32, (qb, ck), 1)

    def score_body(c, carry):
        c0 = pl.multiple_of(c * ck, ck)
        kc = ki_ref[pl.ds(c0, ck), :]
        score = jnp.zeros((qb, ck), F32)
        for h in range(IDX_HEADS):
            s = _dot_nt(qi_ref[:, h * IDX_DIM:(h + 1) * IDX_DIM], kc)
            score = score + jnp.maximum(s, 0.0) * w_idx[:, h:h + 1]
        sc_ref[c] = jnp.where(col + c0 <= t_pos, score, -jnp.inf)
        return carry

    lax.fori_loop(0, nc, score_body, 0)

    need_select = (i + 1) * qb > k_sel

    @pl.when(jnp.logical_not(need_select))
    def _():
        tau_ref[...] = jnp.full(tau_ref.shape, FLT_LOWEST, F32)

    @pl.when(need_select)
    def _():
        _select_topk(sc_ref, tau_ref, nc, qb, ck, k_sel)

    for n in range(N_KV_HEADS):
        for g in range(GQA_GROUP):
            hq = n * GQA_GROUP + g
            qs_ref[n, g * qb:(g + 1) * qb, :] = q_ref[:, hq * HEAD_DIM:(hq + 1) * HEAD_DIM]
    m_ref[...] = jnp.full(m_ref.shape, NEG_BIG, F32)
    l_ref[...] = jnp.zeros_like(l_ref)
    acc_ref[...] = jnp.zeros_like(acc_ref)
    tau_w = jnp.broadcast_to(tau_ref[:, 0:1], (qb, ck))

    def attn_body(c, carry):
        c0 = pl.multiple_of(c * ck, ck)
        bias = jnp.where(sc_ref[c] >= tau_w, 0.0, NEG_BIG)
        bias = jnp.concatenate([bias] * GQA_GROUP, axis=0)
        for n in range(N_KV_HEADS):
            kn = kb_ref[pl.ds(c0, ck), n * HEAD_DIM:(n + 1) * HEAD_DIM]
            vn = vb_ref[pl.ds(c0, ck), n * HEAD_DIM:(n + 1) * HEAD_DIM]
            s = _dot_nt(qs_ref[n], kn) + bias
            tiles = [s[:, t * LANES:(t + 1) * LANES] for t in range(ck // LANES)]
            m_prev = m_ref[n]
            m_new = jnp.maximum(m_prev, jnp.max(functools.reduce(jnp.maximum, tiles), axis=1, keepdims=True))
            alpha = jnp.exp2((m_prev - m_new) * SOFTMAX_LOG2_SCALE)
            p_tiles = [jnp.exp2((t - m_new) * SOFTMAX_LOG2_SCALE) for t in tiles]
            l_ref[n] = alpha * l_ref[n] + functools.reduce(jnp.add, p_tiles)
            p = jnp.concatenate(p_tiles, axis=1).astype(BF16)
            acc_ref[n] = alpha * acc_ref[n] + _dot(p, vn)
            m_ref[n] = m_new
        return carry

    lax.fori_loop(0, nc, attn_body, 0)

    for n in range(N_KV_HEADS):
        o = acc_ref[n] / jnp.sum(l_ref[n], axis=1, keepdims=True)
        for g in range(GQA_GROUP):
            hq = n * GQA_GROUP + g
            o_ref[:, hq * HEAD_DIM:(hq + 1) * HEAD_DIM] = o[g * qb:(g + 1) * qb, :].astype(BF16)


def _prompt_attn(q, qi, kw, kb, vb, *, batch):
    rows = q.shape[0]
    seq = rows // batch
    ck = min(KEY_CHUNK, seq)
    k_sel = min(TOPK_MAX, seq // 4)
    qb = next(r for r in QUERY_ROWS if k_sel % r == 0 and seq % r == 0)
    nblk = seq // qb
    gq = GQA_GROUP * qb
    blk = lambda b, i: (b * nblk + i, 0)
    per_b = lambda b, i: (b, 0)
    return pl.pallas_call(
        functools.partial(_prompt_attn_kernel, k_sel=k_sel, qb=qb),
        grid=(batch, nblk),
        in_specs=[
            pl.BlockSpec((qb, q.shape[1]), blk),
            pl.BlockSpec((qb, qi.shape[1]), blk),
            pl.BlockSpec((qb, LANES), blk),
            pl.BlockSpec((seq, LANES), per_b),
            pl.BlockSpec((seq, kb.shape[1]), per_b),
            pl.BlockSpec((seq, vb.shape[1]), per_b),
        ],
        out_specs=pl.BlockSpec((qb, q.shape[1]), blk),
        out_shape=jax.ShapeDtypeStruct(q.shape, BF16),
        scratch_shapes=[
            pltpu.VMEM((seq, IDX_DIM), BF16),
            pltpu.VMEM((seq // ck, qb, ck), F32),
            pltpu.VMEM((qb, LANES), F32),
            pltpu.VMEM((N_KV_HEADS, gq, HEAD_DIM), BF16),
            pltpu.VMEM((N_KV_HEADS, gq, LANES), F32),
            pltpu.VMEM((N_KV_HEADS, gq, LANES), F32),
            pltpu.VMEM((N_KV_HEADS, gq, HEAD_DIM), F32),
        ],
        compiler_params=_cparams(("parallel", "arbitrary")),
        name="prompt_attn",
    )(q, qi, kw, kw, kb, vb)


def _sample_scores_kernel(pt_ref, qi_ref, w_ref, kin_ref, *rest, nt, npg):
    pages = rest[:npg]
    past_ref, new_ref, kc_ref = rest[npg:]
    c = pl.program_id(1)
    w = w_ref[...] * IDX_W_SCALE * IDX_SCALE
    qi = qi_ref[...]

    def scores(s):
        width = s.shape[1]
        s = jnp.maximum(s, 0.0) * jnp.concatenate([w] * (width // LANES), axis=1)
        return jnp.sum(s.reshape(nt, IDX_HEADS, width), axis=1)

    for p in range(npg):
        kc_ref[:, p * PAGE_SIZE:(p + 1) * PAGE_SIZE] = pages[p][...].astype(BF16)
    past_ref[...] = scores(_dot(qi, kc_ref[...]))

    @pl.when(c == 0)
    def _():
        new_ref[...] = scores(_dot_nt(qi, kin_ref[...].astype(BF16)))


def _sample_scores(page_table, qi_b, w_b, ki_new, cache_ki, *, layer, nt):
    nb, n_pages = page_table.shape
    npg = min(IDX_PAGES_PER_STEP, n_pages)
    rows = nt * IDX_HEADS
    page_specs = [
        pl.BlockSpec((None, None, IDX_DIM, PAGE_SIZE),
                     functools.partial(lambda b, c, pt, p: (layer, pt[b, c * npg + p], 0, 0), p=p))
        for p in range(npg)
    ]
    grid_spec = pltpu.PrefetchScalarGridSpec(
        num_scalar_prefetch=1,
        grid=(nb, n_pages // npg),
        in_specs=[
            pl.BlockSpec((None, rows, IDX_DIM), lambda b, c, pt: (b, 0, 0)),
            pl.BlockSpec((None, rows, LANES), lambda b, c, pt: (b, 0, 0)),
            pl.BlockSpec((None, PAGE_SIZE, IDX_DIM), lambda b, c, pt: (b, 0, 0)),
        ] + page_specs,
        out_specs=[
            pl.BlockSpec((None, nt, npg * PAGE_SIZE), lambda b, c, pt: (b, 0, c)),
            pl.BlockSpec((None, nt, PAGE_SIZE), lambda b, c, pt: (b, 0, 0)),
        ],
        scratch_shapes=[pltpu.VMEM((IDX_DIM, npg * PAGE_SIZE), BF16)],
    )
    return pl.pallas_call(
        functools.partial(_sample_scores_kernel, nt=nt, npg=npg),
        grid_spec=grid_spec,
        out_shape=[jax.ShapeDtypeStruct((nb, nt, n_pages * PAGE_SIZE), F32),
                   jax.ShapeDtypeStruct((nb, nt, PAGE_SIZE), F32)],
        compiler_params=_cparams(("parallel", "arbitrary")),
        name="sample_scores",
    )(page_table, qi_b, w_b, ki_new, *([cache_ki] * npg))


def _sample_select_kernel(past_ref, new_ref, bias_ref, biasn_ref, sc_ref, tau_ref, *, nt, k_sel, ck):
    rows, n_past = past_ref.shape
    npc = n_past // ck
    for c in range(npc):
        sc_ref[c] = past_ref[:, c * ck:(c + 1) * ck]
    t_row = lax.broadcasted_iota(I32, (rows, ck), 0) & (nt - 1)
    col = lax.broadcasted_iota(I32, (rows, ck), 1)
    new_sc = jnp.concatenate([new_ref[...]] * (ck // PAGE_SIZE), axis=1)
    sc_ref[npc] = jnp.where(col <= t_row, new_sc, -jnp.inf)
    _select_topk(sc_ref, tau_ref, npc + 1, rows, ck, k_sel)
    tau_b = tau_ref[...]
    d = (lax.broadcasted_iota(I32, (LANES, LANES * N_KV_HEADS), 1)
         - N_KV_HEADS * lax.broadcasted_iota(I32, (LANES, LANES * N_KV_HEADS), 0))
    expand = jnp.where(d >= 0, jnp.where(d < N_KV_HEADS, 1.0, 0.0), 0.0).astype(BF16)
    def bias_of(c, t):
        sel = jnp.where(sc_ref[c][:, t * LANES:(t + 1) * LANES] >= tau_b, 1.0, 0.0).astype(BF16)
        return (_dot(sel, expand) - 1.0) * (-NEG_BIG)

    for c in range(npc):
        for t in range(ck // LANES):
            lo = (c * ck + t * LANES) * N_KV_HEADS
            bias_ref[:, lo:lo + LANES * N_KV_HEADS] = bias_of(c, t)
    biasn_ref[...] = bias_of(npc, 0)[:, 0:LANES]


def _sample_select(past, new, *, nt):
    rows, n_past = past.shape
    assert nt & (nt - 1) == 0 and nt * N_KV_HEADS <= LANES
    ck = min(KEY_CHUNK, n_past)
    k_sel = min(TOPK_MAX, (n_past + nt) // 4)
    return pl.pallas_call(
        functools.partial(_sample_select_kernel, nt=nt, k_sel=k_sel, ck=ck),
        out_shape=[jax.ShapeDtypeStruct((rows, n_past * N_KV_HEADS), F32), jax.ShapeDtypeStruct((rows, LANES), F32)],
        scratch_shapes=[pltpu.VMEM((n_past // ck + 1, rows, ck), F32), pltpu.VMEM((rows, LANES), F32)],
        compiler_params=pltpu.CompilerParams(vmem_limit_bytes=VMEM_LIMIT),
        name="sample_select",
    )(past, new)


def _sample_attn_kernel(pt_ref, q_ref, bias_ref, biasn_ref, kn_ref, vn_ref, *rest, nt, npg):
    k_pages = rest[:npg]
    v_pages = rest[npg:2 * npg]
    o_ref, kc_ref, vc_ref, m_ref, l_ref, acc_ref = rest[2 * npg:]
    c = pl.program_id(1)
    rows = q_ref.shape[0]
    page_rows = PAGE_SIZE * N_KV_HEADS

    @pl.when(c == 0)
    def _():
        m_ref[...] = jnp.full(m_ref.shape, NEG_BIG, F32)
        l_ref[...] = jnp.zeros_like(l_ref)
        acc_ref[...] = jnp.zeros_like(acc_ref)

    row_head = lax.shift_right_logical(lax.broadcasted_iota(I32, (rows, page_rows), 0), _log2(rows // N_KV_HEADS))
    col_head = lax.broadcasted_iota(I32, (rows, page_rows), 1) & (N_KV_HEADS - 1)
    head_bias = jnp.where(row_head == col_head, 0.0, NEG_BIG)
    first_of_pair = pl.program_id(0) % 2 == 0

    def update(k2, v2, bias_blk):
        s = _dot_nt(q_ref[...], k2)
        bias = jnp.where(first_of_pair, bias_blk[0:nt], bias_blk[nt:2 * nt])
        bias = jnp.concatenate([bias] * (rows // nt), axis=0)
        lanes = lambda a, t: a[:, t * LANES:(t + 1) * LANES]
        per_page = page_rows // LANES
        tiles = [lanes(s, t) + (lanes(bias, t) + lanes(head_bias, t % per_page)) for t in range(k2.shape[0] // LANES)]
        m_prev = m_ref[...]
        m_new = jnp.maximum(m_prev, jnp.max(functools.reduce(jnp.maximum, tiles), axis=1, keepdims=True))
        alpha = jnp.exp2((m_prev - m_new) * SOFTMAX_LOG2_SCALE)
        p_tiles = [jnp.exp2((t - m_new) * SOFTMAX_LOG2_SCALE) for t in tiles]
        l_ref[...] = alpha * l_ref[...] + functools.reduce(jnp.add, p_tiles)
        acc_ref[...] = alpha * acc_ref[...] + _dot(jnp.concatenate(p_tiles, axis=1).astype(BF16), v2)
        m_ref[...] = m_new

    for p in range(npg):
        kc_ref[p * page_rows:(p + 1) * page_rows, :] = k_pages[p][...].reshape(page_rows, HEAD_DIM).astype(BF16)
        vc_ref[p * page_rows:(p + 1) * page_rows, :] = v_pages[p][...].reshape(page_rows, HEAD_DIM).astype(BF16)
    update(kc_ref[...], vc_ref[...], bias_ref[...])

    @pl.when(c == pl.num_programs(1) - 1)
    def _():
        def new_rows(ref):
            pad = jnp.zeros((LANES - ref.shape[0], HEAD_DIM), F32)
            return jnp.concatenate([ref[...], pad], axis=0).astype(BF16)

        update(new_rows(kn_ref), new_rows(vn_ref), biasn_ref[...])
        o_ref[...] = acc_ref[...] / jnp.sum(l_ref[...], axis=1, keepdims=True)


def _sample_attn(page_table, q_b, bias, bias_new, k_new, v_new, cache_k, cache_v, *, layer, nt):
    nb, n_pages = page_table.shape
    npg = min(PAGES_PER_STEP, n_pages)
    rows = q_b.shape[1]
    page_rows = PAGE_SIZE * N_KV_HEADS
    assert N_KV_HEADS & (N_KV_HEADS - 1) == 0 and 2 * nt == SUBLANES and nb % 2 == 0

    def page_spec(p):
        return pl.BlockSpec((None, None, PAGE_SIZE, N_KV_HEADS, HEAD_DIM),
                            functools.partial(lambda b, c, pt, p: (layer, pt[b, c * npg + p], 0, 0, 0), p=p))

    per_b3 = lambda b, c, pt: (b, 0, 0)
    grid_spec = pltpu.PrefetchScalarGridSpec(
        num_scalar_prefetch=1,
        grid=(nb, n_pages // npg),
        in_specs=[
            pl.BlockSpec((None, rows, HEAD_DIM), per_b3),
            pl.BlockSpec((2 * nt, npg * page_rows), lambda b, c, pt: (b // 2, c)),
            pl.BlockSpec((2 * nt, LANES), lambda b, c, pt: (b // 2, 0)),
            pl.BlockSpec((None, nt * N_KV_HEADS, HEAD_DIM), per_b3),
            pl.BlockSpec((None, nt * N_KV_HEADS, HEAD_DIM), per_b3),
        ] + [page_spec(p) for p in range(npg)] + [page_spec(p) for p in range(npg)],
        out_specs=pl.BlockSpec((None, rows, HEAD_DIM), per_b3),
        scratch_shapes=[
            pltpu.VMEM((npg * page_rows, HEAD_DIM), BF16),
            pltpu.VMEM((npg * page_rows, HEAD_DIM), BF16),
            pltpu.VMEM((rows, LANES), F32),
            pltpu.VMEM((rows, LANES), F32),
            pltpu.VMEM((rows, HEAD_DIM), F32),
        ],
    )
    return pl.pallas_call(
        functools.partial(_sample_attn_kernel, nt=nt, npg=npg),
        grid_spec=grid_spec,
        out_shape=jax.ShapeDtypeStruct((nb, rows, HEAD_DIM), F32),
        compiler_params=_cparams(("parallel", "arbitrary")),
        name="sample_attn",
    )(page_table, q_b, bias, bias_new, k_new, v_new, *([cache_k] * npg), *([cache_v] * npg))


def _proj_out_kernel(x_ref, r_ref, a_ref, w_ref, g_ref, b_ref, o_ref, *, alpha):
    d_rnn = r_ref.shape[1]
    mix = _dot(r_ref[...], w_ref[0:d_rnn, :]) + _dot(a_ref[...], w_ref[d_rnn:, :])
    o_ref[...] = _layer_norm(alpha * x_ref[...] + mix, g_ref[...], b_ref[...])


def _proj_out_ln(x, rnn, attn, w, g, b, *, alpha, tm):
    rows, d = x.shape
    return pl.pallas_call(
        functools.partial(_proj_out_kernel, alpha=alpha),
        grid=(rows // tm,),
        in_specs=[
            pl.BlockSpec((tm, d), lambda i: (i, 0)),
            pl.BlockSpec((tm, rnn.shape[1]), lambda i: (i, 0)),
            pl.BlockSpec((tm, attn.shape[1]), lambda i: (i, 0)),
            pl.BlockSpec(w.shape, lambda i: (0, 0)),
            pl.BlockSpec((1, d), lambda i: (0, 0)),
            pl.BlockSpec((1, d), lambda i: (0, 0)),
        ],
        out_specs=pl.BlockSpec((tm, d), lambda i: (i, 0)),
        out_shape=jax.ShapeDtypeStruct((rows, d), F32),
        compiler_params=_cparams(("parallel",)),
        name="proj_out_ln",
    )(x, rnn, attn, w, g, b)


def _row_tile(rows, target):
    tm = min(rows, target)
    while rows % tm:
        tm //= 2
    return tm


def _pad_to(a, axis, size):
    pad = [(0, 0)] * a.ndim
    pad[axis] = (0, size - a.shape[axis])
    return jnp.pad(a, pad)


def _ffn_weights(w_gu, w_down):
    d_ff = w_down.shape[0]
    return w_gu[:, :d_ff].astype(BF16), w_gu[:, d_ff:].astype(BF16), w_down.astype(BF16), 2 * MXU_DIM


def kernel(x_prompt, x_sample, cache_k, cache_v, cache_k_idx, state_conv, state_rnn, page_table, ln1_g, ln1_b, ffn1_w_gu, ffn1_w_down, w_in, conv_w, conv_b, lru_w_a, lru_b_a, lru_w_i, lru_b_i, lru_lambda, w_out, ln2_g, ln2_b, ffn2_w_gu, ffn2_w_down, ln3_g, ln3_b):
    depth = w_in.shape[0]
    bp, seq, d_model = x_prompt.shape
    nb, nt, _ = x_sample.shape
    d_rnn = conv_w.shape[2]
    d_q = N_Q_HEADS * HEAD_DIM
    d_kv = N_KV_HEADS * HEAD_DIM
    d_qi = IDX_HEADS * IDX_DIM
    alpha = (2.0 * depth) ** 0.25

    xp = x_prompt.reshape(bp * seq, d_model)
    xs = x_sample.transpose(1, 0, 2).reshape(nt * nb, d_model)
    tm_p = _row_tile(bp * seq, 512)
    tm_s = _row_tile(nt * nb, 512)
    vec = lambda a: a.reshape(1, -1)

    outs = [[] for _ in range(10)]
    for l in range(depth):
        wg1, wu1, wd1, tf = _ffn_weights(ffn1_w_gu[l], ffn1_w_down[l])
        wg2, wu2, wd2, _ = _ffn_weights(ffn2_w_gu[l], ffn2_w_down[l])
        d_in = w_in.shape[2]
        w_in_b = _pad_to(w_in[l].astype(BF16), 1, d_in - IDX_DIM - IDX_HEADS + LANES)
        w_out_b = w_out[l].astype(BF16)
        wa_b = lru_w_a[l].astype(BF16)
        wi_b = lru_w_i[l].astype(BF16)
        lru_args = (conv_w[l], vec(conv_b[l]), wa_b, vec(lru_b_a[l]), wi_b, vec(lru_b_i[l]), vec(lru_lambda[l]))
        proj = functools.partial(_proj_in, d_rnn=d_rnn, d_q=d_q, d_kv=d_kv, d_qi=d_qi)

        x1p = _ffn_ln(xp, wg1, wu1, wd1, vec(ln1_g[l]), vec(ln1_b[l]), alpha=alpha, tm=tm_p, tf=tf)
        x1s = _ffn_ln(xs, wg1, wu1, wd1, vec(ln1_g[l]), vec(ln1_b[l]), alpha=alpha, tm=tm_s, tf=tf)
        xr_p, gr_p, q_p, k_p, v_p, kb_p, vb_p, qi_p, kw_p = proj(x1p, w_in_b, tm=_row_tile(bp * seq, 256))
        xr_s, gr_s, q_s, k_s, v_s, _, _, qi_s, kw_s = proj(x1s, w_in_b, tm=_row_tile(nt * nb, 256))

        rnn_p, hl_p = _lru_prompt(xr_p, gr_p, *lru_args, batch=bp, tt=_row_tile(seq, 512))
        conv_tm = state_conv[l].transpose(1, 0, 2).reshape((CONV_W - 1) * nb, d_rnn)
        xpad_s = jnp.concatenate([conv_tm, xr_s], axis=0)
        rnn_s, hl_s = _lru_sample(xpad_s, gr_s, state_rnn[l], *lru_args, nb=nb, nt=nt)

        attn_p = _prompt_attn(q_p, qi_p, kw_p, kb_p, vb_p, batch=bp)

        to_b = lambda a: a.reshape(nt, nb, -1).transpose(1, 0, 2)
        qi_b = to_b(qi_s).reshape(nb, nt * IDX_HEADS, IDX_DIM)
        kw_b = to_b(kw_s)
        w_b = jnp.broadcast_to(kw_b[:, :, IDX_DIM:IDX_DIM + IDX_HEADS].reshape(nb, nt * IDX_HEADS, 1),
                               (nb, nt * IDX_HEADS, LANES))
        ki_new = _pad_to(kw_b[:, :, :IDX_DIM], 1, PAGE_SIZE)
        cache_ki_t = jnp.swapaxes(cache_k_idx, 2, 3)
        past_sc, new_sc = _sample_scores(page_table, qi_b, w_b, ki_new, cache_ki_t, layer=l, nt=nt)
        n_past = past_sc.shape[2]
        bias, bias_new = _sample_select(past_sc.reshape(nb * nt, n_past), new_sc.reshape(nb * nt, PAGE_SIZE), nt=nt)
        q_b = to_b(q_s).reshape(nb, nt, N_KV_HEADS, GQA_GROUP, HEAD_DIM)
        q_b = q_b.transpose(0, 2, 3, 1, 4).reshape(nb, N_Q_HEADS * nt, HEAD_DIM)
        new_rows = lambda a: to_b(a).reshape(nb, nt * N_KV_HEADS, HEAD_DIM)
        o_b = _sample_attn(page_table, q_b, bias, bias_new, new_rows(k_s), new_rows(v_s), cache_k, cache_v,
                           layer=l, nt=nt)
        attn_s = o_b.reshape(nb, N_KV_HEADS, GQA_GROUP, nt, HEAD_DIM).transpose(3, 0, 1, 2, 4)
        attn_s = attn_s.reshape(nt * nb, d_q).astype(BF16)

        x2p = _proj_out_ln(x1p, rnn_p, attn_p, w_out_b, vec(ln2_g[l]), vec(ln2_b[l]), alpha=alpha, tm=tm_p)
        x2s = _proj_out_ln(x1s, rnn_s, attn_s, w_out_b, vec(ln2_g[l]), vec(ln2_b[l]), alpha=alpha, tm=tm_s)
        xp = _ffn_ln(x2p, wg2, wu2, wd2, vec(ln3_g[l]), vec(ln3_b[l]), alpha=alpha, tm=tm_p, tf=tf)
        xs = _ffn_ln(x2s, wg2, wu2, wd2, vec(ln3_g[l]), vec(ln3_b[l]), alpha=alpha, tm=tm_s, tf=tf)

        xr_p3 = xr_p.reshape(bp, seq, d_rnn)
        conv_p = xr_p3[:, -(CONV_W - 1):]
        conv_s = xpad_s.reshape(CONV_W - 1 + nt, nb, d_rnn)[-(CONV_W - 1):].transpose(1, 0, 2)
        layer_out = (
            k_p.reshape(bp, seq, N_KV_HEADS, HEAD_DIM), v_p.reshape(bp, seq, N_KV_HEADS, HEAD_DIM),
            kw_p[:, :IDX_DIM].reshape(bp, seq, IDX_DIM), conv_p, hl_p.reshape(bp, d_rnn),
            to_b(k_s).reshape(nb, nt, N_KV_HEADS, HEAD_DIM), to_b(v_s).reshape(nb, nt, N_KV_HEADS, HEAD_DIM),
            kw_b[:, :, :IDX_DIM], conv_s, hl_s,
        )
        for acc, val in zip(outs, layer_out):
            acc.append(val)

    y_p = xp.reshape(bp, seq, d_model)
    y_s = xs.reshape(nt, nb, d_model).transpose(1, 0, 2)
    return (y_p, y_s) + tuple(jnp.stack(o) for o in outs)
```

```python
import functools

import jax
import jax.numpy as jnp
from jax import lax
from jax.experimental import pallas as pl
from jax.experimental.pallas import tpu as pltpu

F32 = jnp.float32
BF16 = jnp.bfloat16
I32 = jnp.int32

LRU_BLOCKS = 8
CONV_W = 4
LRU_C = 8.0
HEAD_DIM = 128
N_KV_HEADS = 4
GQA_GROUP = 2
N_Q_HEADS = N_KV_HEADS * GQA_GROUP
IDX_HEADS = 8
IDX_DIM = 64
TOPK_MAX = 256
QUERY_ROWS = (512, 256, 128)
COUNT_ROWS = 128
PAGE_SIZE = 128
LN_EPS = 1e-5
ATTN_SCALE = HEAD_DIM ** -0.5
SOFTMAX_LOG2_SCALE = ATTN_SCALE * 1.4426950408889634
IDX_SCALE = IDX_DIM ** -0.5
IDX_W_SCALE = IDX_HEADS ** -0.5

LANES = 128
SUBLANES = 8
MXU_DIM = 256
VMEM_LIMIT = 56 * 1024 * 1024

INT_MIN = -2 ** 31
FLT_LOWEST = -3.4028234663852886e38
NEG_BIG = -1e30
KEY_CHUNK = 512
PAGES_PER_STEP = 16
IDX_PAGES_PER_STEP = 16


def _cparams(semantics):
    return pltpu.CompilerParams(dimension_semantics=semantics, vmem_limit_bytes=VMEM_LIMIT)


def _layer_norm(y, g, b):
    mu = jnp.mean(y, axis=-1, keepdims=True)
    d = y - mu
    var = jnp.mean(d * d, axis=-1, keepdims=True)
    return d * lax.rsqrt(var + LN_EPS) * g + b


def _dot(a, b):
    return jnp.dot(a, b, preferred_element_type=F32)


def _dot_nt(a, b):
    return lax.dot_general(a, b, (((1,), (1,)), ((), ())), preferred_element_type=F32)


def _log2(n):
    assert n > 0 and n & (n - 1) == 0, n
    return n.bit_length() - 1


def _ffn_kernel(x_ref, wg_ref, wu_ref, wd_ref, g_ref, b_ref, o_ref, xb_ref, *, alpha, tail):
    j = pl.program_id(1)

    @pl.when(j == 0)
    def _():
        xb_ref[...] = x_ref[...].astype(BF16)
        o_ref[...] = jnp.zeros_like(o_ref)

    last = pl.num_programs(1) - 1

    def chunk(width):
        xb = xb_ref[...]
        gate = _dot(xb, wg_ref[:, 0:width].astype(BF16))
        up = _dot(xb, wu_ref[:, 0:width])
        act = (gate * jax.nn.sigmoid(gate) * up).astype(BF16)
        o_ref[...] += _dot(act, wd_ref[0:width, :].astype(BF16))

    if tail == wg_ref.shape[1]:
        chunk(tail)
    else:
        pl.when(j < last)(lambda: chunk(wg_ref.shape[1]))
        pl.when(j == last)(lambda: chunk(tail))

    @pl.when(j == last)
    def _():
        y = alpha * x_ref[...] + 0.5 * o_ref[...]
        o_ref[...] = _layer_norm(y, g_ref[...], b_ref[...])


def _ffn_ln(x, w_gu, wu, w_down, g, b, *, alpha, tm, tf):
    rows, d = x.shape
    d_ff = w_down.shape[0]
    nj = pl.cdiv(d_ff, tf)
    return pl.pallas_call(
        functools.partial(_ffn_kernel, alpha=alpha, tail=d_ff - (nj - 1) * tf),
        grid=(rows // tm, nj),
        in_specs=[
            pl.BlockSpec((tm, d), lambda i, j: (i, 0)),
            pl.BlockSpec((d, tf), lambda i, j: (0, j)),
            pl.BlockSpec((d, tf), lambda i, j: (0, j)),
            pl.BlockSpec((tf, d), lambda i, j: (j, 0)),
            pl.BlockSpec((1, d), lambda i, j: (0, 0)),
            pl.BlockSpec((1, d), lambda i, j: (0, 0)),
        ],
        out_specs=pl.BlockSpec((tm, d), lambda i, j: (i, 0)),
        out_shape=jax.ShapeDtypeStruct((rows, d), F32),
        scratch_shapes=[pltpu.VMEM((tm, d), BF16)],
        compiler_params=_cparams(("parallel", "arbitrary")),
        name="ffn_ln",
    )(x, w_gu, wu, w_down, g, b)


def _proj_in_kernel(x_ref, w_ref, xr_ref, gr_ref, q_ref, k_ref, v_ref, kb_ref, vb_ref, qi_ref, kw_ref,
                    *, d_rnn, d_q, d_kv, d_qi):
    xb = x_ref[...].astype(BF16)
    off = [0]

    def seg(width):
        lo = off[0]
        off[0] = lo + width
        return _dot(xb, w_ref[:, lo:lo + width])

    xr_ref[...] = seg(d_rnn)
    gr_ref[...] = seg(d_rnn)
    q_ref[...] = seg(d_q).astype(BF16)
    k = seg(d_kv)
    k_ref[...] = k
    kb_ref[...] = k.astype(BF16)
    v = seg(d_kv)
    v_ref[...] = v
    vb_ref[...] = v.astype(BF16)
    qi_ref[...] = seg(d_qi).astype(BF16)
    kw_ref[...] = seg(LANES)


def _proj_in(x, w, *, tm, d_rnn, d_q, d_kv, d_qi):
    rows, d = x.shape
    widths = (d_rnn, d_rnn, d_q, d_kv, d_kv, d_kv, d_kv, d_qi, LANES)
    dtypes = (F32, F32, BF16, F32, F32, BF16, BF16, BF16, F32)
    return pl.pallas_call(
        functools.partial(_proj_in_kernel, d_rnn=d_rnn, d_q=d_q, d_kv=d_kv, d_qi=d_qi),
        grid=(rows // tm,),
        in_specs=[
            pl.BlockSpec((tm, d), lambda i: (i, 0)),
            pl.BlockSpec(w.shape, lambda i: (0, 0), pipeline_mode=pl.Buffered(1)),
        ],
        out_specs=[pl.BlockSpec((tm, wd), lambda i: (i, 0)) for wd in widths],
        out_shape=[jax.ShapeDtypeStruct((rows, wd), dt) for wd, dt in zip(widths, dtypes)],
        compiler_params=_cparams(("parallel",)),
        name="proj_in",
    )(x, w)


def _softplus(z):
    return jnp.maximum(z, 0.0) + jnp.log1p(jnp.exp(-jnp.abs(z)))


def _lru_gates(xc, wa_ref, ba, wi_ref, bi, lam):
    bw = xc.shape[1] // LRU_BLOCKS
    sp = _softplus(-lam)
    a_parts, u_parts = [], []
    for n in range(LRU_BLOCKS):
        sl = slice(n * bw, (n + 1) * bw)
        xn = xc[:, sl]
        xb = xn.astype(BF16)
        r = jax.nn.sigmoid(_dot(xb, wa_ref[n]) + ba[:, sl])
        i = jax.nn.sigmoid(_dot(xb, wi_ref[n]) + bi[:, sl])
        log_a = -LRU_C * r * sp[:, sl]
        a = jnp.exp(log_a)
        a_parts.append(a)
        u_parts.append(jnp.sqrt(-jnp.tanh(log_a) * (a * a + 1.0)) * i * xn)
    return jnp.concatenate(a_parts, axis=1), jnp.concatenate(u_parts, axis=1)


def _lru_prompt_kernel(xr_ref, gr_ref, cw_ref, cb_ref, wa_ref, ba_ref, wi_ref, bi_ref, lam_ref,
                       o_ref, hl_ref, xp_ref, a_ref, u_ref, hs_ref, hc_ref, *, tt):
    j = pl.program_id(1)
    d = xr_ref.shape[1]

    @pl.when(j == 0)
    def _():
        xp_ref[0:SUBLANES, :] = jnp.zeros((SUBLANES, d), F32)
        hc_ref[...] = jnp.zeros_like(hc_ref)

    @pl.when(j > 0)
    def _():
        xp_ref[0:SUBLANES, :] = xp_ref[tt:tt + SUBLANES, :]

    xp_ref[SUBLANES:SUBLANES + tt, :] = xr_ref[...]
    cw = cw_ref[...]
    xc = cb_ref[...]
    for jj in range(CONV_W):
        lo = SUBLANES - (CONV_W - 1) + jj
        xc = xc + cw[jj:jj + 1, :] * xp_ref[lo:lo + tt, :]

    a, u = _lru_gates(xc, wa_ref, ba_ref[...], wi_ref, bi_ref[...], lam_ref[...])
    a_ref[...] = a
    u_ref[...] = u

    row = lax.broadcasted_iota(I32, (SUBLANES, d), 0)

    def group(g, h):
        r0 = pl.multiple_of(g * SUBLANES, SUBLANES)
        a8 = a_ref[pl.ds(r0, SUBLANES), :]
        u8 = u_ref[pl.ds(r0, SUBLANES), :]
        out = jnp.zeros((SUBLANES, d), F32)
        for jj in range(SUBLANES):
            aj = jnp.broadcast_to(a8[jj:jj + 1, :], (SUBLANES, d))
            uj = jnp.broadcast_to(u8[jj:jj + 1, :], (SUBLANES, d))
            h = aj * h + uj
            out = jnp.where(row == jj, h, out)
        hs_ref[pl.ds(r0, SUBLANES), :] = out
        return h

    h = lax.fori_loop(0, tt // SUBLANES, group, hc_ref[...])
    hc_ref[...] = h
    hl_ref[...] = h[0:1, :]
    o_ref[...] = (hs_ref[...] * jax.nn.gelu(gr_ref[...])).astype(BF16)


def _lru_prompt(xr, gr, cw, cb, wa, ba, wi, bi, lam, *, batch, tt):
    rows, d = xr.shape
    seq = rows // batch
    nt = seq // tt
    row_spec = pl.BlockSpec((tt, d), lambda b, j: (b * nt + j, 0))
    vec_spec = pl.BlockSpec((1, d), lambda b, j: (0, 0))
    w_spec = pl.BlockSpec(wa.shape, lambda b, j: (0, 0, 0))
    return pl.pallas_call(
        functools.partial(_lru_prompt_kernel, tt=tt),
        grid=(batch, nt),
        in_specs=[row_spec, row_spec, pl.BlockSpec((CONV_W, d), lambda b, j: (0, 0)), vec_spec,
                  w_spec, vec_spec, w_spec, vec_spec, vec_spec],
        out_specs=[row_spec, pl.BlockSpec((None, 1, d), lambda b, j: (b, 0, 0))],
        out_shape=[jax.ShapeDtypeStruct((rows, d), BF16), jax.ShapeDtypeStruct((batch, 1, d), F32)],
        scratch_shapes=[pltpu.VMEM((tt + SUBLANES, d), F32), pltpu.VMEM((tt, d), F32),
                        pltpu.VMEM((tt, d), F32), pltpu.VMEM((tt, d), F32), pltpu.VMEM((SUBLANES, d), F32)],
        compiler_params=_cparams(("parallel", "arbitrary")),
        name="lru_prompt",
    )(xr, gr, cw, cb, wa, ba, wi, bi, lam)


def _lru_sample_kernel(xp_ref, gr_ref, h0_ref, cw_ref, cb_ref, wa_ref, ba_ref, wi_ref, bi_ref, lam_ref,
                       o_ref, hl_ref, *, nb, nt):
    cw = cw_ref[...]
    xc = cb_ref[...] + cw[0:1, :] * xp_ref[0:nt * nb, :]
    for jj in range(1, CONV_W):
        xc = xc + cw[jj:jj + 1, :] * xp_ref[jj * nb:(jj + nt) * nb, :]
    a, u = _lru_gates(xc, wa_ref, ba_ref[...], wi_ref, bi_ref[...], lam_ref[...])
    gate = jax.nn.gelu(gr_ref[...])
    h = h0_ref[...]
    for t in range(nt):
        sl = slice(t * nb, (t + 1) * nb)
        h = a[sl, :] * h + u[sl, :]
        o_ref[sl, :] = (h * gate[sl, :]).astype(BF16)
    hl_ref[...] = h


def _lru_sample(xp, gr, h0, cw, cb, wa, ba, wi, bi, lam, *, nb, nt):
    d = gr.shape[1]
    return pl.pallas_call(
        functools.partial(_lru_sample_kernel, nb=nb, nt=nt),
        out_shape=[jax.ShapeDtypeStruct((nt * nb, d), BF16), jax.ShapeDtypeStruct((nb, d), F32)],
        compiler_params=pltpu.CompilerParams(vmem_limit_bytes=VMEM_LIMIT),
        name="lru_sample",
    )(xp, gr, h0, cw, cb, wa, ba, wi, bi, lam)


def _rank_to_float(u):
    key = u ^ jnp.int32(INT_MIN)
    return lax.bitcast_convert_type(key ^ ((key >> 31) & jnp.int32(0x7FFFFFFF)), F32)


def _count(sc_ref, nc, rows, ck, thr, strict):
    accs = []
    for r0 in range(0, rows, COUNT_ROWS):
        nr = min(COUNT_ROWS, rows - r0)
        thr_b = jnp.broadcast_to(thr[r0:r0 + nr], (nr, LANES))

        def body(c, acc, r0=r0, nr=nr, thr_b=thr_b):
            for t in range(ck // LANES):
                tile = sc_ref[c, r0:r0 + nr, t * LANES:(t + 1) * LANES]
                hit = (tile > thr_b) if strict else (tile >= thr_b)
                acc = acc + jnp.where(hit, 1.0, 0.0)
            return acc

        accs.append(lax.fori_loop(0, nc, body, jnp.zeros((nr, LANES), F32)))
    acc = accs[0] if len(accs) == 1 else jnp.concatenate(accs, axis=0)
    return jnp.sum(acc, axis=1, keepdims=True)


def _select_topk(sc_ref, tau_ref, nc, rows, ck, k_sel):
    kf = float(k_sel)

    def bit_body(it, carry):
        u, n_ge = carry
        cand = u | jnp.left_shift(jnp.int32(1), 31 - it)
        cnt = _count(sc_ref, nc, rows, ck, _rank_to_float(cand), strict=False)
        take = cnt >= kf
        return jnp.where(take, cand, u), jnp.where(take, cnt, n_ge)

    u, n_ge = lax.fori_loop(0, 32, bit_body, (jnp.zeros((rows, 1), I32), jnp.full((rows, 1), kf, F32)))
    tau = _rank_to_float(u)
    tau_ref[...] = jnp.broadcast_to(tau, (rows, LANES))

    @pl.when(jnp.max(n_ge) > kf)
    def _():
        before = (lax.broadcasted_iota(I32, (ck, ck), 0) < lax.broadcasted_iota(I32, (ck, ck), 1))
        before = jnp.where(before, 1.0, 0.0).astype(BF16)
        tau_w = jnp.broadcast_to(tau, (rows, ck))
        need = kf - _count(sc_ref, nc, rows, ck, tau, strict=True)
        need_w = jnp.broadcast_to(need, (rows, ck))

        def body(c, seen):
            sc = sc_ref[c]
            eq = sc == tau_w
            eqf = jnp.where(eq, 1.0, 0.0)
            rank = _dot(eqf.astype(BF16), before) + seen
            sc_ref[c] = jnp.where(eq, jnp.where(rank >= need_w, -jnp.inf, sc), sc)
            return seen + jnp.sum(eqf, axis=1, keepdims=True)

        lax.fori_loop(0, nc, body, jnp.zeros((rows, 1), F32))


def _prompt_attn_kernel(q_ref, qi_ref, kwq_ref, kwk_ref, kb_ref, vb_ref, o_ref,
                        ki_ref, sc_ref, tau_ref, qs_ref, m_ref, l_ref, acc_ref, *, k_sel, qb):
    i = pl.program_id(1)
    ck = sc_ref.shape[2]
    nc = (i * qb + qb + ck - 1) // ck

    @pl.when(i == 0)
    def _():
        ki_ref[...] = kwk_ref[:, 0:IDX_DIM].astype(BF16)

    w_idx = kwq_ref[:, IDX_DIM:IDX_DIM + IDX_HEADS] * IDX_W_SCALE * IDX_SCALE
    t_pos = i * qb + lax.broadcasted_iota(I32, (qb, ck), 0)
    col = lax.broadcasted_iota(I32, (qb, ck), 1)

    def score_body(c, carry):
        c0 = pl.multiple_of(c * ck, ck)
        kc = ki_ref[pl.ds(c0, ck), :]
        score = jnp.zeros((qb, ck), F32)
        for h in range(IDX_HEADS):
            s = _dot_nt(qi_ref[:, h * IDX_DIM:(h + 1) * IDX_DIM], kc)
            score = score + jnp.maximum(s, 0.0) * w_idx[:, h:h + 1]
        sc_ref[c] = jnp.where(col + c0 <= t_pos, score, -jnp.inf)
        return carry

    lax.fori_loop(0, nc, score_body, 0)

    need_select = (i + 1) * qb > k_sel

    @pl.when(jnp.logical_not(need_select))
    def _():
        tau_ref[...] = jnp.full(tau_ref.shape, FLT_LOWEST, F32)

    @pl.when(need_select)
    def _():
        _select_topk(sc_ref, tau_ref, nc, qb, ck, k_sel)
        row_pos = i * qb + lax.broadcasted_iota(I32, (qb, LANES), 0)
        tau_ref[...] = jnp.where(row_pos < k_sel, FLT_LOWEST, tau_ref[...])

    for n in range(N_KV_HEADS):
        for g in range(GQA_GROUP):
            hq = n * GQA_GROUP + g
            qs_ref[n, g * qb:(g + 1) * qb, :] = q_ref[:, hq * HEAD_DIM:(hq + 1) * HEAD_DIM]
    m_ref[...] = jnp.full(m_ref.shape, NEG_BIG, F32)
    l_ref[...] = jnp.zeros_like(l_ref)
    acc_ref[...] = jnp.zeros_like(acc_ref)
    tau_w = jnp.broadcast_to(tau_ref[:, 0:1], (qb, ck))

    def attn_body(c, carry):
        c0 = pl.multiple_of(c * ck, ck)
        bias = jnp.where(sc_ref[c] >= tau_w, 0.0, NEG_BIG)
        bias = jnp.concatenate([bias] * GQA_GROUP, axis=0)
        for n in range(N_KV_HEADS):
            kn = kb_ref[pl.ds(c0, ck), n * HEAD_DIM:(n + 1) * HEAD_DIM]
            vn = vb_ref[pl.ds(c0, ck), n * HEAD_DIM:(n + 1) * HEAD_DIM]
            s = _dot_nt(qs_ref[n], kn) + bias
            tiles = [s[:, t * LANES:(t + 1) * LANES] for t in range(ck // LANES)]
            m_prev = m_ref[n]
            m_new = jnp.maximum(m_prev, jnp.max(functools.reduce(jnp.maximum, tiles), axis=1, keepdims=True))
            alpha = jnp.exp2((m_prev - m_new) * SOFTMAX_LOG2_SCALE)
            p_tiles = [jnp.exp2((t - m_new) * SOFTMAX_LOG2_SCALE) for t in tiles]
            l_ref[n] = alpha * l_ref[n] + functools.reduce(jnp.add, p_tiles)
            p = jnp.concatenate(p_tiles, axis=1).astype(BF16)
            acc_ref[n] = alpha * acc_ref[n] + _dot(p, vn)
            m_ref[n] = m_new
        return carry

    lax.fori_loop(0, nc, attn_body, 0)

    for n in range(N_KV_HEADS):
        o = acc_ref[n] / jnp.sum(l_ref[n], axis=1, keepdims=True)
        for g in range(GQA_GROUP):
            hq = n * GQA_GROUP + g
            o_ref[:, hq * HEAD_DIM:(hq + 1) * HEAD_DIM] = o[g * qb:(g + 1) * qb, :].astype(BF16)


def _prompt_attn(q, qi, kw, kb, vb, *, batch):
    rows = q.shape[0]
    seq = rows // batch
    ck = min(KEY_CHUNK, seq)
    k_sel = min(TOPK_MAX, seq // 4)
    qb = next(r for r in QUERY_ROWS if seq % r == 0)
    nblk = seq // qb
    gq = GQA_GROUP * qb
    blk = lambda b, i: (b * nblk + i, 0)
    per_b = lambda b, i: (b, 0)
    return pl.pallas_call(
        functools.partial(_prompt_attn_kernel, k_sel=k_sel, qb=qb),
        grid=(batch, nblk),
        in_specs=[
            pl.BlockSpec((qb, q.shape[1]), blk),
            pl.BlockSpec((qb, qi.shape[1]), blk),
            pl.BlockSpec((qb, LANES), blk),
            pl.BlockSpec((seq, LANES), per_b),
            pl.BlockSpec((seq, kb.shape[1]), per_b),
            pl.BlockSpec((seq, vb.shape[1]), per_b),
        ],
        out_specs=pl.BlockSpec((qb, q.shape[1]), blk),
        out_shape=jax.ShapeDtypeStruct(q.shape, BF16),
        scratch_shapes=[
            pltpu.VMEM((seq, IDX_DIM), BF16),
            pltpu.VMEM((seq // ck, qb, ck), F32),
            pltpu.VMEM((qb, LANES), F32),
            pltpu.VMEM((N_KV_HEADS, gq, HEAD_DIM), BF16),
            pltpu.VMEM((N_KV_HEADS, gq, LANES), F32),
            pltpu.VMEM((N_KV_HEADS, gq, LANES), F32),
            pltpu.VMEM((N_KV_HEADS, gq, HEAD_DIM), F32),
        ],
        compiler_params=_cparams(("parallel", "arbitrary")),
        name="prompt_attn",
    )(q, qi, kw, kw, kb, vb)


def _sample_scores_kernel(pt_ref, qi_ref, w_ref, kin_ref, *rest, nt, npg):
    pages = rest[:npg]
    past_ref, new_ref, kc_ref = rest[npg:]
    c = pl.program_id(1)
    w = w_ref[...] * IDX_W_SCALE * IDX_SCALE
    qi = qi_ref[...]

    def scores(s):
        width = s.shape[1]
        s = jnp.maximum(s, 0.0) * jnp.concatenate([w] * (width // LANES), axis=1)
        return jnp.sum(s.reshape(nt, IDX_HEADS, width), axis=1)

    for p in range(npg):
        kc_ref[:, p * PAGE_SIZE:(p + 1) * PAGE_SIZE] = pages[p][...].astype(BF16)
    past_ref[...] = scores(_dot(qi, kc_ref[...]))

    @pl.when(c == 0)
    def _():
        new_ref[...] = scores(_dot_nt(qi, kin_ref[...].astype(BF16)))


def _sample_scores(page_table, qi_b, w_b, ki_new, cache_ki, *, layer, nt):
    nb, n_pages = page_table.shape
    npg = min(IDX_PAGES_PER_STEP, n_pages)
    rows = nt * IDX_HEADS
    page_specs = [
        pl.BlockSpec((None, None, IDX_DIM, PAGE_SIZE),
                     functools.partial(lambda b, c, pt, p: (layer, pt[b, c * npg + p], 0, 0), p=p))
        for p in range(npg)
    ]
    grid_spec = pltpu.PrefetchScalarGridSpec(
        num_scalar_prefetch=1,
        grid=(nb, n_pages // npg),
        in_specs=[
            pl.BlockSpec((None, rows, IDX_DIM), lambda b, c, pt: (b, 0, 0)),
            pl.BlockSpec((None, rows, LANES), lambda b, c, pt: (b, 0, 0)),
            pl.BlockSpec((None, PAGE_SIZE, IDX_DIM), lambda b, c, pt: (b, 0, 0)),
        ] + page_specs,
        out_specs=[
            pl.BlockSpec((None, nt, npg * PAGE_SIZE), lambda b, c, pt: (b, 0, c)),
            pl.BlockSpec((None, nt, PAGE_SIZE), lambda b, c, pt: (b, 0, 0)),
        ],
        scratch_shapes=[pltpu.VMEM((IDX_DIM, npg * PAGE_SIZE), BF16)],
    )
    return pl.pallas_call(
        functools.partial(_sample_scores_kernel, nt=nt, npg=npg),
        grid_spec=grid_spec,
        out_shape=[jax.ShapeDtypeStruct((nb, nt, n_pages * PAGE_SIZE), F32),
                   jax.ShapeDtypeStruct((nb, nt, PAGE_SIZE), F32)],
        compiler_params=_cparams(("parallel", "arbitrary")),
        name="sample_scores",
    )(page_table, qi_b, w_b, ki_new, *([cache_ki] * npg))


def _sample_select_kernel(past_ref, new_ref, bias_ref, biasn_ref, sc_ref, tau_ref, *, nt, k_sel, ck):
    rows, n_past = past_ref.shape
    npc = n_past // ck
    for c in range(npc):
        sc_ref[c] = past_ref[:, c * ck:(c + 1) * ck]
    t_row = lax.broadcasted_iota(I32, (rows, ck), 0) & (nt - 1)
    col = lax.broadcasted_iota(I32, (rows, ck), 1)
    new_sc = jnp.concatenate([new_ref[...]] * (ck // PAGE_SIZE), axis=1)
    sc_ref[npc] = jnp.where(col <= t_row, new_sc, -jnp.inf)
    _select_topk(sc_ref, tau_ref, npc + 1, rows, ck, k_sel)
    tau_b = tau_ref[...]
    d = (lax.broadcasted_iota(I32, (LANES, LANES * N_KV_HEADS), 1)
         - N_KV_HEADS * lax.broadcasted_iota(I32, (LANES, LANES * N_KV_HEADS), 0))
    expand = jnp.where(d >= 0, jnp.where(d < N_KV_HEADS, 1.0, 0.0), 0.0).astype(BF16)
    def bias_of(c, t):
        sel = jnp.where(sc_ref[c][:, t * LANES:(t + 1) * LANES] >= tau_b, 1.0, 0.0).astype(BF16)
        return (_dot(sel, expand) - 1.0) * (-NEG_BIG)

    for c in range(npc):
        for t in range(ck // LANES):
            lo = (c * ck + t * LANES) * N_KV_HEADS
            bias_ref[:, lo:lo + LANES * N_KV_HEADS] = bias_of(c, t)
    biasn_ref[...] = bias_of(npc, 0)[:, 0:LANES]


def _sample_select(past, new, *, nt):
    rows, n_past = past.shape
    assert nt & (nt - 1) == 0 and nt * N_KV_HEADS <= LANES
    ck = min(KEY_CHUNK, n_past)
    k_sel = min(TOPK_MAX, (n_past + nt) // 4)
    return pl.pallas_call(
        functools.partial(_sample_select_kernel, nt=nt, k_sel=k_sel, ck=ck),
        out_shape=[jax.ShapeDtypeStruct((rows, n_past * N_KV_HEADS), F32), jax.ShapeDtypeStruct((rows, LANES), F32)],
        scratch_shapes=[pltpu.VMEM((n_past // ck + 1, rows, ck), F32), pltpu.VMEM((rows, LANES), F32)],
        compiler_params=pltpu.CompilerParams(vmem_limit_bytes=VMEM_LIMIT),
        name="sample_select",
    )(past, new)


def _sample_attn_kernel(pt_ref, q_ref, bias_ref, biasn_ref, kn_ref, vn_ref, *rest, nt, npg):
    k_pages = rest[:npg]
    v_pages = rest[npg:2 * npg]
    o_ref, kc_ref, vc_ref, m_ref, l_ref, acc_ref = rest[2 * npg:]
    c = pl.program_id(1)
    rows = q_ref.shape[0]
    page_rows = PAGE_SIZE * N_KV_HEADS

    @pl.when(c == 0)
    def _():
        m_ref[...] = jnp.full(m_ref.shape, NEG_BIG, F32)
        l_ref[...] = jnp.zeros_like(l_ref)
        acc_ref[...] = jnp.zeros_like(acc_ref)

    row_head = lax.shift_right_logical(lax.broadcasted_iota(I32, (rows, page_rows), 0), _log2(rows // N_KV_HEADS))
    col_head = lax.broadcasted_iota(I32, (rows, page_rows), 1) & (N_KV_HEADS - 1)
    head_bias = jnp.where(row_head == col_head, 0.0, NEG_BIG)
    first_of_pair = pl.program_id(0) % 2 == 0

    def update(k2, v2, bias_blk):
        s = _dot_nt(q_ref[...], k2)
        bias = jnp.where(first_of_pair, bias_blk[0:nt], bias_blk[nt:2 * nt])
        bias = jnp.concatenate([bias] * (rows // nt), axis=0)
        lanes = lambda a, t: a[:, t * LANES:(t + 1) * LANES]
        per_page = page_rows // LANES
        tiles = [lanes(s, t) + (lanes(bias, t) + lanes(head_bias, t % per_page)) for t in range(k2.shape[0] // LANES)]
        m_prev = m_ref[...]
        m_new = jnp.maximum(m_prev, jnp.max(functools.reduce(jnp.maximum, tiles), axis=1, keepdims=True))
        alpha = jnp.exp2((m_prev - m_new) * SOFTMAX_LOG2_SCALE)
        p_tiles = [jnp.exp2((t - m_new) * SOFTMAX_LOG2_SCALE) for t in tiles]
        l_ref[...] = alpha * l_ref[...] + functools.reduce(jnp.add, p_tiles)
        acc_ref[...] = alpha * acc_ref[...] + _dot(jnp.concatenate(p_tiles, axis=1).astype(BF16), v2)
        m_ref[...] = m_new

    for p in range(npg):
        kc_ref[p * page_rows:(p + 1) * page_rows, :] = k_pages[p][...].reshape(page_rows, HEAD_DIM).astype(BF16)
        vc_ref[p * page_rows:(p + 1) * page_rows, :] = v_pages[p][...].reshape(page_rows, HEAD_DIM).astype(BF16)
    update(kc_ref[...], vc_ref[...], bias_ref[...])

    @pl.when(c == pl.num_programs(1) - 1)
    def _():
        def new_rows(ref):
            pad = jnp.zeros((LANES - ref.shape[0], HEAD_DIM), F32)
            return jnp.concatenate([ref[...], pad], axis=0).astype(BF16)

        update(new_rows(kn_ref), new_rows(vn_ref), biasn_ref[...])
        o_ref[...] = acc_ref[...] / jnp.sum(l_ref[...], axis=1, keepdims=True)


def _sample_attn(page_table, q_b, bias, bias_new, k_new, v_new, cache_k, cache_v, *, layer, nt):
    nb, n_pages = page_table.shape
    npg = min(PAGES_PER_STEP, n_pages)
    rows = q_b.shape[1]
    page_rows = PAGE_SIZE * N_KV_HEADS
    assert N_KV_HEADS & (N_KV_HEADS - 1) == 0 and 2 * nt == SUBLANES and nb % 2 == 0

    def page_spec(p):
        return pl.BlockSpec((None, None, PAGE_SIZE, N_KV_HEADS, HEAD_DIM),
                            functools.partial(lambda b, c, pt, p: (layer, pt[b, c * npg + p], 0, 0, 0), p=p))

    per_b3 = lambda b, c, pt: (b, 0, 0)
    grid_spec = pltpu.PrefetchScalarGridSpec(
        num_scalar_prefetch=1,
        grid=(nb, n_pages // npg),
        in_specs=[
            pl.BlockSpec((None, rows, HEAD_DIM), per_b3),
            pl.BlockSpec((2 * nt, npg * page_rows), lambda b, c, pt: (b // 2, c)),
            pl.BlockSpec((2 * nt, LANES), lambda b, c, pt: (b // 2, 0)),
            pl.BlockSpec((None, nt * N_KV_HEADS, HEAD_DIM), per_b3),
            pl.BlockSpec((None, nt * N_KV_HEADS, HEAD_DIM), per_b3),
        ] + [page_spec(p) for p in range(npg)] + [page_spec(p) for p in range(npg)],
        out_specs=pl.BlockSpec((None, rows, HEAD_DIM), per_b3),
        scratch_shapes=[
            pltpu.VMEM((npg * page_rows, HEAD_DIM), BF16),
            pltpu.VMEM((npg * page_rows, HEAD_DIM), BF16),
            pltpu.VMEM((rows, LANES), F32),
            pltpu.VMEM((rows, LANES), F32),
            pltpu.VMEM((rows, HEAD_DIM), F32),
        ],
    )
    return pl.pallas_call(
        functools.partial(_sample_attn_kernel, nt=nt, npg=npg),
        grid_spec=grid_spec,
        out_shape=jax.ShapeDtypeStruct((nb, rows, HEAD_DIM), F32),
        compiler_params=_cparams(("parallel", "arbitrary")),
        name="sample_attn",
    )(page_table, q_b, bias, bias_new, k_new, v_new, *([cache_k] * npg), *([cache_v] * npg))


def _proj_out_kernel(x_ref, r_ref, a_ref, w_ref, g_ref, b_ref, o_ref, *, alpha):
    d_rnn = r_ref.shape[1]
    mix = _dot(r_ref[...], w_ref[0:d_rnn, :]) + _dot(a_ref[...], w_ref[d_rnn:, :])
    o_ref[...] = _layer_norm(alpha * x_ref[...] + mix, g_ref[...], b_ref[...])


def _proj_out_ln(x, rnn, attn, w, g, b, *, alpha, tm):
    rows, d = x.shape
    return pl.pallas_call(
        functools.partial(_proj_out_kernel, alpha=alpha),
        grid=(rows // tm,),
        in_specs=[
            pl.BlockSpec((tm, d), lambda i: (i, 0)),
            pl.BlockSpec((tm, rnn.shape[1]), lambda i: (i, 0)),
            pl.BlockSpec((tm, attn.shape[1]), lambda i: (i, 0)),
            pl.BlockSpec(w.shape, lambda i: (0, 0)),
            pl.BlockSpec((1, d), lambda i: (0, 0)),
            pl.BlockSpec((1, d), lambda i: (0, 0)),
        ],
        out_specs=pl.BlockSpec((tm, d), lambda i: (i, 0)),
        out_shape=jax.ShapeDtypeStruct((rows, d), F32),
        compiler_params=_cparams(("parallel",)),
        name="proj_out_ln",
    )(x, rnn, attn, w, g, b)


def _row_tile(rows, target):
    tm = min(rows, target)
    while rows % tm:
        tm //= 2
    return tm


def _pad_to(a, axis, size):
    pad = [(0, 0)] * a.ndim
    pad[axis] = (0, size - a.shape[axis])
    return jnp.pad(a, pad)


def _ffn_weights(w_gu, w_down):
    d_ff = w_down.shape[0]
    return w_gu, w_gu[:, d_ff:].astype(BF16), w_down, MXU_DIM


def kernel(x_prompt, x_sample, cache_k, cache_v, cache_k_idx, state_conv, state_rnn, page_table, ln1_g, ln1_b, ffn1_w_gu, ffn1_w_down, w_in, conv_w, conv_b, lru_w_a, lru_b_a, lru_w_i, lru_b_i, lru_lambda, w_out, ln2_g, ln2_b, ffn2_w_gu, ffn2_w_down, ln3_g, ln3_b):
    depth = w_in.shape[0]
    bp, seq, d_model = x_prompt.shape
    nb, nt, _ = x_sample.shape
    d_rnn = conv_w.shape[2]
    d_q = N_Q_HEADS * HEAD_DIM
    d_kv = N_KV_HEADS * HEAD_DIM
    d_qi = IDX_HEADS * IDX_DIM
    alpha = (2.0 * depth) ** 0.25

    xp = x_prompt.reshape(bp * seq, d_model)
    xs = x_sample.transpose(1, 0, 2).reshape(nt * nb, d_model)
    tm_p = _row_tile(bp * seq, 1024)
    tm_r = _row_tile(bp * seq, 512)
    tm_s = _row_tile(nt * nb, 512)
    vec = lambda a: a.reshape(1, -1)

    outs = [[] for _ in range(10)]
    for l in range(depth):
        wg1, wu1, wd1, tf = _ffn_weights(ffn1_w_gu[l], ffn1_w_down[l])
        wg2, wu2, wd2, _ = _ffn_weights(ffn2_w_gu[l], ffn2_w_down[l])
        d_in = w_in.shape[2]
        w_in_b = _pad_to(w_in[l].astype(BF16), 1, d_in - IDX_DIM - IDX_HEADS + LANES)
        w_out_b = w_out[l].astype(BF16)
        wa_b = lru_w_a[l].astype(BF16)
        wi_b = lru_w_i[l].astype(BF16)
        lru_args = (conv_w[l], vec(conv_b[l]), wa_b, vec(lru_b_a[l]), wi_b, vec(lru_b_i[l]), vec(lru_lambda[l]))
        proj = functools.partial(_proj_in, d_rnn=d_rnn, d_q=d_q, d_kv=d_kv, d_qi=d_qi)

        x1p = _ffn_ln(xp, wg1, wu1, wd1, vec(ln1_g[l]), vec(ln1_b[l]), alpha=alpha, tm=tm_p, tf=tf)
        x1s = _ffn_ln(xs, wg1, wu1, wd1, vec(ln1_g[l]), vec(ln1_b[l]), alpha=alpha, tm=tm_s, tf=tf)
        xr_p, gr_p, q_p, k_p, v_p, kb_p, vb_p, qi_p, kw_p = proj(x1p, w_in_b, tm=_row_tile(bp * seq, 256))
        xr_s, gr_s, q_s, k_s, v_s, _, _, qi_s, kw_s = proj(x1s, w_in_b, tm=_row_tile(nt * nb, 256))

        rnn_p, hl_p = _lru_prompt(xr_p, gr_p, *lru_args, batch=bp, tt=_row_tile(seq, 512))
        conv_tm = state_conv[l].transpose(1, 0, 2).reshape((CONV_W - 1) * nb, d_rnn)
        xpad_s = jnp.concatenate([conv_tm, xr_s], axis=0)
        rnn_s, hl_s = _lru_sample(xpad_s, gr_s, state_rnn[l], *lru_args, nb=nb, nt=nt)

        attn_p = _prompt_attn(q_p, qi_p, kw_p, kb_p, vb_p, batch=bp)

        to_b = lambda a: a.reshape(nt, nb, -1).transpose(1, 0, 2)
        qi_b = to_b(qi_s).reshape(nb, nt * IDX_HEADS, IDX_DIM)
        kw_b = to_b(kw_s)
        w_b = jnp.broadcast_to(kw_b[:, :, IDX_DIM:IDX_DIM + IDX_HEADS].reshape(nb, nt * IDX_HEADS, 1),
                               (nb, nt * IDX_HEADS, LANES))
        ki_new = _pad_to(kw_b[:, :, :IDX_DIM], 1, PAGE_SIZE)
        cache_ki_t = jnp.swapaxes(cache_k_idx, 2, 3)
        past_sc, new_sc = _sample_scores(page_table, qi_b, w_b, ki_new, cache_ki_t, layer=l, nt=nt)
        n_past = past_sc.shape[2]
        bias, bias_new = _sample_select(past_sc.reshape(nb * nt, n_past), new_sc.reshape(nb * nt, PAGE_SIZE), nt=nt)
        q_b = to_b(q_s).reshape(nb, nt, N_KV_HEADS, GQA_GROUP, HEAD_DIM)
        q_b = q_b.transpose(0, 2, 3, 1, 4).reshape(nb, N_Q_HEADS * nt, HEAD_DIM)
        new_rows = lambda a: to_b(a).reshape(nb, nt * N_KV_HEADS, HEAD_DIM)
        o_b = _sample_attn(page_table, q_b, bias, bias_new, new_rows(k_s), new_rows(v_s), cache_k, cache_v,
                           layer=l, nt=nt)
        attn_s = o_b.reshape(nb, N_KV_HEADS, GQA_GROUP, nt, HEAD_DIM).transpose(3, 0, 1, 2, 4)
        attn_s = attn_s.reshape(nt * nb, d_q).astype(BF16)

        x2p = _proj_out_ln(x1p, rnn_p, attn_p, w_out_b, vec(ln2_g[l]), vec(ln2_b[l]), alpha=alpha, tm=tm_r)
        x2s = _proj_out_ln(x1s, rnn_s, attn_s, w_out_b, vec(ln2_g[l]), vec(ln2_b[l]), alpha=alpha, tm=tm_s)
        xp = _ffn_ln(x2p, wg2, wu2, wd2, vec(ln3_g[l]), vec(ln3_b[l]), alpha=alpha, tm=tm_p, tf=tf)
        xs = _ffn_ln(x2s, wg2, wu2, wd2, vec(ln3_g[l]), vec(ln3_b[l]), alpha=alpha, tm=tm_s, tf=tf)

        xr_p3 = xr_p.reshape(bp, seq, d_rnn)
        conv_p = xr_p3[:, -(CONV_W - 1):]
        conv_s = xpad_s.reshape(CONV_W - 1 + nt, nb, d_rnn)[-(CONV_W - 1):].transpose(1, 0, 2)
        layer_out = (
            k_p.reshape(bp, seq, N_KV_HEADS, HEAD_DIM), v_p.reshape(bp, seq, N_KV_HEADS, HEAD_DIM),
            kw_p[:, :IDX_DIM].reshape(bp, seq, IDX_DIM), conv_p, hl_p.reshape(bp, d_rnn),
            to_b(k_s).reshape(nb, nt, N_KV_HEADS, HEAD_DIM), to_b(v_s).reshape(nb, nt, N_KV_HEADS, HEAD_DIM),
            kw_b[:, :, :IDX_DIM], conv_s, hl_s,
        )
        for acc, val in zip(outs, layer_out):
            acc.append(val)

    y_p = xp.reshape(bp, seq, d_model)
    y_s = xs.reshape(nt, nb, d_model).transpose(1, 0, 2)
    return (y_p, y_s) + tuple(jnp.stack(o) for o in outs)
```

```python
import functools

import jax
import jax.numpy as jnp
from jax import lax
from jax.experimental import pallas as pl
from jax.experimental.pallas import tpu as pltpu

F32 = jnp.float32
BF16 = jnp.bfloat16
I32 = jnp.int32

LRU_BLOCKS = 8
CONV_W = 4
LRU_C = 8.0
HEAD_DIM = 128
N_KV_HEADS = 4
GQA_GROUP = 2
N_Q_HEADS = N_KV_HEADS * GQA_GROUP
IDX_HEADS = 8
IDX_DIM = 64
TOPK_MAX = 256
QUERY_ROWS = (512, 256, 128)
COUNT_ROWS = 128
PAGE_SIZE = 128
LN_EPS = 1e-5
ATTN_SCALE = HEAD_DIM ** -0.5
SOFTMAX_LOG2_SCALE = ATTN_SCALE * 1.4426950408889634
IDX_SCALE = IDX_DIM ** -0.5
IDX_W_SCALE = IDX_HEADS ** -0.5

LANES = 128
SUBLANES = 8
MXU_DIM = 256
VMEM_LIMIT = 56 * 1024 * 1024

INT_MIN = -2 ** 31
FLT_LOWEST = -3.4028234663852886e38
NEG_BIG = -1e30
KEY_CHUNK = 512
PAGES_PER_STEP = 16
IDX_PAGES_PER_STEP = 16


def _cparams(semantics):
    return pltpu.CompilerParams(dimension_semantics=semantics, vmem_limit_bytes=VMEM_LIMIT)


def _layer_norm(y, g, b):
    mu = jnp.mean(y, axis=-1, keepdims=True)
    d = y - mu
    var = jnp.mean(d * d, axis=-1, keepdims=True)
    return d * lax.rsqrt(var + LN_EPS) * g + b


def _dot(a, b):
    return jnp.dot(a, b, preferred_element_type=F32)


def _dot_nt(a, b):
    return lax.dot_general(a, b, (((1,), (1,)), ((), ())), preferred_element_type=F32)


def _log2(n):
    assert n > 0 and n & (n - 1) == 0, n
    return n.bit_length() - 1


def _ffn_kernel(x_ref, wg_ref, wu_ref, wd_ref, g_ref, b_ref, o_ref, xb_ref, *, alpha, tail):
    j = pl.program_id(1)

    @pl.when(j == 0)
    def _():
        xb_ref[...] = x_ref[...].astype(BF16)
        o_ref[...] = jnp.zeros_like(o_ref)

    last = pl.num_programs(1) - 1

    def chunk(width):
        xb = xb_ref[...]
        gate = _dot(xb, wg_ref[:, 0:width].astype(BF16))
        up = _dot(xb, wu_ref[:, 0:width])
        act = (gate * jax.nn.sigmoid(gate) * up).astype(BF16)
        o_ref[...] += _dot(act, wd_ref[0:width, :].astype(BF16))

    if tail == wg_ref.shape[1]:
        chunk(tail)
    else:
        pl.when(j < last)(lambda: chunk(wg_ref.shape[1]))
        pl.when(j == last)(lambda: chunk(tail))

    @pl.when(j == last)
    def _():
        y = alpha * x_ref[...] + 0.5 * o_ref[...]
        o_ref[...] = _layer_norm(y, g_ref[...], b_ref[...])


def _ffn_ln(x, w_gu, wu, w_down, g, b, *, alpha, tm, tf):
    rows, d = x.shape
    d_ff = w_down.shape[0]
    nj = pl.cdiv(d_ff, tf)
    return pl.pallas_call(
        functools.partial(_ffn_kernel, alpha=alpha, tail=d_ff - (nj - 1) * tf),
        grid=(rows // tm, nj),
        in_specs=[
            pl.BlockSpec((tm, d), lambda i, j: (i, 0)),
            pl.BlockSpec((d, tf), lambda i, j: (0, j)),
            pl.BlockSpec((d, tf), lambda i, j: (0, j)),
            pl.BlockSpec((tf, d), lambda i, j: (j, 0)),
            pl.BlockSpec((1, d), lambda i, j: (0, 0)),
            pl.BlockSpec((1, d), lambda i, j: (0, 0)),
        ],
        out_specs=pl.BlockSpec((tm, d), lambda i, j: (i, 0)),
        out_shape=jax.ShapeDtypeStruct((rows, d), F32),
        scratch_shapes=[pltpu.VMEM((tm, d), BF16)],
        compiler_params=_cparams(("parallel", "arbitrary")),
        name="ffn_ln",
    )(x, w_gu, wu, w_down, g, b)


def _proj_in_kernel(x_ref, w_ref, xr_ref, gr_ref, q_ref, k_ref, v_ref, kb_ref, vb_ref, qi_ref, kw_ref,
                    *, d_rnn, d_q, d_kv, d_qi):
    xb = x_ref[...].astype(BF16)
    off = [0]

    def seg(width):
        lo = off[0]
        off[0] = lo + width
        return _dot(xb, w_ref[:, lo:lo + width])

    xr_ref[...] = seg(d_rnn)
    gr_ref[...] = seg(d_rnn)
    q_ref[...] = seg(d_q).astype(BF16)
    k = seg(d_kv)
    k_ref[...] = k
    kb_ref[...] = k.astype(BF16)
    v = seg(d_kv)
    v_ref[...] = v
    vb_ref[...] = v.astype(BF16)
    qi_ref[...] = seg(d_qi).astype(BF16)
    kw_ref[...] = seg(LANES)


def _proj_in(x, w, *, tm, d_rnn, d_q, d_kv, d_qi):
    rows, d = x.shape
    widths = (d_rnn, d_rnn, d_q, d_kv, d_kv, d_kv, d_kv, d_qi, LANES)
    dtypes = (F32, F32, BF16, F32, F32, BF16, BF16, BF16, F32)
    return pl.pallas_call(
        functools.partial(_proj_in_kernel, d_rnn=d_rnn, d_q=d_q, d_kv=d_kv, d_qi=d_qi),
        grid=(rows // tm,),
        in_specs=[
            pl.BlockSpec((tm, d), lambda i: (i, 0)),
            pl.BlockSpec(w.shape, lambda i: (0, 0), pipeline_mode=pl.Buffered(1)),
        ],
        out_specs=[pl.BlockSpec((tm, wd), lambda i: (i, 0)) for wd in widths],
        out_shape=[jax.ShapeDtypeStruct((rows, wd), dt) for wd, dt in zip(widths, dtypes)],
        compiler_params=_cparams(("parallel",)),
        name="proj_in",
    )(x, w)


def _softplus(z):
    return jnp.maximum(z, 0.0) + jnp.log1p(jnp.exp(-jnp.abs(z)))


def _lru_gates(xc, wa_ref, ba, wi_ref, bi, lam):
    bw = xc.shape[1] // LRU_BLOCKS
    sp = _softplus(-lam)
    a_parts, u_parts = [], []
    for n in range(LRU_BLOCKS):
        sl = slice(n * bw, (n + 1) * bw)
        xn = xc[:, sl]
        xb = xn.astype(BF16)
        r = jax.nn.sigmoid(_dot(xb, wa_ref[n]) + ba[:, sl])
        i = jax.nn.sigmoid(_dot(xb, wi_ref[n]) + bi[:, sl])
        log_a = -LRU_C * r * sp[:, sl]
        a = jnp.exp(log_a)
        a_parts.append(a)
        u_parts.append(jnp.sqrt(-jnp.tanh(log_a) * (a * a + 1.0)) * i * xn)
    return jnp.concatenate(a_parts, axis=1), jnp.concatenate(u_parts, axis=1)


def _lru_prompt_kernel(xr_ref, gr_ref, cw_ref, cb_ref, wa_ref, ba_ref, wi_ref, bi_ref, lam_ref,
                       o_ref, hl_ref, xp_ref, a_ref, u_ref, hs_ref, hc_ref, *, tt):
    j = pl.program_id(1)
    d = xr_ref.shape[1]

    @pl.when(j == 0)
    def _():
        xp_ref[0:SUBLANES, :] = jnp.zeros((SUBLANES, d), F32)
        hc_ref[...] = jnp.zeros_like(hc_ref)

    @pl.when(j > 0)
    def _():
        xp_ref[0:SUBLANES, :] = xp_ref[tt:tt + SUBLANES, :]

    xp_ref[SUBLANES:SUBLANES + tt, :] = xr_ref[...]
    cw = cw_ref[...]
    xc = cb_ref[...]
    for jj in range(CONV_W):
        lo = SUBLANES - (CONV_W - 1) + jj
        xc = xc + cw[jj:jj + 1, :] * xp_ref[lo:lo + tt, :]

    a, u = _lru_gates(xc, wa_ref, ba_ref[...], wi_ref, bi_ref[...], lam_ref[...])
    a_ref[...] = a
    u_ref[...] = u

    row = lax.broadcasted_iota(I32, (SUBLANES, d), 0)

    def group(g, h):
        r0 = pl.multiple_of(g * SUBLANES, SUBLANES)
        a8 = a_ref[pl.ds(r0, SUBLANES), :]
        u8 = u_ref[pl.ds(r0, SUBLANES), :]
        out = jnp.zeros((SUBLANES, d), F32)
        for jj in range(SUBLANES):
            aj = jnp.broadcast_to(a8[jj:jj + 1, :], (SUBLANES, d))
            uj = jnp.broadcast_to(u8[jj:jj + 1, :], (SUBLANES, d))
            h = aj * h + uj
            out = jnp.where(row == jj, h, out)
        hs_ref[pl.ds(r0, SUBLANES), :] = out
        return h

    h = lax.fori_loop(0, tt // SUBLANES, group, hc_ref[...])
    hc_ref[...] = h
    hl_ref[...] = h[0:1, :]
    o_ref[...] = (hs_ref[...] * jax.nn.gelu(gr_ref[...])).astype(BF16)


def _lru_prompt(xr, gr, cw, cb, wa, ba, wi, bi, lam, *, batch, tt):
    rows, d = xr.shape
    seq = rows // batch
    nt = seq // tt
    row_spec = pl.BlockSpec((tt, d), lambda b, j: (b * nt + j, 0))
    vec_spec = pl.BlockSpec((1, d), lambda b, j: (0, 0))
    w_spec = pl.BlockSpec(wa.shape, lambda b, j: (0, 0, 0))
    return pl.pallas_call(
        functools.partial(_lru_prompt_kernel, tt=tt),
        grid=(batch, nt),
        in_specs=[row_spec, row_spec, pl.BlockSpec((CONV_W, d), lambda b, j: (0, 0)), vec_spec,
                  w_spec, vec_spec, w_spec, vec_spec, vec_spec],
        out_specs=[row_spec, pl.BlockSpec((None, 1, d), lambda b, j: (b, 0, 0))],
        out_shape=[jax.ShapeDtypeStruct((rows, d), BF16), jax.ShapeDtypeStruct((batch, 1, d), F32)],
        scratch_shapes=[pltpu.VMEM((tt + SUBLANES, d), F32), pltpu.VMEM((tt, d), F32),
                        pltpu.VMEM((tt, d), F32), pltpu.VMEM((tt, d), F32), pltpu.VMEM((SUBLANES, d), F32)],
        compiler_params=_cparams(("parallel", "arbitrary")),
        name="lru_prompt",
    )(xr, gr, cw, cb, wa, ba, wi, bi, lam)


def _lru_sample_kernel(xp_ref, gr_ref, h0_ref, cw_ref, cb_ref, wa_ref, ba_ref, wi_ref, bi_ref, lam_ref,
                       o_ref, hl_ref, *, nb, nt):
    cw = cw_ref[...]
    xc = cb_ref[...] + cw[0:1, :] * xp_ref[0:nt * nb, :]
    for jj in range(1, CONV_W):
        xc = xc + cw[jj:jj + 1, :] * xp_ref[jj * nb:(jj + nt) * nb, :]
    a, u = _lru_gates(xc, wa_ref, ba_ref[...], wi_ref, bi_ref[...], lam_ref[...])
    gate = jax.nn.gelu(gr_ref[...])
    h = h0_ref[...]
    for t in range(nt):
        sl = slice(t * nb, (t + 1) * nb)
        h = a[sl, :] * h + u[sl, :]
        o_ref[sl, :] = (h * gate[sl, :]).astype(BF16)
    hl_ref[...] = h


def _lru_sample(xp, gr, h0, cw, cb, wa, ba, wi, bi, lam, *, nb, nt):
    d = gr.shape[1]
    return pl.pallas_call(
        functools.partial(_lru_sample_kernel, nb=nb, nt=nt),
        out_shape=[jax.ShapeDtypeStruct((nt * nb, d), BF16), jax.ShapeDtypeStruct((nb, d), F32)],
        compiler_params=pltpu.CompilerParams(vmem_limit_bytes=VMEM_LIMIT),
        name="lru_sample",
    )(xp, gr, h0, cw, cb, wa, ba, wi, bi, lam)


def _rank_to_float(u):
    key = u ^ jnp.int32(INT_MIN)
    return lax.bitcast_convert_type(key ^ ((key >> 31) & jnp.int32(0x7FFFFFFF)), F32)


def _count(sc_ref, nc, rows, ck, thr, strict):
    accs = []
    for r0 in range(0, rows, COUNT_ROWS):
        nr = min(COUNT_ROWS, rows - r0)
        thr_b = jnp.broadcast_to(thr[r0:r0 + nr], (nr, LANES))

        def body(c, acc, r0=r0, nr=nr, thr_b=thr_b):
            for t in range(ck // LANES):
                tile = sc_ref[c, r0:r0 + nr, t * LANES:(t + 1) * LANES]
                hit = (tile > thr_b) if strict else (tile >= thr_b)
                acc = acc + jnp.where(hit, 1.0, 0.0)
            return acc

        accs.append(lax.fori_loop(0, nc, body, jnp.zeros((nr, LANES), F32)))
    acc = accs[0] if len(accs) == 1 else jnp.concatenate(accs, axis=0)
    return jnp.sum(acc, axis=1, keepdims=True)


def _select_topk(sc_ref, tau_ref, nc, rows, ck, k_sel):
    kf = float(k_sel)

    def bit_body(it, carry):
        u, n_ge = carry
        cand = u | jnp.left_shift(jnp.int32(1), 31 - it)
        cnt = _count(sc_ref, nc, rows, ck, _rank_to_float(cand), strict=False)
        take = cnt >= kf
        return jnp.where(take, cand, u), jnp.where(take, cnt, n_ge)

    u, n_ge = lax.fori_loop(0, 32, bit_body, (jnp.zeros((rows, 1), I32), jnp.full((rows, 1), kf, F32)))
    tau = _rank_to_float(u)
    tau_ref[...] = jnp.broadcast_to(tau, (rows, LANES))

    @pl.when(jnp.max(n_ge) > kf)
    def _():
        before = (lax.broadcasted_iota(I32, (ck, ck), 0) < lax.broadcasted_iota(I32, (ck, ck), 1))
        before = jnp.where(before, 1.0, 0.0).astype(BF16)
        tau_w = jnp.broadcast_to(tau, (rows, ck))
        need = kf - _count(sc_ref, nc, rows, ck, tau, strict=True)
        need_w = jnp.broadcast_to(need, (rows, ck))

        def body(c, seen):
            sc = sc_ref[c]
            eq = sc == tau_w
            eqf = jnp.where(eq, 1.0, 0.0)
            rank = _dot(eqf.astype(BF16), before) + seen
            sc_ref[c] = jnp.where(eq, jnp.where(rank >= need_w, -jnp.inf, sc), sc)
            return seen + jnp.sum(eqf, axis=1, keepdims=True)

        lax.fori_loop(0, nc, body, jnp.zeros((rows, 1), F32))


def _prompt_attn_kernel(q_ref, qi_ref, kwq_ref, kwk_ref, kb_ref, vb_ref, o_ref,
                        ki_ref, sc_ref, tau_ref, qs_ref, m_ref, l_ref, acc_ref, *, k_sel, qb):
    i = pl.program_id(1)
    ck = sc_ref.shape[2]
    nc = (i * qb + qb + ck - 1) // ck

    @pl.when(i == 0)
    def _():
        ki_ref[...] = kwk_ref[:, 0:IDX_DIM].astype(BF16)

    w_idx = kwq_ref[:, IDX_DIM:IDX_DIM + IDX_HEADS] * IDX_W_SCALE * IDX_SCALE
    t_pos = i * qb + lax.broadcasted_iota(I32, (qb, ck), 0)
    col = lax.broadcasted_iota(I32, (qb, ck), 1)

    def score_body(c, carry):
        c0 = pl.multiple_of(c * ck, ck)
        kc = ki_ref[pl.ds(c0, ck), :]
        score = jnp.zeros((qb, ck), F32)
        for h in range(IDX_HEADS):
            s = _dot_nt(qi_ref[:, h * IDX_DIM:(h + 1) * IDX_DIM], kc)
            score = score + jnp.maximum(s, 0.0) * w_idx[:, h:h + 1]
        sc_ref[c] = jnp.where(col + c0 <= t_pos, score, -jnp.inf)
        return carry

    lax.fori_loop(0, nc, score_body, 0)

    need_select = (i + 1) * qb > k_sel

    @pl.when(jnp.logical_not(need_select))
    def _():
        tau_ref[...] = jnp.full(tau_ref.shape, FLT_LOWEST, F32)

    @pl.when(need_select)
    def _():
        _select_topk(sc_ref, tau_ref, nc, qb, ck, k_sel)
        row_pos = i * qb + lax.broadcasted_iota(I32, (qb, LANES), 0)
        tau_ref[...] = jnp.where(row_pos < k_sel, FLT_LOWEST, tau_ref[...])

    for n in range(N_KV_HEADS):
        for g in range(GQA_GROUP):
            hq = n * GQA_GROUP + g
            qs_ref[n, g * qb:(g + 1) * qb, :] = q_ref[:, hq * HEAD_DIM:(hq + 1) * HEAD_DIM]
    m_ref[...] = jnp.full(m_ref.shape, NEG_BIG, F32)
    l_ref[...] = jnp.zeros_like(l_ref)
    acc_ref[...] = jnp.zeros_like(acc_ref)
    tau_w = jnp.broadcast_to(tau_ref[:, 0:1], (qb, ck))

    def attn_body(c, carry):
        c0 = pl.multiple_of(c * ck, ck)
        bias = jnp.where(sc_ref[c] >= tau_w, 0.0, NEG_BIG)
        bias = jnp.concatenate([bias] * GQA_GROUP, axis=0)
        for n in range(N_KV_HEADS):
            kn = kb_ref[pl.ds(c0, ck), n * HEAD_DIM:(n + 1) * HEAD_DIM]
            vn = vb_ref[pl.ds(c0, ck), n * HEAD_DIM:(n + 1) * HEAD_DIM]
            s = _dot_nt(qs_ref[n], kn) + bias
            tiles = [s[:, t * LANES:(t + 1) * LANES] for t in range(ck // LANES)]
            m_prev = m_ref[n]
            m_new = jnp.maximum(m_prev, jnp.max(functools.reduce(jnp.maximum, tiles), axis=1, keepdims=True))
            alpha = jnp.exp2((m_prev - m_new) * SOFTMAX_LOG2_SCALE)
            p_tiles = [jnp.exp2((t - m_new) * SOFTMAX_LOG2_SCALE) for t in tiles]
            l_ref[n] = alpha * l_ref[n] + functools.reduce(jnp.add, p_tiles)
            p = jnp.concatenate(p_tiles, axis=1).astype(BF16)
            acc_ref[n] = alpha * acc_ref[n] + _dot(p, vn)
            m_ref[n] = m_new
        return carry

    lax.fori_loop(0, nc, attn_body, 0)

    for n in range(N_KV_HEADS):
        o = acc_ref[n] / jnp.sum(l_ref[n], axis=1, keepdims=True)
        for g in range(GQA_GROUP):
            hq = n * GQA_GROUP + g
            o_ref[:, hq * HEAD_DIM:(hq + 1) * HEAD_DIM] = o[g * qb:(g + 1) * qb, :].astype(BF16)


def _prompt_attn(q, qi, kw, kb, vb, *, batch):
    rows = q.shape[0]
    seq = rows // batch
    ck = min(KEY_CHUNK, seq)
    k_sel = min(TOPK_MAX, seq // 4)
    qb = next(r for r in QUERY_ROWS if seq % r == 0)
    nblk = seq // qb
    gq = GQA_GROUP * qb
    blk = lambda b, i: (b * nblk + i, 0)
    per_b = lambda b, i: (b, 0)
    return pl.pallas_call(
        functools.partial(_prompt_attn_kernel, k_sel=k_sel, qb=qb),
        grid=(batch, nblk),
        in_specs=[
            pl.BlockSpec((qb, q.shape[1]), blk),
            pl.BlockSpec((qb, qi.shape[1]), blk),
            pl.BlockSpec((qb, LANES), blk),
            pl.BlockSpec((seq, LANES), per_b),
            pl.BlockSpec((seq, kb.shape[1]), per_b),
            pl.BlockSpec((seq, vb.shape[1]), per_b),
        ],
        out_specs=pl.BlockSpec((qb, q.shape[1]), blk),
        out_shape=jax.ShapeDtypeStruct(q.shape, BF16),
        scratch_shapes=[
            pltpu.VMEM((seq, IDX_DIM), BF16),
            pltpu.VMEM((seq // ck, qb, ck), F32),
            pltpu.VMEM((qb, LANES), F32),
            pltpu.VMEM((N_KV_HEADS, gq, HEAD_DIM), BF16),
            pltpu.VMEM((N_KV_HEADS, gq, LANES), F32),
            pltpu.VMEM((N_KV_HEADS, gq, LANES), F32),
            pltpu.VMEM((N_KV_HEADS, gq, HEAD_DIM), F32),
        ],
        compiler_params=_cparams(("parallel", "arbitrary")),
        name="prompt_attn",
    )(q, qi, kw, kw, kb, vb)


def _page_copy(cache_ref, layer, page, buf_ref, sem_ref, slot, p):
    return pltpu.make_async_copy(cache_ref.at[layer, page], buf_ref.at[slot, p], sem_ref.at[slot])


def _prefetch_pages(pt_ref, npg, streams):
    b, c = pl.program_id(0), pl.program_id(1)
    n_c = pl.num_programs(1)
    step = b * n_c + c
    slot = lax.rem(step, 2)

    def start(bb, cc, sl):
        for cache_ref, layer, buf_ref, sem_ref in streams:
            for p in range(npg):
                _page_copy(cache_ref, layer, pt_ref[bb, cc * npg + p], buf_ref, sem_ref, sl, p).start()

    @pl.when(step == 0)
    def _():
        start(b, c, slot)

    @pl.when(step + 1 < pl.num_programs(0) * n_c)
    def _():
        wrap = c + 1 == n_c
        start(jnp.where(wrap, b + 1, b), jnp.where(wrap, 0, c + 1), 1 - slot)

    for cache_ref, layer, buf_ref, sem_ref in streams:
        for p in range(npg):
            _page_copy(cache_ref, layer, 0, buf_ref, sem_ref, slot, p).wait()
    return slot


def _sample_scores_kernel(pt_ref, qi_ref, w_ref, kin_ref, cache_ref, past_ref, new_ref, buf_ref, kc_ref, sem_ref,
                          *, layer, nt, npg):
    slot = _prefetch_pages(pt_ref, npg, [(cache_ref, layer, buf_ref, sem_ref)])
    c = pl.program_id(1)
    w = w_ref[...] * IDX_W_SCALE * IDX_SCALE
    qi = qi_ref[...]

    def scores(s):
        width = s.shape[1]
        s = jnp.maximum(s, 0.0) * jnp.concatenate([w] * (width // LANES), axis=1)
        return jnp.sum(s.reshape(nt, IDX_HEADS, width), axis=1)

    for p in range(npg):
        kc_ref[:, p * PAGE_SIZE:(p + 1) * PAGE_SIZE] = buf_ref[slot, p].astype(BF16)
    past_ref[...] = scores(_dot(qi, kc_ref[...]))

    @pl.when(c == 0)
    def _():
        new_ref[...] = scores(_dot_nt(qi, kin_ref[...].astype(BF16)))


def _sample_scores(page_table, qi_b, w_b, ki_new, cache_ki, *, layer, nt):
    nb, n_pages = page_table.shape
    npg = min(IDX_PAGES_PER_STEP, n_pages)
    rows = nt * IDX_HEADS
    grid_spec = pltpu.PrefetchScalarGridSpec(
        num_scalar_prefetch=1,
        grid=(nb, n_pages // npg),
        in_specs=[
            pl.BlockSpec((None, rows, IDX_DIM), lambda b, c, pt: (b, 0, 0)),
            pl.BlockSpec((None, rows, LANES), lambda b, c, pt: (b, 0, 0)),
            pl.BlockSpec((None, PAGE_SIZE, IDX_DIM), lambda b, c, pt: (b, 0, 0)),
            pl.BlockSpec(memory_space=pl.ANY),
        ],
        out_specs=[
            pl.BlockSpec((None, nt, npg * PAGE_SIZE), lambda b, c, pt: (b, 0, c)),
            pl.BlockSpec((None, nt, PAGE_SIZE), lambda b, c, pt: (b, 0, 0)),
        ],
        scratch_shapes=[pltpu.VMEM((2, npg, IDX_DIM, PAGE_SIZE), F32), pltpu.VMEM((IDX_DIM, npg * PAGE_SIZE), BF16),
                        pltpu.SemaphoreType.DMA((2,))],
    )
    return pl.pallas_call(
        functools.partial(_sample_scores_kernel, layer=layer, nt=nt, npg=npg),
        grid_spec=grid_spec,
        out_shape=[jax.ShapeDtypeStruct((nb, nt, n_pages * PAGE_SIZE), F32),
                   jax.ShapeDtypeStruct((nb, nt, PAGE_SIZE), F32)],
        compiler_params=_cparams(("arbitrary", "arbitrary")),
        name="sample_scores",
    )(page_table, qi_b, w_b, ki_new, cache_ki)


def _sample_select_kernel(past_ref, new_ref, bias_ref, biasn_ref, sc_ref, tau_ref, *, nt, k_sel, ck):
    rows, n_past = past_ref.shape
    npc = n_past // ck
    for c in range(npc):
        sc_ref[c] = past_ref[:, c * ck:(c + 1) * ck]
    t_row = lax.broadcasted_iota(I32, (rows, ck), 0) & (nt - 1)
    col = lax.broadcasted_iota(I32, (rows, ck), 1)
    new_sc = jnp.concatenate([new_ref[...]] * (ck // PAGE_SIZE), axis=1)
    sc_ref[npc] = jnp.where(col <= t_row, new_sc, -jnp.inf)
    _select_topk(sc_ref, tau_ref, npc + 1, rows, ck, k_sel)
    tau_b = tau_ref[...]
    d = (lax.broadcasted_iota(I32, (LANES, LANES * N_KV_HEADS), 1)
         - N_KV_HEADS * lax.broadcasted_iota(I32, (LANES, LANES * N_KV_HEADS), 0))
    expand = jnp.where(d >= 0, jnp.where(d < N_KV_HEADS, 1.0, 0.0), 0.0).astype(BF16)
    def bias_of(c, t):
        sel = jnp.where(sc_ref[c][:, t * LANES:(t + 1) * LANES] >= tau_b, 1.0, 0.0).astype(BF16)
        return (_dot(sel, expand) - 1.0) * (-NEG_BIG)

    for c in range(npc):
        for t in range(ck // LANES):
            lo = (c * ck + t * LANES) * N_KV_HEADS
            bias_ref[:, lo:lo + LANES * N_KV_HEADS] = bias_of(c, t)
    biasn_ref[...] = bias_of(npc, 0)[:, 0:LANES]


def _sample_select(past, new, *, nt):
    rows, n_past = past.shape
    assert nt & (nt - 1) == 0 and nt * N_KV_HEADS <= LANES
    ck = min(KEY_CHUNK, n_past)
    k_sel = min(TOPK_MAX, (n_past + nt) // 4)
    return pl.pallas_call(
        functools.partial(_sample_select_kernel, nt=nt, k_sel=k_sel, ck=ck),
        out_shape=[jax.ShapeDtypeStruct((rows, n_past * N_KV_HEADS), F32), jax.ShapeDtypeStruct((rows, LANES), F32)],
        scratch_shapes=[pltpu.VMEM((n_past // ck + 1, rows, ck), F32), pltpu.VMEM((rows, LANES), F32)],
        compiler_params=pltpu.CompilerParams(vmem_limit_bytes=VMEM_LIMIT),
        name="sample_select",
    )(past, new)


def _sample_attn_kernel(pt_ref, q_ref, bias_ref, biasn_ref, kn_ref, vn_ref, ck_ref, cv_ref, o_ref,
                        kbuf_ref, vbuf_ref, kc_ref, vc_ref, m_ref, l_ref, acc_ref, ksem_ref, vsem_ref, *, layer, nt, npg):
    slot = _prefetch_pages(pt_ref, npg, [(ck_ref, layer, kbuf_ref, ksem_ref), (cv_ref, layer, vbuf_ref, vsem_ref)])
    c = pl.program_id(1)
    rows = q_ref.shape[0]
    page_rows = PAGE_SIZE * N_KV_HEADS

    @pl.when(c == 0)
    def _():
        m_ref[...] = jnp.full(m_ref.shape, NEG_BIG, F32)
        l_ref[...] = jnp.zeros_like(l_ref)
        acc_ref[...] = jnp.zeros_like(acc_ref)

    row_head = lax.shift_right_logical(lax.broadcasted_iota(I32, (rows, page_rows), 0), _log2(rows // N_KV_HEADS))
    col_head = lax.broadcasted_iota(I32, (rows, page_rows), 1) & (N_KV_HEADS - 1)
    head_bias = jnp.where(row_head == col_head, 0.0, NEG_BIG)
    first_of_pair = pl.program_id(0) % 2 == 0

    def update(k2, v2, bias_blk):
        s = _dot_nt(q_ref[...], k2)
        bias = jnp.where(first_of_pair, bias_blk[0:nt], bias_blk[nt:2 * nt])
        bias = jnp.concatenate([bias] * (rows // nt), axis=0)
        lanes = lambda a, t: a[:, t * LANES:(t + 1) * LANES]
        per_page = page_rows // LANES
        tiles = [lanes(s, t) + (lanes(bias, t) + lanes(head_bias, t % per_page)) for t in range(k2.shape[0] // LANES)]
        m_prev = m_ref[...]
        m_new = jnp.maximum(m_prev, jnp.max(functools.reduce(jnp.maximum, tiles), axis=1, keepdims=True))
        alpha = jnp.exp2((m_prev - m_new) * SOFTMAX_LOG2_SCALE)
        p_tiles = [jnp.exp2((t - m_new) * SOFTMAX_LOG2_SCALE) for t in tiles]
        l_ref[...] = alpha * l_ref[...] + functools.reduce(jnp.add, p_tiles)
        acc_ref[...] = alpha * acc_ref[...] + _dot(jnp.concatenate(p_tiles, axis=1).astype(BF16), v2)
        m_ref[...] = m_new

    for p in range(npg):
        kc_ref[p * page_rows:(p + 1) * page_rows, :] = kbuf_ref[slot, p].reshape(page_rows, HEAD_DIM).astype(BF16)
        vc_ref[p * page_rows:(p + 1) * page_rows, :] = vbuf_ref[slot, p].reshape(page_rows, HEAD_DIM).astype(BF16)
    update(kc_ref[...], vc_ref[...], bias_ref[...])

    @pl.when(c == pl.num_programs(1) - 1)
    def _():
        def new_rows(ref):
            pad = jnp.zeros((LANES - ref.shape[0], HEAD_DIM), F32)
            return jnp.concatenate([ref[...], pad], axis=0).astype(BF16)

        update(new_rows(kn_ref), new_rows(vn_ref), biasn_ref[...])
        o_ref[...] = acc_ref[...] / jnp.sum(l_ref[...], axis=1, keepdims=True)


def _sample_attn(page_table, q_b, bias, bias_new, k_new, v_new, cache_k, cache_v, *, layer, nt):
    nb, n_pages = page_table.shape
    npg = min(PAGES_PER_STEP, n_pages)
    rows = q_b.shape[1]
    page_rows = PAGE_SIZE * N_KV_HEADS
    assert N_KV_HEADS & (N_KV_HEADS - 1) == 0 and 2 * nt == SUBLANES and nb % 2 == 0

    per_b3 = lambda b, c, pt: (b, 0, 0)
    grid_spec = pltpu.PrefetchScalarGridSpec(
        num_scalar_prefetch=1,
        grid=(nb, n_pages // npg),
        in_specs=[
            pl.BlockSpec((None, rows, HEAD_DIM), per_b3),
            pl.BlockSpec((2 * nt, npg * page_rows), lambda b, c, pt: (b // 2, c)),
            pl.BlockSpec((2 * nt, LANES), lambda b, c, pt: (b // 2, 0)),
            pl.BlockSpec((None, nt * N_KV_HEADS, HEAD_DIM), per_b3),
            pl.BlockSpec((None, nt * N_KV_HEADS, HEAD_DIM), per_b3),
            pl.BlockSpec(memory_space=pl.ANY),
            pl.BlockSpec(memory_space=pl.ANY),
        ],
        out_specs=pl.BlockSpec((None, rows, HEAD_DIM), per_b3),
        scratch_shapes=[
            pltpu.VMEM((2, npg, PAGE_SIZE, N_KV_HEADS, HEAD_DIM), F32),
            pltpu.VMEM((2, npg, PAGE_SIZE, N_KV_HEADS, HEAD_DIM), F32),
            pltpu.VMEM((npg * page_rows, HEAD_DIM), BF16),
            pltpu.VMEM((npg * page_rows, HEAD_DIM), BF16),
            pltpu.VMEM((rows, LANES), F32),
            pltpu.VMEM((rows, LANES), F32),
            pltpu.VMEM((rows, HEAD_DIM), F32),
            pltpu.SemaphoreType.DMA((2,)),
            pltpu.SemaphoreType.DMA((2,)),
        ],
    )
    return pl.pallas_call(
        functools.partial(_sample_attn_kernel, layer=layer, nt=nt, npg=npg),
        grid_spec=grid_spec,
        out_shape=jax.ShapeDtypeStruct((nb, rows, HEAD_DIM), F32),
        compiler_params=_cparams(("arbitrary", "arbitrary")),
        name="sample_attn",
    )(page_table, q_b, bias, bias_new, k_new, v_new, cache_k, cache_v)


def _proj_out_kernel(x_ref, r_ref, a_ref, w_ref, g_ref, b_ref, o_ref, *, alpha):
    d_rnn = r_ref.shape[1]
    mix = _dot(r_ref[...], w_ref[0:d_rnn, :]) + _dot(a_ref[...], w_ref[d_rnn:, :])
    o_ref[...] = _layer_norm(alpha * x_ref[...] + mix, g_ref[...], b_ref[...])


def _proj_out_ln(x, rnn, attn, w, g, b, *, alpha, tm):
    rows, d = x.shape
    return pl.pallas_call(
        functools.partial(_proj_out_kernel, alpha=alpha),
        grid=(rows // tm,),
        in_specs=[
            pl.BlockSpec((tm, d), lambda i: (i, 0)),
            pl.BlockSpec((tm, rnn.shape[1]), lambda i: (i, 0)),
            pl.BlockSpec((tm, attn.shape[1]), lambda i: (i, 0)),
            pl.BlockSpec(w.shape, lambda i: (0, 0)),
            pl.BlockSpec((1, d), lambda i: (0, 0)),
            pl.BlockSpec((1, d), lambda i: (0, 0)),
        ],
        out_specs=pl.BlockSpec((tm, d), lambda i: (i, 0)),
        out_shape=jax.ShapeDtypeStruct((rows, d), F32),
        compiler_params=_cparams(("parallel",)),
        name="proj_out_ln",
    )(x, rnn, attn, w, g, b)


def _row_tile(rows, target):
    tm = min(rows, target)
    while rows % tm:
        tm //= 2
    return tm


def _pad_to(a, axis, size):
    pad = [(0, 0)] * a.ndim
    pad[axis] = (0, size - a.shape[axis])
    return jnp.pad(a, pad)


def _ffn_weights(w_gu, w_down):
    d_ff = w_down.shape[0]
    return w_gu, w_gu[:, d_ff:].astype(BF16), w_down, MXU_DIM


def kernel(x_prompt, x_sample, cache_k, cache_v, cache_k_idx, state_conv, state_rnn, page_table, ln1_g, ln1_b, ffn1_w_gu, ffn1_w_down, w_in, conv_w, conv_b, lru_w_a, lru_b_a, lru_w_i, lru_b_i, lru_lambda, w_out, ln2_g, ln2_b, ffn2_w_gu, ffn2_w_down, ln3_g, ln3_b):
    depth = w_in.shape[0]
    bp, seq, d_model = x_prompt.shape
    nb, nt, _ = x_sample.shape
    d_rnn = conv_w.shape[2]
    d_q = N_Q_HEADS * HEAD_DIM
    d_kv = N_KV_HEADS * HEAD_DIM
    d_qi = IDX_HEADS * IDX_DIM
    alpha = (2.0 * depth) ** 0.25

    xp = x_prompt.reshape(bp * seq, d_model)
    xs = x_sample.transpose(1, 0, 2).reshape(nt * nb, d_model)
    tm_p = _row_tile(bp * seq, 1024)
    tm_r = _row_tile(bp * seq, 512)
    tm_s = _row_tile(nt * nb, 512)
    vec = lambda a: a.reshape(1, -1)

    outs = [[] for _ in range(10)]
    for l in range(depth):
        wg1, wu1, wd1, tf = _ffn_weights(ffn1_w_gu[l], ffn1_w_down[l])
        wg2, wu2, wd2, _ = _ffn_weights(ffn2_w_gu[l], ffn2_w_down[l])
        d_in = w_in.shape[2]
        w_in_b = _pad_to(w_in[l].astype(BF16), 1, d_in - IDX_DIM - IDX_HEADS + LANES)
        w_out_b = w_out[l].astype(BF16)
        wa_b = lru_w_a[l].astype(BF16)
        wi_b = lru_w_i[l].astype(BF16)
        lru_args = (conv_w[l], vec(conv_b[l]), wa_b, vec(lru_b_a[l]), wi_b, vec(lru_b_i[l]), vec(lru_lambda[l]))
        proj = functools.partial(_proj_in, d_rnn=d_rnn, d_q=d_q, d_kv=d_kv, d_qi=d_qi)

        x1p = _ffn_ln(xp, wg1, wu1, wd1, vec(ln1_g[l]), vec(ln1_b[l]), alpha=alpha, tm=tm_p, tf=tf)
        x1s = _ffn_ln(xs, wg1, wu1, wd1, vec(ln1_g[l]), vec(ln1_b[l]), alpha=alpha, tm=tm_s, tf=tf)
        xr_p, gr_p, q_p, k_p, v_p, kb_p, vb_p, qi_p, kw_p = proj(x1p, w_in_b, tm=_row_tile(bp * seq, 256))
        xr_s, gr_s, q_s, k_s, v_s, _, _, qi_s, kw_s = proj(x1s, w_in_b, tm=_row_tile(nt * nb, 256))

        rnn_p, hl_p = _lru_prompt(xr_p, gr_p, *lru_args, batch=bp, tt=_row_tile(seq, 512))
        conv_tm = state_conv[l].transpose(1, 0, 2).reshape((CONV_W - 1) * nb, d_rnn)
        xpad_s = jnp.concatenate([conv_tm, xr_s], axis=0)
        rnn_s, hl_s = _lru_sample(xpad_s, gr_s, state_rnn[l], *lru_args, nb=nb, nt=nt)

        attn_p = _prompt_attn(q_p, qi_p, kw_p, kb_p, vb_p, batch=bp)

        to_b = lambda a: a.reshape(nt, nb, -1).transpose(1, 0, 2)
        qi_b = to_b(qi_s).reshape(nb, nt * IDX_HEADS, IDX_DIM)
        kw_b = to_b(kw_s)
        w_b = jnp.broadcast_to(kw_b[:, :, IDX_DIM:IDX_DIM + IDX_HEADS].reshape(nb, nt * IDX_HEADS, 1),
                               (nb, nt * IDX_HEADS, LANES))
        ki_new = _pad_to(kw_b[:, :, :IDX_DIM], 1, PAGE_SIZE)
        cache_ki_t = jnp.swapaxes(cache_k_idx, 2, 3)
        past_sc, new_sc = _sample_scores(page_table, qi_b, w_b, ki_new, cache_ki_t, layer=l, nt=nt)
        n_past = past_sc.shape[2]
        bias, bias_new = _sample_select(past_sc.reshape(nb * nt, n_past), new_sc.reshape(nb * nt, PAGE_SIZE), nt=nt)
        q_b = to_b(q_s).reshape(nb, nt, N_KV_HEADS, GQA_GROUP, HEAD_DIM)
        q_b = q_b.transpose(0, 2, 3, 1, 4).reshape(nb, N_Q_HEADS * nt, HEAD_DIM)
        new_rows = lambda a: to_b(a).reshape(nb, nt * N_KV_HEADS, HEAD_DIM)
        o_b = _sample_attn(page_table, q_b, bias, bias_new, new_rows(k_s), new_rows(v_s), cache_k, cache_v,
                           layer=l, nt=nt)
        attn_s = o_b.reshape(nb, N_KV_HEADS, GQA_GROUP, nt, HEAD_DIM).transpose(3, 0, 1, 2, 4)
        attn_s = attn_s.reshape(nt * nb, d_q).astype(BF16)

        x2p = _proj_out_ln(x1p, rnn_p, attn_p, w_out_b, vec(ln2_g[l]), vec(ln2_b[l]), alpha=alpha, tm=tm_r)
        x2s = _proj_out_ln(x1s, rnn_s, attn_s, w_out_b, vec(ln2_g[l]), vec(ln2_b[l]), alpha=alpha, tm=tm_s)
        xp = _ffn_ln(x2p, wg2, wu2, wd2, vec(ln3_g[l]), vec(ln3_b[l]), alpha=alpha, tm=tm_p, tf=tf)
        xs = _ffn_ln(x2s, wg2, wu2, wd2, vec(ln3_g[l]), vec(ln3_b[l]), alpha=alpha, tm=tm_s, tf=tf)

        xr_p3 = xr_p.reshape(bp, seq, d_rnn)
        conv_p = xr_p3[:, -(CONV_W - 1):]
        conv_s = xpad_s.reshape(CONV_W - 1 + nt, nb, d_rnn)[-(CONV_W - 1):].transpose(1, 0, 2)
        layer_out = (
            k_p.reshape(bp, seq, N_KV_HEADS, HEAD_DIM), v_p.reshape(bp, seq, N_KV_HEADS, HEAD_DIM),
            kw_p[:, :IDX_DIM].reshape(bp, seq, IDX_DIM), conv_p, hl_p.reshape(bp, d_rnn),
            to_b(k_s).reshape(nb, nt, N_KV_HEADS, HEAD_DIM), to_b(v_s).reshape(nb, nt, N_KV_HEADS, HEAD_DIM),
            kw_b[:, :, :IDX_DIM], conv_s, hl_s,
        )
        for acc, val in zip(outs, layer_out):
            acc.append(val)

    y_p = xp.reshape(bp, seq, d_model)
    y_s = xs.reshape(nt, nb, d_model).transpose(1, 0, 2)
    return (y_p, y_s) + tuple(jnp.stack(o) for o in outs)
```

```python
import functools
import math

import jax
import jax.numpy as jnp
from jax import lax
from jax.experimental import pallas as pl
from jax.experimental.pallas import tpu as pltpu

F32 = jnp.float32
BF16 = jnp.bfloat16
I32 = jnp.int32

LRU_BLOCKS = 8
CONV_W = 4
LRU_C = 8.0
HEAD_DIM = 128
N_KV_HEADS = 4
GQA_GROUP = 2
N_Q_HEADS = N_KV_HEADS * GQA_GROUP
IDX_HEADS = 8
IDX_DIM = 64
TOPK_MAX = 256
QUERY_ROWS = (512, 256, 128)
COUNT_ROWS = 128
PAGE_SIZE = 128
LN_EPS = 1e-5
ATTN_SCALE = HEAD_DIM ** -0.5
SOFTMAX_LOG2_SCALE = ATTN_SCALE * 1.4426950408889634
IDX_SCALE = IDX_DIM ** -0.5
IDX_W_SCALE = IDX_HEADS ** -0.5

LANES = 128
SUBLANES = 8
MXU_DIM = 256
VMEM_LIMIT = 56 * 1024 * 1024

INT_MIN = -2 ** 31
FLT_LOWEST = -3.4028234663852886e38
NEG_BIG = -1e30
KEY_CHUNK = 512
PAGES_PER_STEP = 16
IDX_PAGES_PER_STEP = 16


def _cparams(semantics):
    return pltpu.CompilerParams(dimension_semantics=semantics, vmem_limit_bytes=VMEM_LIMIT)


def _layer_norm(y, g, b):
    mu = jnp.mean(y, axis=-1, keepdims=True)
    d = y - mu
    var = jnp.mean(d * d, axis=-1, keepdims=True)
    return d * lax.rsqrt(var + LN_EPS) * g + b


def _dot(a, b):
    return jnp.dot(a, b, preferred_element_type=F32)


def _dot_nt(a, b):
    return lax.dot_general(a, b, (((1,), (1,)), ((), ())), preferred_element_type=F32)


def _log2(n):
    assert n > 0 and n & (n - 1) == 0, n
    return n.bit_length() - 1


def _ffn_kernel(x_ref, wg_ref, *rest, alpha, tail, n_up):
    wu_refs = rest[:n_up]
    wd_ref, g_ref, b_ref, o_ref, xb_ref = rest[n_up:]
    j = pl.program_id(1)
    tf = wg_ref.shape[1]
    up_w = wu_refs[0].shape[1]

    @pl.when(j == 0)
    def _():
        xb_ref[...] = x_ref[...].astype(BF16)
        o_ref[...] = jnp.zeros_like(o_ref)

    last = pl.num_programs(1) - 1

    def chunk(width):
        xb = xb_ref[...]
        gate = _dot(xb, wg_ref[:, 0:width].astype(BF16))
        w_up = [r[...].astype(BF16) for r in wu_refs[:width // up_w]]
        up = _dot(xb, w_up[0] if len(w_up) == 1 else jnp.concatenate(w_up, axis=1))
        act = (gate * jax.nn.sigmoid(gate) * up).astype(BF16)
        o_ref[...] += _dot(act, wd_ref[0:width, :].astype(BF16))

    if tail == tf:
        chunk(tf)
    else:
        pl.when(j < last)(lambda: chunk(tf))
        pl.when(j == last)(lambda: chunk(tail))

    @pl.when(j == last)
    def _():
        y = alpha * x_ref[...] + 0.5 * o_ref[...]
        o_ref[...] = _layer_norm(y, g_ref[...], b_ref[...])


def _ffn_ln(x, w_gu, w_down, g, b, *, alpha, tm, tf):
    rows, d = x.shape
    d_ff = w_down.shape[0]
    nj = pl.cdiv(d_ff, tf)
    up_w = math.gcd(d_ff, tf)
    n_up = tf // up_w
    last_up = 2 * d_ff // up_w - 1
    up_specs = [
        pl.BlockSpec((d, up_w), functools.partial(
            lambda i, j, p: (0, jnp.minimum((d_ff + j * tf) // up_w + p, last_up)), p=p))
        for p in range(n_up)
    ]
    return pl.pallas_call(
        functools.partial(_ffn_kernel, alpha=alpha, tail=d_ff - (nj - 1) * tf, n_up=n_up),
        grid=(rows // tm, nj),
        in_specs=[
            pl.BlockSpec((tm, d), lambda i, j: (i, 0)),
            pl.BlockSpec((d, tf), lambda i, j: (0, j)),
        ] + up_specs + [
            pl.BlockSpec((tf, d), lambda i, j: (j, 0)),
            pl.BlockSpec((1, d), lambda i, j: (0, 0)),
            pl.BlockSpec((1, d), lambda i, j: (0, 0)),
        ],
        out_specs=pl.BlockSpec((tm, d), lambda i, j: (i, 0)),
        out_shape=jax.ShapeDtypeStruct((rows, d), F32),
        scratch_shapes=[pltpu.VMEM((tm, d), BF16)],
        compiler_params=_cparams(("parallel", "arbitrary")),
        name="ffn_ln",
    )(x, w_gu, *([w_gu] * n_up), w_down, g, b)


def _proj_in_kernel(x_ref, w_ref, xr_ref, gr_ref, q_ref, k_ref, v_ref, kb_ref, vb_ref, qi_ref, kw_ref,
                    *, d_rnn, d_q, d_kv, d_qi):
    xb = x_ref[...].astype(BF16)
    off = [0]

    def seg(width):
        lo = off[0]
        off[0] = lo + width
        return _dot(xb, w_ref[:, lo:lo + width])

    xr_ref[...] = seg(d_rnn)
    gr_ref[...] = seg(d_rnn)
    q_ref[...] = seg(d_q).astype(BF16)
    for full_ref, bf16_ref in ((k_ref, kb_ref), (v_ref, vb_ref)):
        kv = seg(d_kv)
        bf16_ref[...] = kv.astype(BF16)
        for n in range(N_KV_HEADS):
            full_ref[:, n, :] = kv[:, n * HEAD_DIM:(n + 1) * HEAD_DIM]
    qi_ref[...] = seg(d_qi).astype(BF16)
    kw_ref[...] = seg(LANES)


def _proj_in(x, w, *, tm, d_rnn, d_q, d_kv, d_qi):
    rows, d = x.shape
    heads = (d_kv // HEAD_DIM, HEAD_DIM)
    widths = ((d_rnn,), (d_rnn,), (d_q,), heads, heads, (d_kv,), (d_kv,), (d_qi,), (LANES,))
    dtypes = (F32, F32, BF16, F32, F32, BF16, BF16, BF16, F32)
    return pl.pallas_call(
        functools.partial(_proj_in_kernel, d_rnn=d_rnn, d_q=d_q, d_kv=d_kv, d_qi=d_qi),
        grid=(rows // tm,),
        in_specs=[
            pl.BlockSpec((tm, d), lambda i: (i, 0)),
            pl.BlockSpec(w.shape, lambda i: (0, 0), pipeline_mode=pl.Buffered(1)),
        ],
        out_specs=[pl.BlockSpec((tm,) + wd, lambda i, nd=len(wd): (i,) + (0,) * nd) for wd in widths],
        out_shape=[jax.ShapeDtypeStruct((rows,) + wd, dt) for wd, dt in zip(widths, dtypes)],
        compiler_params=_cparams(("parallel",)),
        name="proj_in",
    )(x, w)


def _softplus(z):
    return jnp.maximum(z, 0.0) + jnp.log1p(jnp.exp(-jnp.abs(z)))


def _lru_gates(xc, wa_ref, ba, wi_ref, bi, lam):
    bw = xc.shape[1] // LRU_BLOCKS
    sp = _softplus(-lam)
    a_parts, u_parts = [], []
    for n in range(LRU_BLOCKS):
        sl = slice(n * bw, (n + 1) * bw)
        xn = xc[:, sl]
        xb = xn.astype(BF16)
        r = jax.nn.sigmoid(_dot(xb, wa_ref[n]) + ba[:, sl])
        i = jax.nn.sigmoid(_dot(xb, wi_ref[n]) + bi[:, sl])
        log_a = -LRU_C * r * sp[:, sl]
        a = jnp.exp(log_a)
        a_parts.append(a)
        u_parts.append(jnp.sqrt(-jnp.tanh(log_a) * (a * a + 1.0)) * i * xn)
    return jnp.concatenate(a_parts, axis=1), jnp.concatenate(u_parts, axis=1)


def _lru_prompt_kernel(xr_ref, gr_ref, cw_ref, cb_ref, wa_ref, ba_ref, wi_ref, bi_ref, lam_ref,
                       o_ref, hl_ref, xp_ref, a_ref, u_ref, hs_ref, hc_ref, *, tt):
    j = pl.program_id(1)
    d = xr_ref.shape[1]

    @pl.when(j == 0)
    def _():
        xp_ref[0:SUBLANES, :] = jnp.zeros((SUBLANES, d), F32)
        hc_ref[...] = jnp.zeros_like(hc_ref)

    @pl.when(j > 0)
    def _():
        xp_ref[0:SUBLANES, :] = xp_ref[tt:tt + SUBLANES, :]

    xp_ref[SUBLANES:SUBLANES + tt, :] = xr_ref[...]
    cw = cw_ref[...]
    xc = cb_ref[...]
    for jj in range(CONV_W):
        lo = SUBLANES - (CONV_W - 1) + jj
        xc = xc + cw[jj:jj + 1, :] * xp_ref[lo:lo + tt, :]

    a, u = _lru_gates(xc, wa_ref, ba_ref[...], wi_ref, bi_ref[...], lam_ref[...])
    a_ref[...] = a
    u_ref[...] = u

    row = lax.broadcasted_iota(I32, (SUBLANES, d), 0)

    def group(g, h):
        r0 = pl.multiple_of(g * SUBLANES, SUBLANES)
        a8 = a_ref[pl.ds(r0, SUBLANES), :]
        u8 = u_ref[pl.ds(r0, SUBLANES), :]
        out = jnp.zeros((SUBLANES, d), F32)
        for jj in range(SUBLANES):
            aj = jnp.broadcast_to(a8[jj:jj + 1, :], (SUBLANES, d))
            uj = jnp.broadcast_to(u8[jj:jj + 1, :], (SUBLANES, d))
            h = aj * h + uj
            out = jnp.where(row == jj, h, out)
        hs_ref[pl.ds(r0, SUBLANES), :] = out
        return h

    h = lax.fori_loop(0, tt // SUBLANES, group, hc_ref[...])
    hc_ref[...] = h
    hl_ref[...] = h[0:1, :]
    o_ref[...] = (hs_ref[...] * jax.nn.gelu(gr_ref[...])).astype(BF16)


def _lru_prompt(xr, gr, cw, cb, wa, ba, wi, bi, lam, *, batch, tt):
    rows, d = xr.shape
    seq = rows // batch
    nt = seq // tt
    row_spec = pl.BlockSpec((tt, d), lambda b, j: (b * nt + j, 0))
    vec_spec = pl.BlockSpec((1, d), lambda b, j: (0, 0))
    w_spec = pl.BlockSpec(wa.shape, lambda b, j: (0, 0, 0))
    return pl.pallas_call(
        functools.partial(_lru_prompt_kernel, tt=tt),
        grid=(batch, nt),
        in_specs=[row_spec, row_spec, pl.BlockSpec((CONV_W, d), lambda b, j: (0, 0)), vec_spec,
                  w_spec, vec_spec, w_spec, vec_spec, vec_spec],
        out_specs=[row_spec, pl.BlockSpec((None, 1, d), lambda b, j: (b, 0, 0))],
        out_shape=[jax.ShapeDtypeStruct((rows, d), BF16), jax.ShapeDtypeStruct((batch, 1, d), F32)],
        scratch_shapes=[pltpu.VMEM((tt + SUBLANES, d), F32), pltpu.VMEM((tt, d), F32),
                        pltpu.VMEM((tt, d), F32), pltpu.VMEM((tt, d), F32), pltpu.VMEM((SUBLANES, d), F32)],
        compiler_params=_cparams(("parallel", "arbitrary")),
        name="lru_prompt",
    )(xr, gr, cw, cb, wa, ba, wi, bi, lam)


def _lru_sample_kernel(xp_ref, gr_ref, h0_ref, cw_ref, cb_ref, wa_ref, ba_ref, wi_ref, bi_ref, lam_ref,
                       o_ref, hl_ref, *, nb, nt):
    cw = cw_ref[...]
    xc = cb_ref[...] + cw[0:1, :] * xp_ref[0:nt * nb, :]
    for jj in range(1, CONV_W):
        xc = xc + cw[jj:jj + 1, :] * xp_ref[jj * nb:(jj + nt) * nb, :]
    a, u = _lru_gates(xc, wa_ref, ba_ref[...], wi_ref, bi_ref[...], lam_ref[...])
    gate = jax.nn.gelu(gr_ref[...])
    h = h0_ref[...]
    for t in range(nt):
        sl = slice(t * nb, (t + 1) * nb)
        h = a[sl, :] * h + u[sl, :]
        o_ref[sl, :] = (h * gate[sl, :]).astype(BF16)
    hl_ref[...] = h


def _lru_sample(xp, gr, h0, cw, cb, wa, ba, wi, bi, lam, *, nb, nt):
    d = gr.shape[1]
    return pl.pallas_call(
        functools.partial(_lru_sample_kernel, nb=nb, nt=nt),
        out_shape=[jax.ShapeDtypeStruct((nt * nb, d), BF16), jax.ShapeDtypeStruct((nb, d), F32)],
        compiler_params=pltpu.CompilerParams(vmem_limit_bytes=VMEM_LIMIT),
        name="lru_sample",
    )(xp, gr, h0, cw, cb, wa, ba, wi, bi, lam)


def _rank_to_float(u):
    key = u ^ jnp.int32(INT_MIN)
    return lax.bitcast_convert_type(key ^ ((key >> 31) & jnp.int32(0x7FFFFFFF)), F32)


def _count(sc_ref, nc, rows, ck, thr, strict):
    accs = []
    for r0 in range(0, rows, COUNT_ROWS):
        nr = min(COUNT_ROWS, rows - r0)
        thr_b = jnp.broadcast_to(thr[r0:r0 + nr], (nr, LANES))

        def body(c, acc, r0=r0, nr=nr, thr_b=thr_b):
            for t in range(ck // LANES):
                tile = sc_ref[c, r0:r0 + nr, t * LANES:(t + 1) * LANES]
                hit = (tile > thr_b) if strict else (tile >= thr_b)
                acc = acc + jnp.where(hit, 1.0, 0.0)
            return acc

        accs.append(lax.fori_loop(0, nc, body, jnp.zeros((nr, LANES), F32)))
    acc = accs[0] if len(accs) == 1 else jnp.concatenate(accs, axis=0)
    return jnp.sum(acc, axis=1, keepdims=True)


def _select_topk(sc_ref, tau_ref, nc, rows, ck, k_sel):
    kf = float(k_sel)

    def bit_body(it, carry):
        u, n_ge = carry
        cand = u | jnp.left_shift(jnp.int32(1), 31 - it)
        cnt = _count(sc_ref, nc, rows, ck, _rank_to_float(cand), strict=False)
        take = cnt >= kf
        return jnp.where(take, cand, u), jnp.where(take, cnt, n_ge)

    u, n_ge = lax.fori_loop(0, 32, bit_body, (jnp.zeros((rows, 1), I32), jnp.full((rows, 1), kf, F32)))
    tau = _rank_to_float(u)
    tau_ref[...] = jnp.broadcast_to(tau, (rows, LANES))

    @pl.when(jnp.max(n_ge) > kf)
    def _():
        before = (lax.broadcasted_iota(I32, (ck, ck), 0) < lax.broadcasted_iota(I32, (ck, ck), 1))
        before = jnp.where(before, 1.0, 0.0).astype(BF16)
        tau_w = jnp.broadcast_to(tau, (rows, ck))
        need = kf - _count(sc_ref, nc, rows, ck, tau, strict=True)
        need_w = jnp.broadcast_to(need, (rows, ck))

        def body(c, seen):
            sc = sc_ref[c]
            eq = sc == tau_w
            eqf = jnp.where(eq, 1.0, 0.0)
            rank = _dot(eqf.astype(BF16), before) + seen
            sc_ref[c] = jnp.where(eq, jnp.where(rank >= need_w, -jnp.inf, sc), sc)
            return seen + jnp.sum(eqf, axis=1, keepdims=True)

        lax.fori_loop(0, nc, body, jnp.zeros((rows, 1), F32))


def _prompt_attn_kernel(q_ref, qi_ref, kwq_ref, kwk_ref, kb_ref, vb_ref, o_ref,
                        ki_ref, sc_ref, tau_ref, qs_ref, m_ref, l_ref, acc_ref, *, k_sel, qb):
    i = pl.program_id(1)
    ck = sc_ref.shape[2]
    nc = (i * qb + qb + ck - 1) // ck

    @pl.when(i == 0)
    def _():
        ki_ref[...] = kwk_ref[:, 0:IDX_DIM].astype(BF16)

    w_idx = kwq_ref[:, IDX_DIM:IDX_DIM + IDX_HEADS] * IDX_W_SCALE * IDX_SCALE
    t_pos = i * qb + lax.broadcasted_iota(I32, (qb, ck), 0)
    col = lax.broadcasted_iota(I32, (qb, ck), 1)

    def score_body(c, carry):
        c0 = pl.multiple_of(c * ck, ck)
        kc = ki_ref[pl.ds(c0, ck), :]
        score = jnp.zeros((qb, ck), F32)
        for h in range(IDX_HEADS):
            s = _dot_nt(qi_ref[:, h * IDX_DIM:(h + 1) * IDX_DIM], kc)
            score = score + jnp.maximum(s, 0.0) * w_idx[:, h:h + 1]
        sc_ref[c] = jnp.where(col + c0 <= t_pos, score, -jnp.inf)
        return carry

    lax.fori_loop(0, nc, score_body, 0)

    need_select = (i + 1) * qb > k_sel

    @pl.when(jnp.logical_not(need_select))
    def _():
        tau_ref[...] = jnp.full(tau_ref.shape, FLT_LOWEST, F32)

    @pl.when(need_select)
    def _():
        _select_topk(sc_ref, tau_ref, nc, qb, ck, k_sel)
        row_pos = i * qb + lax.broadcasted_iota(I32, (qb, LANES), 0)
        tau_ref[...] = jnp.where(row_pos < k_sel, FLT_LOWEST, tau_ref[...])

    for n in range(N_KV_HEADS):
        for g in range(GQA_GROUP):
            hq = n * GQA_GROUP + g
            qs_ref[n, g * qb:(g + 1) * qb, :] = q_ref[:, hq * HEAD_DIM:(hq + 1) * HEAD_DIM]
    m_ref[...] = jnp.full(m_ref.shape, NEG_BIG, F32)
    l_ref[...] = jnp.zeros_like(l_ref)
    acc_ref[...] = jnp.zeros_like(acc_ref)
    tau_w = jnp.broadcast_to(tau_ref[:, 0:1], (qb, ck))

    def attn_body(c, carry):
        c0 = pl.multiple_of(c * ck, ck)
        bias = jnp.where(sc_ref[c] >= tau_w, 0.0, NEG_BIG)
        bias = jnp.concatenate([bias] * GQA_GROUP, axis=0)
        for n in range(N_KV_HEADS):
            kn = kb_ref[pl.ds(c0, ck), n * HEAD_DIM:(n + 1) * HEAD_DIM]
            vn = vb_ref[pl.ds(c0, ck), n * HEAD_DIM:(n + 1) * HEAD_DIM]
            s = _dot_nt(qs_ref[n], kn) + bias
            tiles = [s[:, t * LANES:(t + 1) * LANES] for t in range(ck // LANES)]
            m_prev = m_ref[n]
            m_new = jnp.maximum(m_prev, jnp.max(functools.reduce(jnp.maximum, tiles), axis=1, keepdims=True))
            alpha = jnp.exp2((m_prev - m_new) * SOFTMAX_LOG2_SCALE)
            p_tiles = [jnp.exp2((t - m_new) * SOFTMAX_LOG2_SCALE) for t in tiles]
            l_ref[n] = alpha * l_ref[n] + functools.reduce(jnp.add, p_tiles)
            p = jnp.concatenate(p_tiles, axis=1).astype(BF16)
            acc_ref[n] = alpha * acc_ref[n] + _dot(p, vn)
            m_ref[n] = m_new
        return carry

    lax.fori_loop(0, nc, attn_body, 0)

    for n in range(N_KV_HEADS):
        o = acc_ref[n] / jnp.sum(l_ref[n], axis=1, keepdims=True)
        for g in range(GQA_GROUP):
            hq = n * GQA_GROUP + g
            o_ref[:, hq * HEAD_DIM:(hq + 1) * HEAD_DIM] = o[g * qb:(g + 1) * qb, :].astype(BF16)


def _prompt_attn(q, qi, kw, kb, vb, *, batch):
    rows = q.shape[0]
    seq = rows // batch
    ck = min(KEY_CHUNK, seq)
    k_sel = min(TOPK_MAX, seq // 4)
    qb = next(r for r in QUERY_ROWS if seq % r == 0)
    nblk = seq // qb
    gq = GQA_GROUP * qb
    blk = lambda b, i: (b * nblk + i, 0)
    per_b = lambda b, i: (b, 0)
    return pl.pallas_call(
        functools.partial(_prompt_attn_kernel, k_sel=k_sel, qb=qb),
        grid=(batch, nblk),
        in_specs=[
            pl.BlockSpec((qb, q.shape[1]), blk),
            pl.BlockSpec((qb, qi.shape[1]), blk),
            pl.BlockSpec((qb, LANES), blk),
            pl.BlockSpec((seq, LANES), per_b),
            pl.BlockSpec((seq, kb.shape[1]), per_b),
            pl.BlockSpec((seq, vb.shape[1]), per_b),
        ],
        out_specs=pl.BlockSpec((qb, q.shape[1]), blk),
        out_shape=jax.ShapeDtypeStruct(q.shape, BF16),
        scratch_shapes=[
            pltpu.VMEM((seq, IDX_DIM), BF16),
            pltpu.VMEM((seq // ck, qb, ck), F32),
            pltpu.VMEM((qb, LANES), F32),
            pltpu.VMEM((N_KV_HEADS, gq, HEAD_DIM), BF16),
            pltpu.VMEM((N_KV_HEADS, gq, LANES), F32),
            pltpu.VMEM((N_KV_HEADS, gq, LANES), F32),
            pltpu.VMEM((N_KV_HEADS, gq, HEAD_DIM), F32),
        ],
        compiler_params=_cparams(("parallel", "arbitrary")),
        name="prompt_attn",
    )(q, qi, kw, kw, kb, vb)


def _page_copy(cache_ref, layer, page, buf_ref, sem_ref, slot, p):
    return pltpu.make_async_copy(cache_ref.at[layer, page], buf_ref.at[slot, p], sem_ref.at[slot])


def _prefetch_pages(pt_ref, npg, streams):
    b, c = pl.program_id(0), pl.program_id(1)
    n_c = pl.num_programs(1)
    step = b * n_c + c
    slot = lax.rem(step, 2)

    def start(bb, cc, sl):
        for cache_ref, layer, buf_ref, sem_ref in streams:
            for p in range(npg):
                _page_copy(cache_ref, layer, pt_ref[bb, cc * npg + p], buf_ref, sem_ref, sl, p).start()

    @pl.when(step == 0)
    def _():
        start(b, c, slot)

    @pl.when(step + 1 < pl.num_programs(0) * n_c)
    def _():
        wrap = c + 1 == n_c
        start(jnp.where(wrap, b + 1, b), jnp.where(wrap, 0, c + 1), 1 - slot)

    for cache_ref, layer, buf_ref, sem_ref in streams:
        for p in range(npg):
            _page_copy(cache_ref, layer, 0, buf_ref, sem_ref, slot, p).wait()
    return slot


def _sample_scores_kernel(pt_ref, qi_ref, w_ref, kin_ref, cache_ref, past_ref, new_ref, buf_ref, kc_ref, sem_ref,
                          *, layer, nt, npg):
    slot = _prefetch_pages(pt_ref, npg, [(cache_ref, layer, buf_ref, sem_ref)])
    c = pl.program_id(1)
    w = w_ref[...] * IDX_W_SCALE * IDX_SCALE
    qi = qi_ref[...]

    def scores(s):
        width = s.shape[1]
        s = jnp.maximum(s, 0.0) * jnp.concatenate([w] * (width // LANES), axis=1)
        return jnp.sum(s.reshape(nt, IDX_HEADS, width), axis=1)

    for p in range(npg):
        kc_ref[:, p * PAGE_SIZE:(p + 1) * PAGE_SIZE] = buf_ref[slot, p].astype(BF16)
    past_ref[...] = scores(_dot(qi, kc_ref[...]))

    @pl.when(c == 0)
    def _():
        new_ref[...] = scores(_dot_nt(qi, kin_ref[...].astype(BF16)))


def _sample_scores(page_table, qi_b, w_b, ki_new, cache_ki, *, layer, nt):
    nb, n_pages = page_table.shape
    npg = min(IDX_PAGES_PER_STEP, n_pages)
    rows = nt * IDX_HEADS
    grid_spec = pltpu.PrefetchScalarGridSpec(
        num_scalar_prefetch=1,
        grid=(nb, n_pages // npg),
        in_specs=[
            pl.BlockSpec((None, rows, IDX_DIM), lambda b, c, pt: (b, 0, 0)),
            pl.BlockSpec((None, rows, LANES), lambda b, c, pt: (b, 0, 0)),
            pl.BlockSpec((None, PAGE_SIZE, IDX_DIM), lambda b, c, pt: (b, 0, 0)),
            pl.BlockSpec(memory_space=pl.ANY),
        ],
        out_specs=[
            pl.BlockSpec((None, nt, npg * PAGE_SIZE), lambda b, c, pt: (b, 0, c)),
            pl.BlockSpec((None, nt, PAGE_SIZE), lambda b, c, pt: (b, 0, 0)),
        ],
        scratch_shapes=[pltpu.VMEM((2, npg, IDX_DIM, PAGE_SIZE), F32), pltpu.VMEM((IDX_DIM, npg * PAGE_SIZE), BF16),
                        pltpu.SemaphoreType.DMA((2,))],
    )
    return pl.pallas_call(
        functools.partial(_sample_scores_kernel, layer=layer, nt=nt, npg=npg),
        grid_spec=grid_spec,
        out_shape=[jax.ShapeDtypeStruct((nb, nt, n_pages * PAGE_SIZE), F32),
                   jax.ShapeDtypeStruct((nb, nt, PAGE_SIZE), F32)],
        compiler_params=_cparams(("arbitrary", "arbitrary")),
        name="sample_scores",
    )(page_table, qi_b, w_b, ki_new, cache_ki)


def _sample_select_kernel(past_ref, new_ref, bias_ref, biasn_ref, sc_ref, tau_ref, *, nt, k_sel, ck):
    rows, n_past = past_ref.shape
    npc = n_past // ck
    for c in range(npc):
        sc_ref[c] = past_ref[:, c * ck:(c + 1) * ck]
    t_row = lax.broadcasted_iota(I32, (rows, ck), 0) & (nt - 1)
    col = lax.broadcasted_iota(I32, (rows, ck), 1)
    new_sc = jnp.concatenate([new_ref[...]] * (ck // PAGE_SIZE), axis=1)
    sc_ref[npc] = jnp.where(col <= t_row, new_sc, -jnp.inf)
    _select_topk(sc_ref, tau_ref, npc + 1, rows, ck, k_sel)
    tau_b = tau_ref[...]
    d = (lax.broadcasted_iota(I32, (LANES, LANES * N_KV_HEADS), 1)
         - N_KV_HEADS * lax.broadcasted_iota(I32, (LANES, LANES * N_KV_HEADS), 0))
    expand = jnp.where(d >= 0, jnp.where(d < N_KV_HEADS, 1.0, 0.0), 0.0).astype(BF16)
    def bias_of(c, t):
        sel = jnp.where(sc_ref[c][:, t * LANES:(t + 1) * LANES] >= tau_b, 1.0, 0.0).astype(BF16)
        return (_dot(sel, expand) - 1.0) * (-NEG_BIG)

    for c in range(npc):
        for t in range(ck // LANES):
            lo = (c * ck + t * LANES) * N_KV_HEADS
            bias_ref[:, lo:lo + LANES * N_KV_HEADS] = bias_of(c, t)
    biasn_ref[...] = bias_of(npc, 0)[:, 0:LANES]


def _sample_select(past, new, *, nt):
    rows, n_past = past.shape
    assert nt & (nt - 1) == 0 and nt * N_KV_HEADS <= LANES
    ck = min(KEY_CHUNK, n_past)
    k_sel = min(TOPK_MAX, (n_past + nt) // 4)
    return pl.pallas_call(
        functools.partial(_sample_select_kernel, nt=nt, k_sel=k_sel, ck=ck),
        out_shape=[jax.ShapeDtypeStruct((rows, n_past * N_KV_HEADS), F32), jax.ShapeDtypeStruct((rows, LANES), F32)],
        scratch_shapes=[pltpu.VMEM((n_past // ck + 1, rows, ck), F32), pltpu.VMEM((rows, LANES), F32)],
        compiler_params=pltpu.CompilerParams(vmem_limit_bytes=VMEM_LIMIT),
        name="sample_select",
    )(past, new)


def _sample_attn_kernel(pt_ref, q_ref, bias_ref, biasn_ref, kn_ref, vn_ref, ck_ref, cv_ref, o_ref,
                        kbuf_ref, vbuf_ref, kc_ref, vc_ref, m_ref, l_ref, acc_ref, ksem_ref, vsem_ref, *, layer, nt, npg):
    slot = _prefetch_pages(pt_ref, npg, [(ck_ref, layer, kbuf_ref, ksem_ref), (cv_ref, layer, vbuf_ref, vsem_ref)])
    c = pl.program_id(1)
    rows = q_ref.shape[0]
    page_rows = PAGE_SIZE * N_KV_HEADS

    @pl.when(c == 0)
    def _():
        m_ref[...] = jnp.full(m_ref.shape, NEG_BIG, F32)
        l_ref[...] = jnp.zeros_like(l_ref)
        acc_ref[...] = jnp.zeros_like(acc_ref)

    row_head = lax.shift_right_logical(lax.broadcasted_iota(I32, (rows, page_rows), 0), _log2(rows // N_KV_HEADS))
    col_head = lax.broadcasted_iota(I32, (rows, page_rows), 1) & (N_KV_HEADS - 1)
    head_bias = jnp.where(row_head == col_head, 0.0, NEG_BIG)
    first_of_pair = pl.program_id(0) % 2 == 0

    def update(k2, v2, bias_blk):
        s = _dot_nt(q_ref[...], k2)
        bias = jnp.where(first_of_pair, bias_blk[0:nt], bias_blk[nt:2 * nt])
        bias = jnp.concatenate([bias] * (rows // nt), axis=0)
        lanes = lambda a, t: a[:, t * LANES:(t + 1) * LANES]
        per_page = page_rows // LANES
        tiles = [lanes(s, t) + (lanes(bias, t) + lanes(head_bias, t % per_page)) for t in range(k2.shape[0] // LANES)]
        m_prev = m_ref[...]
        m_new = jnp.maximum(m_prev, jnp.max(functools.reduce(jnp.maximum, tiles), axis=1, keepdims=True))
        alpha = jnp.exp2((m_prev - m_new) * SOFTMAX_LOG2_SCALE)
        p_tiles = [jnp.exp2((t - m_new) * SOFTMAX_LOG2_SCALE) for t in tiles]
        l_ref[...] = alpha * l_ref[...] + functools.reduce(jnp.add, p_tiles)
        acc_ref[...] = alpha * acc_ref[...] + _dot(jnp.concatenate(p_tiles, axis=1).astype(BF16), v2)
        m_ref[...] = m_new

    for p in range(npg):
        kc_ref[p * page_rows:(p + 1) * page_rows, :] = kbuf_ref[slot, p].reshape(page_rows, HEAD_DIM).astype(BF16)
        vc_ref[p * page_rows:(p + 1) * page_rows, :] = vbuf_ref[slot, p].reshape(page_rows, HEAD_DIM).astype(BF16)
    update(kc_ref[...], vc_ref[...], bias_ref[...])

    @pl.when(c == pl.num_programs(1) - 1)
    def _():
        def new_rows(ref):
            pad = jnp.zeros((LANES - ref.shape[0], HEAD_DIM), F32)
            return jnp.concatenate([ref[...], pad], axis=0).astype(BF16)

        update(new_rows(kn_ref), new_rows(vn_ref), biasn_ref[...])
        o_ref[...] = acc_ref[...] / jnp.sum(l_ref[...], axis=1, keepdims=True)


def _sample_attn(page_table, q_b, bias, bias_new, k_new, v_new, cache_k, cache_v, *, layer, nt):
    nb, n_pages = page_table.shape
    npg = min(PAGES_PER_STEP, n_pages)
    rows = q_b.shape[1]
    page_rows = PAGE_SIZE * N_KV_HEADS
    assert N_KV_HEADS & (N_KV_HEADS - 1) == 0 and 2 * nt == SUBLANES and nb % 2 == 0

    per_b3 = lambda b, c, pt: (b, 0, 0)
    grid_spec = pltpu.PrefetchScalarGridSpec(
        num_scalar_prefetch=1,
        grid=(nb, n_pages // npg),
        in_specs=[
            pl.BlockSpec((None, rows, HEAD_DIM), per_b3),
            pl.BlockSpec((2 * nt, npg * page_rows), lambda b, c, pt: (b // 2, c)),
            pl.BlockSpec((2 * nt, LANES), lambda b, c, pt: (b // 2, 0)),
            pl.BlockSpec((None, nt * N_KV_HEADS, HEAD_DIM), per_b3),
            pl.BlockSpec((None, nt * N_KV_HEADS, HEAD_DIM), per_b3),
            pl.BlockSpec(memory_space=pl.ANY),
            pl.BlockSpec(memory_space=pl.ANY),
        ],
        out_specs=pl.BlockSpec((None, rows, HEAD_DIM), per_b3),
        scratch_shapes=[
            pltpu.VMEM((2, npg, PAGE_SIZE, N_KV_HEADS, HEAD_DIM), F32),
            pltpu.VMEM((2, npg, PAGE_SIZE, N_KV_HEADS, HEAD_DIM), F32),
            pltpu.VMEM((npg * page_rows, HEAD_DIM), BF16),
            pltpu.VMEM((npg * page_rows, HEAD_DIM), BF16),
            pltpu.VMEM((rows, LANES), F32),
            pltpu.VMEM((rows, LANES), F32),
            pltpu.VMEM((rows, HEAD_DIM), F32),
            pltpu.SemaphoreType.DMA((2,)),
            pltpu.SemaphoreType.DMA((2,)),
        ],
    )
    return pl.pallas_call(
        functools.partial(_sample_attn_kernel, layer=layer, nt=nt, npg=npg),
        grid_spec=grid_spec,
        out_shape=jax.ShapeDtypeStruct((nb, rows, HEAD_DIM), F32),
        compiler_params=_cparams(("arbitrary", "arbitrary")),
        name="sample_attn",
    )(page_table, q_b, bias, bias_new, k_new, v_new, cache_k, cache_v)


def _proj_out_kernel(x_ref, r_ref, a_ref, w_ref, g_ref, b_ref, o_ref, *, alpha):
    d_rnn = r_ref.shape[1]
    mix = _dot(r_ref[...], w_ref[0:d_rnn, :]) + _dot(a_ref[...], w_ref[d_rnn:, :])
    o_ref[...] = _layer_norm(alpha * x_ref[...] + mix, g_ref[...], b_ref[...])


def _proj_out_ln(x, rnn, attn, w, g, b, *, alpha, tm):
    rows, d = x.shape
    return pl.pallas_call(
        functools.partial(_proj_out_kernel, alpha=alpha),
        grid=(rows // tm,),
        in_specs=[
            pl.BlockSpec((tm, d), lambda i: (i, 0)),
            pl.BlockSpec((tm, rnn.shape[1]), lambda i: (i, 0)),
            pl.BlockSpec((tm, attn.shape[1]), lambda i: (i, 0)),
            pl.BlockSpec(w.shape, lambda i: (0, 0)),
            pl.BlockSpec((1, d), lambda i: (0, 0)),
            pl.BlockSpec((1, d), lambda i: (0, 0)),
        ],
        out_specs=pl.BlockSpec((tm, d), lambda i: (i, 0)),
        out_shape=jax.ShapeDtypeStruct((rows, d), F32),
        compiler_params=_cparams(("parallel",)),
        name="proj_out_ln",
    )(x, rnn, attn, w, g, b)


def _row_tile(rows, target):
    tm = min(rows, target)
    while rows % tm:
        tm //= 2
    return tm


def _pad_to(a, axis, size):
    pad = [(0, 0)] * a.ndim
    pad[axis] = (0, size - a.shape[axis])
    return jnp.pad(a, pad)


def kernel(x_prompt, x_sample, cache_k, cache_v, cache_k_idx, state_conv, state_rnn, page_table, ln1_g, ln1_b, ffn1_w_gu, ffn1_w_down, w_in, conv_w, conv_b, lru_w_a, lru_b_a, lru_w_i, lru_b_i, lru_lambda, w_out, ln2_g, ln2_b, ffn2_w_gu, ffn2_w_down, ln3_g, ln3_b):
    depth = w_in.shape[0]
    bp, seq, d_model = x_prompt.shape
    nb, nt, _ = x_sample.shape
    d_rnn = conv_w.shape[2]
    d_q = N_Q_HEADS * HEAD_DIM
    d_kv = N_KV_HEADS * HEAD_DIM
    d_qi = IDX_HEADS * IDX_DIM
    alpha = (2.0 * depth) ** 0.25

    xp = x_prompt.reshape(bp * seq, d_model)
    xs = x_sample.transpose(1, 0, 2).reshape(nt * nb, d_model)
    tm_p = _row_tile(bp * seq, 1024)
    tm_r = _row_tile(bp * seq, 512)
    tm_s = _row_tile(nt * nb, 512)
    vec = lambda a: a.reshape(1, -1)

    outs = [[] for _ in range(10)]
    for l in range(depth):
        d_in = w_in.shape[2]
        w_in_b = _pad_to(w_in[l].astype(BF16), 1, d_in - IDX_DIM - IDX_HEADS + LANES)
        w_out_b = w_out[l].astype(BF16)
        wa_b = lru_w_a[l].astype(BF16)
        wi_b = lru_w_i[l].astype(BF16)
        lru_args = (conv_w[l], vec(conv_b[l]), wa_b, vec(lru_b_a[l]), wi_b, vec(lru_b_i[l]), vec(lru_lambda[l]))
        proj = functools.partial(_proj_in, d_rnn=d_rnn, d_q=d_q, d_kv=d_kv, d_qi=d_qi)

        x1p = _ffn_ln(xp, ffn1_w_gu[l], ffn1_w_down[l], vec(ln1_g[l]), vec(ln1_b[l]), alpha=alpha, tm=tm_p, tf=MXU_DIM)
        x1s = _ffn_ln(xs, ffn1_w_gu[l], ffn1_w_down[l], vec(ln1_g[l]), vec(ln1_b[l]), alpha=alpha, tm=tm_s, tf=MXU_DIM)
        xr_p, gr_p, q_p, k_p, v_p, kb_p, vb_p, qi_p, kw_p = proj(x1p, w_in_b, tm=_row_tile(bp * seq, 256))
        xr_s, gr_s, q_s, k_s, v_s, _, _, qi_s, kw_s = proj(x1s, w_in_b, tm=_row_tile(nt * nb, 256))

        rnn_p, hl_p = _lru_prompt(xr_p, gr_p, *lru_args, batch=bp, tt=_row_tile(seq, 512))
        conv_tm = state_conv[l].transpose(1, 0, 2).reshape((CONV_W - 1) * nb, d_rnn)
        xpad_s = jnp.concatenate([conv_tm, xr_s], axis=0)
        rnn_s, hl_s = _lru_sample(xpad_s, gr_s, state_rnn[l], *lru_args, nb=nb, nt=nt)

        attn_p = _prompt_attn(q_p, qi_p, kw_p, kb_p, vb_p, batch=bp)

        to_b = lambda a: a.reshape(nt, nb, -1).transpose(1, 0, 2)
        qi_b = to_b(qi_s).reshape(nb, nt * IDX_HEADS, IDX_DIM)
        kw_b = to_b(kw_s)
        w_b = jnp.broadcast_to(kw_b[:, :, IDX_DIM:IDX_DIM + IDX_HEADS].reshape(nb, nt * IDX_HEADS, 1),
                               (nb, nt * IDX_HEADS, LANES))
        ki_new = _pad_to(kw_b[:, :, :IDX_DIM], 1, PAGE_SIZE)
        cache_ki_t = jnp.swapaxes(cache_k_idx, 2, 3)
        past_sc, new_sc = _sample_scores(page_table, qi_b, w_b, ki_new, cache_ki_t, layer=l, nt=nt)
        n_past = past_sc.shape[2]
        bias, bias_new = _sample_select(past_sc.reshape(nb * nt, n_past), new_sc.reshape(nb * nt, PAGE_SIZE), nt=nt)
        q_b = to_b(q_s).reshape(nb, nt, N_KV_HEADS, GQA_GROUP, HEAD_DIM)
        q_b = q_b.transpose(0, 2, 3, 1, 4).reshape(nb, N_Q_HEADS * nt, HEAD_DIM)
        new_rows = lambda a: to_b(a).reshape(nb, nt * N_KV_HEADS, HEAD_DIM)
        o_b = _sample_attn(page_table, q_b, bias, bias_new, new_rows(k_s), new_rows(v_s), cache_k, cache_v,
                           layer=l, nt=nt)
        attn_s = o_b.reshape(nb, N_KV_HEADS, GQA_GROUP, nt, HEAD_DIM).transpose(3, 0, 1, 2, 4)
        attn_s = attn_s.reshape(nt * nb, d_q).astype(BF16)

        x2p = _proj_out_ln(x1p, rnn_p, attn_p, w_out_b, vec(ln2_g[l]), vec(ln2_b[l]), alpha=alpha, tm=tm_r)
        x2s = _proj_out_ln(x1s, rnn_s, attn_s, w_out_b, vec(ln2_g[l]), vec(ln2_b[l]), alpha=alpha, tm=tm_s)
        xp = _ffn_ln(x2p, ffn2_w_gu[l], ffn2_w_down[l], vec(ln3_g[l]), vec(ln3_b[l]), alpha=alpha, tm=tm_p, tf=MXU_DIM)
        xs = _ffn_ln(x2s, ffn2_w_gu[l], ffn2_w_down[l], vec(ln3_g[l]), vec(ln3_b[l]), alpha=alpha, tm=tm_s, tf=MXU_DIM)

        xr_p3 = xr_p.reshape(bp, seq, d_rnn)
        conv_p = xr_p3[:, -(CONV_W - 1):]
        conv_s = xpad_s.reshape(CONV_W - 1 + nt, nb, d_rnn)[-(CONV_W - 1):].transpose(1, 0, 2)
        layer_out = (
            k_p.reshape(bp, seq, N_KV_HEADS, HEAD_DIM), v_p.reshape(bp, seq, N_KV_HEADS, HEAD_DIM),
            kw_p[:, :IDX_DIM].reshape(bp, seq, IDX_DIM), conv_p, hl_p.reshape(bp, d_rnn),
            to_b(k_s).reshape(nb, nt, N_KV_HEADS, HEAD_DIM), to_b(v_s).reshape(nb, nt, N_KV_HEADS, HEAD_DIM),
            kw_b[:, :, :IDX_DIM], conv_s, hl_s,
        )
        for acc, val in zip(outs, layer_out):
            acc.append(val)

    y_p = xp.reshape(bp, seq, d_model)
    y_s = xs.reshape(nt, nb, d_model).transpose(1, 0, 2)
    return (y_p, y_s) + tuple(jnp.stack(o) for o in outs)
```

```python
import functools
import math

import jax
import jax.numpy as jnp
from jax import lax
from jax.experimental import pallas as pl
from jax.experimental.pallas import tpu as pltpu

F32 = jnp.float32
BF16 = jnp.bfloat16
I32 = jnp.int32

LRU_BLOCKS = 8
CONV_W = 4
LRU_C = 8.0
HEAD_DIM = 128
N_KV_HEADS = 4
GQA_GROUP = 2
N_Q_HEADS = N_KV_HEADS * GQA_GROUP
IDX_HEADS = 8
IDX_DIM = 64
TOPK_MAX = 256
QUERY_ROWS = (512, 256, 128)
COUNT_ROWS = 128
PAGE_SIZE = 128
LN_EPS = 1e-5
ATTN_SCALE = HEAD_DIM ** -0.5
SOFTMAX_LOG2_SCALE = ATTN_SCALE * 1.4426950408889634
IDX_SCALE = IDX_DIM ** -0.5
IDX_W_SCALE = IDX_HEADS ** -0.5

LANES = 128
SUBLANES = 8
MXU_DIM = 256
VMEM_LIMIT = 56 * 1024 * 1024
FFN_VMEM_LIMIT = 60 * 1024 * 1024

INT_MIN = -2 ** 31
FLT_LOWEST = -3.4028234663852886e38
NEG_BIG = -1e30
KEY_CHUNK = 512
PAGES_PER_STEP = 16
IDX_PAGES_PER_STEP = 16


def _cparams(semantics):
    return pltpu.CompilerParams(dimension_semantics=semantics, vmem_limit_bytes=VMEM_LIMIT)


def _layer_norm(y, g, b):
    mu = jnp.mean(y, axis=-1, keepdims=True)
    d = y - mu
    var = jnp.mean(d * d, axis=-1, keepdims=True)
    return d * lax.rsqrt(var + LN_EPS) * g + b


def _dot(a, b):
    return jnp.dot(a, b, preferred_element_type=F32)


def _dot_nt(a, b):
    return lax.dot_general(a, b, (((1,), (1,)), ((), ())), preferred_element_type=F32)


def _log2(n):
    assert n > 0 and n & (n - 1) == 0, n
    return n.bit_length() - 1


def _ffn_kernel(x_ref, xs_ref, wg_ref, *rest, alpha, tail, n_up):
    wu_refs = rest[:n_up]
    wd_ref, g_ref, b_ref, o_ref, os_ref, xb_ref, xsb_ref = rest[n_up:]
    i, j = pl.program_id(0), pl.program_id(1)
    tf = wg_ref.shape[1]
    up_w = wu_refs[0].shape[1]
    last = pl.num_programs(1) - 1

    @pl.when(j == 0)
    def _():
        xb_ref[...] = x_ref[...].astype(BF16)
        o_ref[...] = jnp.zeros_like(o_ref)

    @pl.when((j == 0) & (i == 0))
    def _():
        xsb_ref[...] = xs_ref[...].astype(BF16)
        os_ref[...] = jnp.zeros_like(os_ref)

    def chunk(width):
        w_gate = wg_ref[:, 0:width].astype(BF16)
        w_up = [r[...].astype(BF16) for r in wu_refs[:width // up_w]]
        w_up = w_up[0] if len(w_up) == 1 else jnp.concatenate(w_up, axis=1)
        w_down = wd_ref[0:width, :].astype(BF16)

        def apply(rows_bf16, acc_ref):
            gate = _dot(rows_bf16, w_gate)
            act = (gate * jax.nn.sigmoid(gate) * _dot(rows_bf16, w_up)).astype(BF16)
            acc_ref[...] += _dot(act, w_down)

        apply(xb_ref[...], o_ref)
        pl.when(i == 0)(lambda: apply(xsb_ref[...], os_ref))

    if tail == tf:
        chunk(tf)
    else:
        pl.when(j < last)(lambda: chunk(tf))
        pl.when(j == last)(lambda: chunk(tail))

    @pl.when(j == last)
    def _():
        o_ref[...] = _layer_norm(alpha * x_ref[...] + 0.5 * o_ref[...], g_ref[...], b_ref[...])

    @pl.when((j == last) & (i == 0))
    def _():
        os_ref[...] = _layer_norm(alpha * xs_ref[...] + 0.5 * os_ref[...], g_ref[...], b_ref[...])


def _ffn_ln(x, xs, w_gu, w_down, g, b, *, alpha, tm, tf):
    rows, d = x.shape
    d_ff = w_down.shape[0]
    nj = pl.cdiv(d_ff, tf)
    up_w = math.gcd(d_ff, tf)
    n_up = tf // up_w
    last_up = 2 * d_ff // up_w - 1
    up_specs = [
        pl.BlockSpec((d, up_w), functools.partial(
            lambda i, j, p: (0, jnp.minimum((d_ff + j * tf) // up_w + p, last_up)), p=p))
        for p in range(n_up)
    ]
    const = lambda i, j: (0, 0)
    return pl.pallas_call(
        functools.partial(_ffn_kernel, alpha=alpha, tail=d_ff - (nj - 1) * tf, n_up=n_up),
        grid=(rows // tm, nj),
        in_specs=[
            pl.BlockSpec((tm, d), lambda i, j: (i, 0)),
            pl.BlockSpec(xs.shape, const, pipeline_mode=pl.Buffered(1)),
            pl.BlockSpec((d, tf), lambda i, j: (0, j)),
        ] + up_specs + [
            pl.BlockSpec((tf, d), lambda i, j: (j, 0)),
            pl.BlockSpec((1, d), const),
            pl.BlockSpec((1, d), const),
        ],
        out_specs=[pl.BlockSpec((tm, d), lambda i, j: (i, 0)), pl.BlockSpec(xs.shape, const)],
        out_shape=[jax.ShapeDtypeStruct((rows, d), F32), jax.ShapeDtypeStruct(xs.shape, F32)],
        scratch_shapes=[pltpu.VMEM((tm, d), BF16), pltpu.VMEM(xs.shape, BF16)],
        compiler_params=pltpu.CompilerParams(dimension_semantics=("arbitrary", "arbitrary"),
                                             vmem_limit_bytes=FFN_VMEM_LIMIT),
        name="ffn_ln",
    )(x, xs, w_gu, *([w_gu] * n_up), w_down, g, b)


def _proj_in_kernel(x_ref, w_ref, xr_ref, gr_ref, q_ref, k_ref, v_ref, kb_ref, vb_ref, qi_ref, kw_ref,
                    *, d_rnn, d_q, d_kv, d_qi):
    xb = x_ref[...].astype(BF16)
    off = [0]

    def seg(width):
        lo = off[0]
        off[0] = lo + width
        return _dot(xb, w_ref[:, lo:lo + width])

    xr_ref[...] = seg(d_rnn)
    gr_ref[...] = seg(d_rnn)
    q_ref[...] = seg(d_q).astype(BF16)
    for full_ref, bf16_ref in ((k_ref, kb_ref), (v_ref, vb_ref)):
        kv = seg(d_kv)
        bf16_ref[...] = kv.astype(BF16)
        for n in range(N_KV_HEADS):
            full_ref[:, n, :] = kv[:, n * HEAD_DIM:(n + 1) * HEAD_DIM]
    qi_ref[...] = seg(d_qi).astype(BF16)
    kw_ref[...] = seg(LANES)


def _proj_in(x, w, *, tm, d_rnn, d_q, d_kv, d_qi):
    rows, d = x.shape
    heads = (d_kv // HEAD_DIM, HEAD_DIM)
    widths = ((d_rnn,), (d_rnn,), (d_q,), heads, heads, (d_kv,), (d_kv,), (d_qi,), (LANES,))
    dtypes = (F32, F32, BF16, F32, F32, BF16, BF16, BF16, F32)
    return pl.pallas_call(
        functools.partial(_proj_in_kernel, d_rnn=d_rnn, d_q=d_q, d_kv=d_kv, d_qi=d_qi),
        grid=(rows // tm,),
        in_specs=[
            pl.BlockSpec((tm, d), lambda i: (i, 0)),
            pl.BlockSpec(w.shape, lambda i: (0, 0), pipeline_mode=pl.Buffered(1)),
        ],
        out_specs=[pl.BlockSpec((tm,) + wd, lambda i, nd=len(wd): (i,) + (0,) * nd) for wd in widths],
        out_shape=[jax.ShapeDtypeStruct((rows,) + wd, dt) for wd, dt in zip(widths, dtypes)],
        compiler_params=_cparams(("parallel",)),
        name="proj_in",
    )(x, w)


def _softplus(z):
    return jnp.maximum(z, 0.0) + jnp.log1p(jnp.exp(-jnp.abs(z)))


def _lru_gates(xc, wa_ref, ba, wi_ref, bi, lam):
    bw = xc.shape[1] // LRU_BLOCKS
    sp = _softplus(-lam)
    a_parts, u_parts = [], []
    for n in range(LRU_BLOCKS):
        sl = slice(n * bw, (n + 1) * bw)
        xn = xc[:, sl]
        xb = xn.astype(BF16)
        r = jax.nn.sigmoid(_dot(xb, wa_ref[n]) + ba[:, sl])
        i = jax.nn.sigmoid(_dot(xb, wi_ref[n]) + bi[:, sl])
        log_a = -LRU_C * r * sp[:, sl]
        a = jnp.exp(log_a)
        a_parts.append(a)
        u_parts.append(jnp.sqrt(-jnp.tanh(log_a) * (a * a + 1.0)) * i * xn)
    return jnp.concatenate(a_parts, axis=1), jnp.concatenate(u_parts, axis=1)


def _lru_prompt_kernel(xr_ref, gr_ref, cw_ref, cb_ref, wa_ref, ba_ref, wi_ref, bi_ref, lam_ref,
                       o_ref, hl_ref, xp_ref, a_ref, u_ref, hs_ref, hc_ref, *, tt):
    j = pl.program_id(1)
    d = xr_ref.shape[1]

    @pl.when(j == 0)
    def _():
        xp_ref[0:SUBLANES, :] = jnp.zeros((SUBLANES, d), F32)
        hc_ref[...] = jnp.zeros_like(hc_ref)

    @pl.when(j > 0)
    def _():
        xp_ref[0:SUBLANES, :] = xp_ref[tt:tt + SUBLANES, :]

    xp_ref[SUBLANES:SUBLANES + tt, :] = xr_ref[...]
    cw = cw_ref[...]
    xc = cb_ref[...]
    for jj in range(CONV_W):
        lo = SUBLANES - (CONV_W - 1) + jj
        xc = xc + cw[jj:jj + 1, :] * xp_ref[lo:lo + tt, :]

    a, u = _lru_gates(xc, wa_ref, ba_ref[...], wi_ref, bi_ref[...], lam_ref[...])
    a_ref[...] = a
    u_ref[...] = u

    row = lax.broadcasted_iota(I32, (SUBLANES, d), 0)

    def group(g, h):
        r0 = pl.multiple_of(g * SUBLANES, SUBLANES)
        a8 = a_ref[pl.ds(r0, SUBLANES), :]
        u8 = u_ref[pl.ds(r0, SUBLANES), :]
        out = jnp.zeros((SUBLANES, d), F32)
        for jj in range(SUBLANES):
            aj = jnp.broadcast_to(a8[jj:jj + 1, :], (SUBLANES, d))
            uj = jnp.broadcast_to(u8[jj:jj + 1, :], (SUBLANES, d))
            h = aj * h + uj
            out = jnp.where(row == jj, h, out)
        hs_ref[pl.ds(r0, SUBLANES), :] = out
        return h

    h = lax.fori_loop(0, tt // SUBLANES, group, hc_ref[...])
    hc_ref[...] = h
    hl_ref[...] = h[0:1, :]
    o_ref[...] = (hs_ref[...] * jax.nn.gelu(gr_ref[...])).astype(BF16)


def _lru_prompt(xr, gr, cw, cb, wa, ba, wi, bi, lam, *, batch, tt):
    rows, d = xr.shape
    seq = rows // batch
    nt = seq // tt
    row_spec = pl.BlockSpec((tt, d), lambda b, j: (b * nt + j, 0))
    vec_spec = pl.BlockSpec((1, d), lambda b, j: (0, 0))
    w_spec = pl.BlockSpec(wa.shape, lambda b, j: (0, 0, 0))
    return pl.pallas_call(
        functools.partial(_lru_prompt_kernel, tt=tt),
        grid=(batch, nt),
        in_specs=[row_spec, row_spec, pl.BlockSpec((CONV_W, d), lambda b, j: (0, 0)), vec_spec,
                  w_spec, vec_spec, w_spec, vec_spec, vec_spec],
        out_specs=[row_spec, pl.BlockSpec((None, 1, d), lambda b, j: (b, 0, 0))],
        out_shape=[jax.ShapeDtypeStruct((rows, d), BF16), jax.ShapeDtypeStruct((batch, 1, d), F32)],
        scratch_shapes=[pltpu.VMEM((tt + SUBLANES, d), F32), pltpu.VMEM((tt, d), F32),
                        pltpu.VMEM((tt, d), F32), pltpu.VMEM((tt, d), F32), pltpu.VMEM((SUBLANES, d), F32)],
        compiler_params=_cparams(("parallel", "arbitrary")),
        name="lru_prompt",
    )(xr, gr, cw, cb, wa, ba, wi, bi, lam)


def _lru_sample_kernel(xp_ref, gr_ref, h0_ref, cw_ref, cb_ref, wa_ref, ba_ref, wi_ref, bi_ref, lam_ref,
                       o_ref, hl_ref, *, nb, nt):
    cw = cw_ref[...]
    xc = cb_ref[...] + cw[0:1, :] * xp_ref[0:nt * nb, :]
    for jj in range(1, CONV_W):
        xc = xc + cw[jj:jj + 1, :] * xp_ref[jj * nb:(jj + nt) * nb, :]
    a, u = _lru_gates(xc, wa_ref, ba_ref[...], wi_ref, bi_ref[...], lam_ref[...])
    gate = jax.nn.gelu(gr_ref[...])
    h = h0_ref[...]
    for t in range(nt):
        sl = slice(t * nb, (t + 1) * nb)
        h = a[sl, :] * h + u[sl, :]
        o_ref[sl, :] = (h * gate[sl, :]).astype(BF16)
    hl_ref[...] = h


def _lru_sample(xp, gr, h0, cw, cb, wa, ba, wi, bi, lam, *, nb, nt):
    d = gr.shape[1]
    return pl.pallas_call(
        functools.partial(_lru_sample_kernel, nb=nb, nt=nt),
        out_shape=[jax.ShapeDtypeStruct((nt * nb, d), BF16), jax.ShapeDtypeStruct((nb, d), F32)],
        compiler_params=pltpu.CompilerParams(vmem_limit_bytes=VMEM_LIMIT),
        name="lru_sample",
    )(xp, gr, h0, cw, cb, wa, ba, wi, bi, lam)


def _rank_to_float(u):
    key = u ^ jnp.int32(INT_MIN)
    return lax.bitcast_convert_type(key ^ ((key >> 31) & jnp.int32(0x7FFFFFFF)), F32)


def _count(sc_ref, nc, rows, ck, thr, strict):
    accs = []
    for r0 in range(0, rows, COUNT_ROWS):
        nr = min(COUNT_ROWS, rows - r0)
        thr_b = jnp.broadcast_to(thr[r0:r0 + nr], (nr, LANES))

        def body(c, acc, r0=r0, nr=nr, thr_b=thr_b):
            for t in range(ck // LANES):
                tile = sc_ref[c, r0:r0 + nr, t * LANES:(t + 1) * LANES]
                hit = (tile > thr_b) if strict else (tile >= thr_b)
                acc = acc + jnp.where(hit, 1.0, 0.0)
            return acc

        accs.append(lax.fori_loop(0, nc, body, jnp.zeros((nr, LANES), F32)))
    acc = accs[0] if len(accs) == 1 else jnp.concatenate(accs, axis=0)
    return jnp.sum(acc, axis=1, keepdims=True)


def _select_topk(sc_ref, tau_ref, nc, rows, ck, k_sel):
    kf = float(k_sel)

    def bit_body(it, carry):
        u, n_ge = carry
        cand = u | jnp.left_shift(jnp.int32(1), 31 - it)
        cnt = _count(sc_ref, nc, rows, ck, _rank_to_float(cand), strict=False)
        take = cnt >= kf
        return jnp.where(take, cand, u), jnp.where(take, cnt, n_ge)

    u, n_ge = lax.fori_loop(0, 32, bit_body, (jnp.zeros((rows, 1), I32), jnp.full((rows, 1), kf, F32)))
    tau = _rank_to_float(u)
    tau_ref[...] = jnp.broadcast_to(tau, (rows, LANES))

    @pl.when(jnp.max(n_ge) > kf)
    def _():
        before = (lax.broadcasted_iota(I32, (ck, ck), 0) < lax.broadcasted_iota(I32, (ck, ck), 1))
        before = jnp.where(before, 1.0, 0.0).astype(BF16)
        tau_w = jnp.broadcast_to(tau, (rows, ck))
        need = kf - _count(sc_ref, nc, rows, ck, tau, strict=True)
        need_w = jnp.broadcast_to(need, (rows, ck))

        def body(c, seen):
            sc = sc_ref[c]
            eq = sc == tau_w
            eqf = jnp.where(eq, 1.0, 0.0)
            rank = _dot(eqf.astype(BF16), before) + seen
            sc_ref[c] = jnp.where(eq, jnp.where(rank >= need_w, -jnp.inf, sc), sc)
            return seen + jnp.sum(eqf, axis=1, keepdims=True)

        lax.fori_loop(0, nc, body, jnp.zeros((rows, 1), F32))


def _prompt_attn_kernel(q_ref, qi_ref, kwq_ref, kwk_ref, kb_ref, vb_ref, o_ref,
                        ki_ref, sc_ref, tau_ref, qs_ref, m_ref, l_ref, acc_ref, *, k_sel, qb):
    i = pl.program_id(1)
    ck = sc_ref.shape[2]
    nc = (i * qb + qb + ck - 1) // ck

    @pl.when(i == 0)
    def _():
        ki_ref[...] = kwk_ref[:, 0:IDX_DIM].astype(BF16)

    w_idx = kwq_ref[:, IDX_DIM:IDX_DIM + IDX_HEADS] * IDX_W_SCALE * IDX_SCALE
    t_pos = i * qb + lax.broadcasted_iota(I32, (qb, ck), 0)
    col = lax.broadcasted_iota(I32, (qb, ck), 1)

    def score_body(c, carry):
        c0 = pl.multiple_of(c * ck, ck)
        kc = ki_ref[pl.ds(c0, ck), :]
        score = jnp.zeros((qb, ck), F32)
        for h in range(IDX_HEADS):
            s = _dot_nt(qi_ref[:, h * IDX_DIM:(h + 1) * IDX_DIM], kc)
            score = score + jnp.maximum(s, 0.0) * w_idx[:, h:h + 1]
        sc_ref[c] = jnp.where(col + c0 <= t_pos, score, -jnp.inf)
        return carry

    lax.fori_loop(0, nc, score_body, 0)

    need_select = (i + 1) * qb > k_sel

    @pl.when(jnp.logical_not(need_select))
    def _():
        tau_ref[...] = jnp.full(tau_ref.shape, FLT_LOWEST, F32)

    @pl.when(need_select)
    def _():
        _select_topk(sc_ref, tau_ref, nc, qb, ck, k_sel)
        row_pos = i * qb + lax.broadcasted_iota(I32, (qb, LANES), 0)
        tau_ref[...] = jnp.where(row_pos < k_sel, FLT_LOWEST, tau_ref[...])

    for n in range(N_KV_HEADS):
        for g in range(GQA_GROUP):
            hq = n * GQA_GROUP + g
            qs_ref[n, g * qb:(g + 1) * qb, :] = q_ref[:, hq * HEAD_DIM:(hq + 1) * HEAD_DIM]
    m_ref[...] = jnp.full(m_ref.shape, NEG_BIG, F32)
    l_ref[...] = jnp.zeros_like(l_ref)
    acc_ref[...] = jnp.zeros_like(acc_ref)
    tau_w = jnp.broadcast_to(tau_ref[:, 0:1], (qb, ck))

    def attn_body(c, carry):
        c0 = pl.multiple_of(c * ck, ck)
        bias = jnp.where(sc_ref[c] >= tau_w, 0.0, NEG_BIG)
        bias = jnp.concatenate([bias] * GQA_GROUP, axis=0)
        for n in range(N_KV_HEADS):
            kn = kb_ref[pl.ds(c0, ck), n * HEAD_DIM:(n + 1) * HEAD_DIM]
            vn = vb_ref[pl.ds(c0, ck), n * HEAD_DIM:(n + 1) * HEAD_DIM]
            s = _dot_nt(qs_ref[n], kn) + bias
            tiles = [s[:, t * LANES:(t + 1) * LANES] for t in range(ck // LANES)]
            m_prev = m_ref[n]
            m_new = jnp.maximum(m_prev, jnp.max(functools.reduce(jnp.maximum, tiles), axis=1, keepdims=True))
            alpha = jnp.exp2((m_prev - m_new) * SOFTMAX_LOG2_SCALE)
            p_tiles = [jnp.exp2((t - m_new) * SOFTMAX_LOG2_SCALE) for t in tiles]
            l_ref[n] = alpha * l_ref[n] + functools.reduce(jnp.add, p_tiles)
            p = jnp.concatenate(p_tiles, axis=1).astype(BF16)
            acc_ref[n] = alpha * acc_ref[n] + _dot(p, vn)
            m_ref[n] = m_new
        return carry

    lax.fori_loop(0, nc, attn_body, 0)

    for n in range(N_KV_HEADS):
        o = acc_ref[n] / jnp.sum(l_ref[n], axis=1, keepdims=True)
        for g in range(GQA_GROUP):
            hq = n * GQA_GROUP + g
            o_ref[:, hq * HEAD_DIM:(hq + 1) * HEAD_DIM] = o[g * qb:(g + 1) * qb, :].astype(BF16)


def _prompt_attn(q, qi, kw, kb, vb, *, batch):
    rows = q.shape[0]
    seq = rows // batch
    ck = min(KEY_CHUNK, seq)
    k_sel = min(TOPK_MAX, seq // 4)
    qb = next(r for r in QUERY_ROWS if seq % r == 0)
    nblk = seq // qb
    gq = GQA_GROUP * qb
    blk = lambda b, i: (b * nblk + i, 0)
    per_b = lambda b, i: (b, 0)
    return pl.pallas_call(
        functools.partial(_prompt_attn_kernel, k_sel=k_sel, qb=qb),
        grid=(batch, nblk),
        in_specs=[
            pl.BlockSpec((qb, q.shape[1]), blk),
            pl.BlockSpec((qb, qi.shape[1]), blk),
            pl.BlockSpec((qb, LANES), blk),
            pl.BlockSpec((seq, LANES), per_b),
            pl.BlockSpec((seq, kb.shape[1]), per_b),
            pl.BlockSpec((seq, vb.shape[1]), per_b),
        ],
        out_specs=pl.BlockSpec((qb, q.shape[1]), blk),
        out_shape=jax.ShapeDtypeStruct(q.shape, BF16),
        scratch_shapes=[
            pltpu.VMEM((seq, IDX_DIM), BF16),
            pltpu.VMEM((seq // ck, qb, ck), F32),
            pltpu.VMEM((qb, LANES), F32),
            pltpu.VMEM((N_KV_HEADS, gq, HEAD_DIM), BF16),
            pltpu.VMEM((N_KV_HEADS, gq, LANES), F32),
            pltpu.VMEM((N_KV_HEADS, gq, LANES), F32),
            pltpu.VMEM((N_KV_HEADS, gq, HEAD_DIM), F32),
        ],
        compiler_params=_cparams(("parallel", "arbitrary")),
        name="prompt_attn",
    )(q, qi, kw, kw, kb, vb)


def _page_copy(cache_ref, layer, page, buf_ref, sem_ref, slot, p):
    return pltpu.make_async_copy(cache_ref.at[layer, page], buf_ref.at[slot, p], sem_ref.at[slot])


def _prefetch_pages(pt_ref, npg, streams):
    b, c = pl.program_id(0), pl.program_id(1)
    n_c = pl.num_programs(1)
    step = b * n_c + c
    slot = lax.rem(step, 2)

    def start(bb, cc, sl):
        for cache_ref, layer, buf_ref, sem_ref in streams:
            for p in range(npg):
                _page_copy(cache_ref, layer, pt_ref[bb, cc * npg + p], buf_ref, sem_ref, sl, p).start()

    @pl.when(step == 0)
    def _():
        start(b, c, slot)

    @pl.when(step + 1 < pl.num_programs(0) * n_c)
    def _():
        wrap = c + 1 == n_c
        start(jnp.where(wrap, b + 1, b), jnp.where(wrap, 0, c + 1), 1 - slot)

    for cache_ref, layer, buf_ref, sem_ref in streams:
        for p in range(npg):
            _page_copy(cache_ref, layer, 0, buf_ref, sem_ref, slot, p).wait()
    return slot


def _sample_scores_kernel(pt_ref, qi_ref, w_ref, kin_ref, cache_ref, past_ref, new_ref, buf_ref, kc_ref, sem_ref,
                          *, layer, nt, npg):
    slot = _prefetch_pages(pt_ref, npg, [(cache_ref, layer, buf_ref, sem_ref)])
    c = pl.program_id(1)
    w = w_ref[...] * IDX_W_SCALE * IDX_SCALE
    qi = qi_ref[...]

    def scores(s):
        width = s.shape[1]
        s = jnp.maximum(s, 0.0) * jnp.concatenate([w] * (width // LANES), axis=1)
        return jnp.sum(s.reshape(nt, IDX_HEADS, width), axis=1)

    for p in range(npg):
        kc_ref[:, p * PAGE_SIZE:(p + 1) * PAGE_SIZE] = buf_ref[slot, p].astype(BF16)
    past_ref[...] = scores(_dot(qi, kc_ref[...]))

    @pl.when(c == 0)
    def _():
        new_ref[...] = scores(_dot_nt(qi, kin_ref[...].astype(BF16)))


def _sample_scores(page_table, qi_b, w_b, ki_new, cache_ki, *, layer, nt):
    nb, n_pages = page_table.shape
    npg = min(IDX_PAGES_PER_STEP, n_pages)
    rows = nt * IDX_HEADS
    grid_spec = pltpu.PrefetchScalarGridSpec(
        num_scalar_prefetch=1,
        grid=(nb, n_pages // npg),
        in_specs=[
            pl.BlockSpec((None, rows, IDX_DIM), lambda b, c, pt: (b, 0, 0)),
            pl.BlockSpec((None, rows, LANES), lambda b, c, pt: (b, 0, 0)),
            pl.BlockSpec((None, PAGE_SIZE, IDX_DIM), lambda b, c, pt: (b, 0, 0)),
            pl.BlockSpec(memory_space=pl.ANY),
        ],
        out_specs=[
            pl.BlockSpec((None, nt, npg * PAGE_SIZE), lambda b, c, pt: (b, 0, c)),
            pl.BlockSpec((None, nt, PAGE_SIZE), lambda b, c, pt: (b, 0, 0)),
        ],
        scratch_shapes=[pltpu.VMEM((2, npg, IDX_DIM, PAGE_SIZE), F32), pltpu.VMEM((IDX_DIM, npg * PAGE_SIZE), BF16),
                        pltpu.SemaphoreType.DMA((2,))],
    )
    return pl.pallas_call(
        functools.partial(_sample_scores_kernel, layer=layer, nt=nt, npg=npg),
        grid_spec=grid_spec,
        out_shape=[jax.ShapeDtypeStruct((nb, nt, n_pages * PAGE_SIZE), F32),
                   jax.ShapeDtypeStruct((nb, nt, PAGE_SIZE), F32)],
        compiler_params=_cparams(("arbitrary", "arbitrary")),
        name="sample_scores",
    )(page_table, qi_b, w_b, ki_new, cache_ki)


def _sample_select_kernel(past_ref, new_ref, bias_ref, biasn_ref, sc_ref, tau_ref, *, nt, k_sel, ck):
    rows, n_past = past_ref.shape
    npc = n_past // ck
    for c in range(npc):
        sc_ref[c] = past_ref[:, c * ck:(c + 1) * ck]
    t_row = lax.broadcasted_iota(I32, (rows, ck), 0) & (nt - 1)
    col = lax.broadcasted_iota(I32, (rows, ck), 1)
    new_sc = jnp.concatenate([new_ref[...]] * (ck // PAGE_SIZE), axis=1)
    sc_ref[npc] = jnp.where(col <= t_row, new_sc, -jnp.inf)
    _select_topk(sc_ref, tau_ref, npc + 1, rows, ck, k_sel)
    tau_b = tau_ref[...]
    d = (lax.broadcasted_iota(I32, (LANES, LANES * N_KV_HEADS), 1)
         - N_KV_HEADS * lax.broadcasted_iota(I32, (LANES, LANES * N_KV_HEADS), 0))
    expand = jnp.where(d >= 0, jnp.where(d < N_KV_HEADS, 1.0, 0.0), 0.0).astype(BF16)
    def bias_of(c, t):
        sel = jnp.where(sc_ref[c][:, t * LANES:(t + 1) * LANES] >= tau_b, 1.0, 0.0).astype(BF16)
        return (_dot(sel, expand) - 1.0) * (-NEG_BIG)

    for c in range(npc):
        for t in range(ck // LANES):
            lo = (c * ck + t * LANES) * N_KV_HEADS
            bias_ref[:, lo:lo + LANES * N_KV_HEADS] = bias_of(c, t)
    biasn_ref[...] = bias_of(npc, 0)[:, 0:LANES]


def _sample_select(past, new, *, nt):
    rows, n_past = past.shape
    assert nt & (nt - 1) == 0 and nt * N_KV_HEADS <= LANES
    ck = min(KEY_CHUNK, n_past)
    k_sel = min(TOPK_MAX, (n_past + nt) // 4)
    return pl.pallas_call(
        functools.partial(_sample_select_kernel, nt=nt, k_sel=k_sel, ck=ck),
        out_shape=[jax.ShapeDtypeStruct((rows, n_past * N_KV_HEADS), F32), jax.ShapeDtypeStruct((rows, LANES), F32)],
        scratch_shapes=[pltpu.VMEM((n_past // ck + 1, rows, ck), F32), pltpu.VMEM((rows, LANES), F32)],
        compiler_params=pltpu.CompilerParams(vmem_limit_bytes=VMEM_LIMIT),
        name="sample_select",
    )(past, new)


def _sample_attn_kernel(pt_ref, q_ref, bias_ref, biasn_ref, kn_ref, vn_ref, ck_ref, cv_ref, o_ref,
                        kbuf_ref, vbuf_ref, kc_ref, vc_ref, m_ref, l_ref, acc_ref, ksem_ref, vsem_ref, *, layer, nt, npg):
    slot = _prefetch_pages(pt_ref, npg, [(ck_ref, layer, kbuf_ref, ksem_ref), (cv_ref, layer, vbuf_ref, vsem_ref)])
    c = pl.program_id(1)
    rows = q_ref.shape[0]
    page_rows = PAGE_SIZE * N_KV_HEADS

    @pl.when(c == 0)
    def _():
        m_ref[...] = jnp.full(m_ref.shape, NEG_BIG, F32)
        l_ref[...] = jnp.zeros_like(l_ref)
        acc_ref[...] = jnp.zeros_like(acc_ref)

    row_head = lax.shift_right_logical(lax.broadcasted_iota(I32, (rows, page_rows), 0), _log2(rows // N_KV_HEADS))
    col_head = lax.broadcasted_iota(I32, (rows, page_rows), 1) & (N_KV_HEADS - 1)
    head_bias = jnp.where(row_head == col_head, 0.0, NEG_BIG)
    first_of_pair = pl.program_id(0) % 2 == 0

    def update(k2, v2, bias_blk):
        s = _dot_nt(q_ref[...], k2)
        bias = jnp.where(first_of_pair, bias_blk[0:nt], bias_blk[nt:2 * nt])
        bias = jnp.concatenate([bias] * (rows // nt), axis=0)
        lanes = lambda a, t: a[:, t * LANES:(t + 1) * LANES]
        per_page = page_rows // LANES
        tiles = [lanes(s, t) + (lanes(bias, t) + lanes(head_bias, t % per_page)) for t in range(k2.shape[0] // LANES)]
        m_prev = m_ref[...]
        m_new = jnp.maximum(m_prev, jnp.max(functools.reduce(jnp.maximum, tiles), axis=1, keepdims=True))
        alpha = jnp.exp2((m_prev - m_new) * SOFTMAX_LOG2_SCALE)
        p_tiles = [jnp.exp2((t - m_new) * SOFTMAX_LOG2_SCALE) for t in tiles]
        l_ref[...] = alpha * l_ref[...] + functools.reduce(jnp.add, p_tiles)
        acc_ref[...] = alpha * acc_ref[...] + _dot(jnp.concatenate(p_tiles, axis=1).astype(BF16), v2)
        m_ref[...] = m_new

    for p in range(npg):
        kc_ref[p * page_rows:(p + 1) * page_rows, :] = kbuf_ref[slot, p].reshape(page_rows, HEAD_DIM).astype(BF16)
        vc_ref[p * page_rows:(p + 1) * page_rows, :] = vbuf_ref[slot, p].reshape(page_rows, HEAD_DIM).astype(BF16)
    update(kc_ref[...], vc_ref[...], bias_ref[...])

    @pl.when(c == pl.num_programs(1) - 1)
    def _():
        def new_rows(ref):
            pad = jnp.zeros((LANES - ref.shape[0], HEAD_DIM), F32)
            return jnp.concatenate([ref[...], pad], axis=0).astype(BF16)

        update(new_rows(kn_ref), new_rows(vn_ref), biasn_ref[...])
        o_ref[...] = acc_ref[...] / jnp.sum(l_ref[...], axis=1, keepdims=True)


def _sample_attn(page_table, q_b, bias, bias_new, k_new, v_new, cache_k, cache_v, *, layer, nt):
    nb, n_pages = page_table.shape
    npg = min(PAGES_PER_STEP, n_pages)
    rows = q_b.shape[1]
    page_rows = PAGE_SIZE * N_KV_HEADS
    assert N_KV_HEADS & (N_KV_HEADS - 1) == 0 and 2 * nt == SUBLANES and nb % 2 == 0

    per_b3 = lambda b, c, pt: (b, 0, 0)
    grid_spec = pltpu.PrefetchScalarGridSpec(
        num_scalar_prefetch=1,
        grid=(nb, n_pages // npg),
        in_specs=[
            pl.BlockSpec((None, rows, HEAD_DIM), per_b3),
            pl.BlockSpec((2 * nt, npg * page_rows), lambda b, c, pt: (b // 2, c)),
            pl.BlockSpec((2 * nt, LANES), lambda b, c, pt: (b // 2, 0)),
            pl.BlockSpec((None, nt * N_KV_HEADS, HEAD_DIM), per_b3),
            pl.BlockSpec((None, nt * N_KV_HEADS, HEAD_DIM), per_b3),
            pl.BlockSpec(memory_space=pl.ANY),
            pl.BlockSpec(memory_space=pl.ANY),
        ],
        out_specs=pl.BlockSpec((None, rows, HEAD_DIM), per_b3),
        scratch_shapes=[
            pltpu.VMEM((2, npg, PAGE_SIZE, N_KV_HEADS, HEAD_DIM), F32),
            pltpu.VMEM((2, npg, PAGE_SIZE, N_KV_HEADS, HEAD_DIM), F32),
            pltpu.VMEM((npg * page_rows, HEAD_DIM), BF16),
            pltpu.VMEM((npg * page_rows, HEAD_DIM), BF16),
            pltpu.VMEM((rows, LANES), F32),
            pltpu.VMEM((rows, LANES), F32),
            pltpu.VMEM((rows, HEAD_DIM), F32),
            pltpu.SemaphoreType.DMA((2,)),
            pltpu.SemaphoreType.DMA((2,)),
        ],
    )
    return pl.pallas_call(
        functools.partial(_sample_attn_kernel, layer=layer, nt=nt, npg=npg),
        grid_spec=grid_spec,
        out_shape=jax.ShapeDtypeStruct((nb, rows, HEAD_DIM), F32),
        compiler_params=_cparams(("arbitrary", "arbitrary")),
        name="sample_attn",
    )(page_table, q_b, bias, bias_new, k_new, v_new, cache_k, cache_v)


def _proj_out_kernel(x_ref, r_ref, a_ref, w_ref, g_ref, b_ref, o_ref, *, alpha):
    d_rnn = r_ref.shape[1]
    mix = _dot(r_ref[...], w_ref[0:d_rnn, :]) + _dot(a_ref[...], w_ref[d_rnn:, :])
    o_ref[...] = _layer_norm(alpha * x_ref[...] + mix, g_ref[...], b_ref[...])


def _proj_out_ln(x, rnn, attn, w, g, b, *, alpha, tm):
    rows, d = x.shape
    return pl.pallas_call(
        functools.partial(_proj_out_kernel, alpha=alpha),
        grid=(rows // tm,),
        in_specs=[
            pl.BlockSpec((tm, d), lambda i: (i, 0)),
            pl.BlockSpec((tm, rnn.shape[1]), lambda i: (i, 0)),
            pl.BlockSpec((tm, attn.shape[1]), lambda i: (i, 0)),
            pl.BlockSpec(w.shape, lambda i: (0, 0)),
            pl.BlockSpec((1, d), lambda i: (0, 0)),
            pl.BlockSpec((1, d), lambda i: (0, 0)),
        ],
        out_specs=pl.BlockSpec((tm, d), lambda i: (i, 0)),
        out_shape=jax.ShapeDtypeStruct((rows, d), F32),
        compiler_params=_cparams(("parallel",)),
        name="proj_out_ln",
    )(x, rnn, attn, w, g, b)


def _row_tile(rows, target):
    tm = min(rows, target)
    while rows % tm:
        tm //= 2
    return tm


def _pad_to(a, axis, size):
    pad = [(0, 0)] * a.ndim
    pad[axis] = (0, size - a.shape[axis])
    return jnp.pad(a, pad)


def kernel(x_prompt, x_sample, cache_k, cache_v, cache_k_idx, state_conv, state_rnn, page_table, ln1_g, ln1_b, ffn1_w_gu, ffn1_w_down, w_in, conv_w, conv_b, lru_w_a, lru_b_a, lru_w_i, lru_b_i, lru_lambda, w_out, ln2_g, ln2_b, ffn2_w_gu, ffn2_w_down, ln3_g, ln3_b):
    depth = w_in.shape[0]
    bp, seq, d_model = x_prompt.shape
    nb, nt, _ = x_sample.shape
    d_rnn = conv_w.shape[2]
    d_q = N_Q_HEADS * HEAD_DIM
    d_kv = N_KV_HEADS * HEAD_DIM
    d_qi = IDX_HEADS * IDX_DIM
    alpha = (2.0 * depth) ** 0.25

    xp = x_prompt.reshape(bp * seq, d_model)
    xs = x_sample.transpose(1, 0, 2).reshape(nt * nb, d_model)
    tm_p = _row_tile(bp * seq, 1024)
    tm_r = _row_tile(bp * seq, 512)
    tm_s = _row_tile(nt * nb, 512)
    vec = lambda a: a.reshape(1, -1)

    outs = [[] for _ in range(10)]
    for l in range(depth):
        d_in = w_in.shape[2]
        w_in_b = _pad_to(w_in[l].astype(BF16), 1, d_in - IDX_DIM - IDX_HEADS + LANES)
        w_out_b = w_out[l].astype(BF16)
        wa_b = lru_w_a[l].astype(BF16)
        wi_b = lru_w_i[l].astype(BF16)
        lru_args = (conv_w[l], vec(conv_b[l]), wa_b, vec(lru_b_a[l]), wi_b, vec(lru_b_i[l]), vec(lru_lambda[l]))
        proj = functools.partial(_proj_in, d_rnn=d_rnn, d_q=d_q, d_kv=d_kv, d_qi=d_qi)

        x1p, x1s = _ffn_ln(xp, xs, ffn1_w_gu[l], ffn1_w_down[l], vec(ln1_g[l]), vec(ln1_b[l]),
                           alpha=alpha, tm=tm_p, tf=MXU_DIM)
        xr_p, gr_p, q_p, k_p, v_p, kb_p, vb_p, qi_p, kw_p = proj(x1p, w_in_b, tm=_row_tile(bp * seq, 256))
        xr_s, gr_s, q_s, k_s, v_s, _, _, qi_s, kw_s = proj(x1s, w_in_b, tm=_row_tile(nt * nb, 256))

        rnn_p, hl_p = _lru_prompt(xr_p, gr_p, *lru_args, batch=bp, tt=_row_tile(seq, 512))
        conv_tm = state_conv[l].transpose(1, 0, 2).reshape((CONV_W - 1) * nb, d_rnn)
        xpad_s = jnp.concatenate([conv_tm, xr_s], axis=0)
        rnn_s, hl_s = _lru_sample(xpad_s, gr_s, state_rnn[l], *lru_args, nb=nb, nt=nt)

        attn_p = _prompt_attn(q_p, qi_p, kw_p, kb_p, vb_p, batch=bp)

        to_b = lambda a: a.reshape(nt, nb, -1).transpose(1, 0, 2)
        qi_b = to_b(qi_s).reshape(nb, nt * IDX_HEADS, IDX_DIM)
        kw_b = to_b(kw_s)
        w_b = jnp.broadcast_to(kw_b[:, :, IDX_DIM:IDX_DIM + IDX_HEADS].reshape(nb, nt * IDX_HEADS, 1),
                               (nb, nt * IDX_HEADS, LANES))
        ki_new = _pad_to(kw_b[:, :, :IDX_DIM], 1, PAGE_SIZE)
        cache_ki_t = jnp.swapaxes(cache_k_idx, 2, 3)
        past_sc, new_sc = _sample_scores(page_table, qi_b, w_b, ki_new, cache_ki_t, layer=l, nt=nt)
        n_past = past_sc.shape[2]
        bias, bias_new = _sample_select(past_sc.reshape(nb * nt, n_past), new_sc.reshape(nb * nt, PAGE_SIZE), nt=nt)
        q_b = to_b(q_s).reshape(nb, nt, N_KV_HEADS, GQA_GROUP, HEAD_DIM)
        q_b = q_b.transpose(0, 2, 3, 1, 4).reshape(nb, N_Q_HEADS * nt, HEAD_DIM)
        new_rows = lambda a: to_b(a).reshape(nb, nt * N_KV_HEADS, HEAD_DIM)
        o_b = _sample_attn(page_table, q_b, bias, bias_new, new_rows(k_s), new_rows(v_s), cache_k, cache_v,
                           layer=l, nt=nt)
        attn_s = o_b.reshape(nb, N_KV_HEADS, GQA_GROUP, nt, HEAD_DIM).transpose(3, 0, 1, 2, 4)
        attn_s = attn_s.reshape(nt * nb, d_q).astype(BF16)

        x2p = _proj_out_ln(x1p, rnn_p, attn_p, w_out_b, vec(ln2_g[l]), vec(ln2_b[l]), alpha=alpha, tm=tm_r)
        x2s = _proj_out_ln(x1s, rnn_s, attn_s, w_out_b, vec(ln2_g[l]), vec(ln2_b[l]), alpha=alpha, tm=tm_s)
        xp, xs = _ffn_ln(x2p, x2s, ffn2_w_gu[l], ffn2_w_down[l], vec(ln3_g[l]), vec(ln3_b[l]),
                         alpha=alpha, tm=tm_p, tf=MXU_DIM)

        xr_p3 = xr_p.reshape(bp, seq, d_rnn)
        conv_p = xr_p3[:, -(CONV_W - 1):]
        conv_s = xpad_s.reshape(CONV_W - 1 + nt, nb, d_rnn)[-(CONV_W - 1):].transpose(1, 0, 2)
        layer_out = (
            k_p.reshape(bp, seq, N_KV_HEADS, HEAD_DIM), v_p.reshape(bp, seq, N_KV_HEADS, HEAD_DIM),
            kw_p[:, :IDX_DIM].reshape(bp, seq, IDX_DIM), conv_p, hl_p.reshape(bp, d_rnn),
            to_b(k_s).reshape(nb, nt, N_KV_HEADS, HEAD_DIM), to_b(v_s).reshape(nb, nt, N_KV_HEADS, HEAD_DIM),
            kw_b[:, :, :IDX_DIM], conv_s, hl_s,
        )
        for acc, val in zip(outs, layer_out):
            acc.append(val)

    y_p = xp.reshape(bp, seq, d_model)
    y_s = xs.reshape(nt, nb, d_model).transpose(1, 0, 2)
    return (y_p, y_s) + tuple(jnp.stack(o) for o in outs)
```

```python
import functools
import math

import jax
import jax.numpy as jnp
from jax import lax
from jax.experimental import pallas as pl
from jax.experimental.pallas import tpu as pltpu

F32 = jnp.float32
BF16 = jnp.bfloat16
I32 = jnp.int32

LRU_BLOCKS = 8
CONV_W = 4
LRU_C = 8.0
HEAD_DIM = 128
N_KV_HEADS = 4
GQA_GROUP = 2
N_Q_HEADS = N_KV_HEADS * GQA_GROUP
IDX_HEADS = 8
IDX_DIM = 64
TOPK_MAX = 256
QUERY_ROWS = (512, 256, 128)
COUNT_ROWS = 128
PAGE_SIZE = 128
LN_EPS = 1e-5
ATTN_SCALE = HEAD_DIM ** -0.5
SOFTMAX_LOG2_SCALE = ATTN_SCALE * 1.4426950408889634
IDX_SCALE = IDX_DIM ** -0.5
IDX_W_SCALE = IDX_HEADS ** -0.5

LANES = 128
SUBLANES = 8
MXU_DIM = 256
VMEM_LIMIT = 56 * 1024 * 1024
FFN_VMEM_LIMIT = 60 * 1024 * 1024

INT_MIN = -2 ** 31
FLT_LOWEST = -3.4028234663852886e38
NEG_BIG = -1e30
KEY_CHUNK = 512
PAGES_PER_STEP = 16
IDX_PAGES_PER_STEP = 64


def _cparams(semantics):
    return pltpu.CompilerParams(dimension_semantics=semantics, vmem_limit_bytes=VMEM_LIMIT)


def _layer_norm(y, g, b):
    mu = jnp.mean(y, axis=-1, keepdims=True)
    d = y - mu
    var = jnp.mean(d * d, axis=-1, keepdims=True)
    return d * lax.rsqrt(var + LN_EPS) * g + b


def _dot(a, b):
    return jnp.dot(a, b, preferred_element_type=F32)


def _dot_nt(a, b):
    return lax.dot_general(a, b, (((1,), (1,)), ((), ())), preferred_element_type=F32)


def _log2(n):
    assert n > 0 and n & (n - 1) == 0, n
    return n.bit_length() - 1


def _ffn_kernel(x_ref, xs_ref, wg_ref, *rest, alpha, tail, n_up):
    wu_refs = rest[:n_up]
    wd_ref, g_ref, b_ref, o_ref, os_ref, xb_ref, xsb_ref = rest[n_up:]
    i, j = pl.program_id(0), pl.program_id(1)
    tf = wg_ref.shape[1]
    up_w = wu_refs[0].shape[1]
    last = pl.num_programs(1) - 1

    @pl.when(j == 0)
    def _():
        xb_ref[...] = x_ref[...].astype(BF16)
        o_ref[...] = jnp.zeros_like(o_ref)

    @pl.when((j == 0) & (i == 0))
    def _():
        xsb_ref[...] = xs_ref[...].astype(BF16)
        os_ref[...] = jnp.zeros_like(os_ref)

    def chunk(width):
        w_gate = wg_ref[:, 0:width].astype(BF16)
        w_up = [r[...].astype(BF16) for r in wu_refs[:width // up_w]]
        w_up = w_up[0] if len(w_up) == 1 else jnp.concatenate(w_up, axis=1)
        w_down = wd_ref[0:width, :].astype(BF16)

        def apply(rows_bf16, acc_ref):
            gate = _dot(rows_bf16, w_gate)
            act = (gate * jax.nn.sigmoid(gate) * _dot(rows_bf16, w_up)).astype(BF16)
            acc_ref[...] += _dot(act, w_down)

        apply(xb_ref[...], o_ref)
        pl.when(i == 0)(lambda: apply(xsb_ref[...], os_ref))

    if tail == tf:
        chunk(tf)
    else:
        pl.when(j < last)(lambda: chunk(tf))
        pl.when(j == last)(lambda: chunk(tail))

    @pl.when(j == last)
    def _():
        o_ref[...] = _layer_norm(alpha * x_ref[...] + 0.5 * o_ref[...], g_ref[...], b_ref[...])

    @pl.when((j == last) & (i == 0))
    def _():
        os_ref[...] = _layer_norm(alpha * xs_ref[...] + 0.5 * os_ref[...], g_ref[...], b_ref[...])


def _ffn_ln(x, xs, w_gu, w_down, g, b, *, alpha, tm, tf):
    rows, d = x.shape
    d_ff = w_down.shape[0]
    nj = pl.cdiv(d_ff, tf)
    up_w = math.gcd(d_ff, tf)
    n_up = tf // up_w
    last_up = 2 * d_ff // up_w - 1
    up_specs = [
        pl.BlockSpec((d, up_w), functools.partial(
            lambda i, j, p: (0, jnp.minimum((d_ff + j * tf) // up_w + p, last_up)), p=p))
        for p in range(n_up)
    ]
    const = lambda i, j: (0, 0)
    return pl.pallas_call(
        functools.partial(_ffn_kernel, alpha=alpha, tail=d_ff - (nj - 1) * tf, n_up=n_up),
        grid=(rows // tm, nj),
        in_specs=[
            pl.BlockSpec((tm, d), lambda i, j: (i, 0)),
            pl.BlockSpec(xs.shape, const, pipeline_mode=pl.Buffered(1)),
            pl.BlockSpec((d, tf), lambda i, j: (0, j)),
        ] + up_specs + [
            pl.BlockSpec((tf, d), lambda i, j: (j, 0)),
            pl.BlockSpec((1, d), const),
            pl.BlockSpec((1, d), const),
        ],
        out_specs=[pl.BlockSpec((tm, d), lambda i, j: (i, 0)), pl.BlockSpec(xs.shape, const)],
        out_shape=[jax.ShapeDtypeStruct((rows, d), F32), jax.ShapeDtypeStruct(xs.shape, F32)],
        scratch_shapes=[pltpu.VMEM((tm, d), BF16), pltpu.VMEM(xs.shape, BF16)],
        compiler_params=pltpu.CompilerParams(dimension_semantics=("arbitrary", "arbitrary"),
                                             vmem_limit_bytes=FFN_VMEM_LIMIT),
        name="ffn_ln",
    )(x, xs, w_gu, *([w_gu] * n_up), w_down, g, b)


def _proj_in_kernel(x_ref, w_ref, xr_ref, gr_ref, q_ref, k_ref, v_ref, kb_ref, vb_ref, qi_ref, kw_ref,
                    *, d_rnn, d_q, d_kv, d_qi):
    xb = x_ref[...].astype(BF16)
    off = [0]

    def seg(width):
        lo = off[0]
        off[0] = lo + width
        return _dot(xb, w_ref[:, lo:lo + width])

    xr_ref[...] = seg(d_rnn)
    gr_ref[...] = seg(d_rnn)
    q_ref[...] = seg(d_q).astype(BF16)
    for full_ref, bf16_ref in ((k_ref, kb_ref), (v_ref, vb_ref)):
        kv = seg(d_kv)
        bf16_ref[...] = kv.astype(BF16)
        for n in range(N_KV_HEADS):
            full_ref[:, n, :] = kv[:, n * HEAD_DIM:(n + 1) * HEAD_DIM]
    qi_ref[...] = seg(d_qi).astype(BF16)
    kw_ref[...] = seg(LANES)


def _proj_in(x, w, *, tm, d_rnn, d_q, d_kv, d_qi):
    rows, d = x.shape
    heads = (d_kv // HEAD_DIM, HEAD_DIM)
    widths = ((d_rnn,), (d_rnn,), (d_q,), heads, heads, (d_kv,), (d_kv,), (d_qi,), (LANES,))
    dtypes = (F32, F32, BF16, F32, F32, BF16, BF16, BF16, F32)
    return pl.pallas_call(
        functools.partial(_proj_in_kernel, d_rnn=d_rnn, d_q=d_q, d_kv=d_kv, d_qi=d_qi),
        grid=(rows // tm,),
        in_specs=[
            pl.BlockSpec((tm, d), lambda i: (i, 0)),
            pl.BlockSpec(w.shape, lambda i: (0, 0), pipeline_mode=pl.Buffered(1)),
        ],
        out_specs=[pl.BlockSpec((tm,) + wd, lambda i, nd=len(wd): (i,) + (0,) * nd) for wd in widths],
        out_shape=[jax.ShapeDtypeStruct((rows,) + wd, dt) for wd, dt in zip(widths, dtypes)],
        compiler_params=_cparams(("parallel",)),
        name="proj_in",
    )(x, w)


def _softplus(z):
    return jnp.maximum(z, 0.0) + jnp.log1p(jnp.exp(-jnp.abs(z)))


def _lru_gates(xc, wa_ref, ba, wi_ref, bi, lam):
    bw = xc.shape[1] // LRU_BLOCKS
    sp = _softplus(-lam)
    a_parts, u_parts = [], []
    for n in range(LRU_BLOCKS):
        sl = slice(n * bw, (n + 1) * bw)
        xn = xc[:, sl]
        xb = xn.astype(BF16)
        r = jax.nn.sigmoid(_dot(xb, wa_ref[n]) + ba[:, sl])
        i = jax.nn.sigmoid(_dot(xb, wi_ref[n]) + bi[:, sl])
        log_a = -LRU_C * r * sp[:, sl]
        a = jnp.exp(log_a)
        a_parts.append(a)
        u_parts.append(jnp.sqrt(-jnp.tanh(log_a) * (a * a + 1.0)) * i * xn)
    return jnp.concatenate(a_parts, axis=1), jnp.concatenate(u_parts, axis=1)


def _lru_prompt_kernel(xr_ref, gr_ref, cw_ref, cb_ref, wa_ref, ba_ref, wi_ref, bi_ref, lam_ref,
                       o_ref, hl_ref, xp_ref, a_ref, u_ref, hs_ref, hc_ref, *, tt):
    j = pl.program_id(1)
    d = xr_ref.shape[1]

    @pl.when(j == 0)
    def _():
        xp_ref[0:SUBLANES, :] = jnp.zeros((SUBLANES, d), F32)
        hc_ref[...] = jnp.zeros_like(hc_ref)

    @pl.when(j > 0)
    def _():
        xp_ref[0:SUBLANES, :] = xp_ref[tt:tt + SUBLANES, :]

    xp_ref[SUBLANES:SUBLANES + tt, :] = xr_ref[...]
    cw = cw_ref[...]
    xc = cb_ref[...]
    for jj in range(CONV_W):
        lo = SUBLANES - (CONV_W - 1) + jj
        xc = xc + cw[jj:jj + 1, :] * xp_ref[lo:lo + tt, :]

    a, u = _lru_gates(xc, wa_ref, ba_ref[...], wi_ref, bi_ref[...], lam_ref[...])
    a_ref[...] = a
    u_ref[...] = u

    row = lax.broadcasted_iota(I32, (SUBLANES, d), 0)

    def group(g, h):
        r0 = pl.multiple_of(g * SUBLANES, SUBLANES)
        a8 = a_ref[pl.ds(r0, SUBLANES), :]
        u8 = u_ref[pl.ds(r0, SUBLANES), :]
        out = jnp.zeros((SUBLANES, d), F32)
        for jj in range(SUBLANES):
            aj = jnp.broadcast_to(a8[jj:jj + 1, :], (SUBLANES, d))
            uj = jnp.broadcast_to(u8[jj:jj + 1, :], (SUBLANES, d))
            h = aj * h + uj
            out = jnp.where(row == jj, h, out)
        hs_ref[pl.ds(r0, SUBLANES), :] = out
        return h

    h = lax.fori_loop(0, tt // SUBLANES, group, hc_ref[...])
    hc_ref[...] = h
    hl_ref[...] = h[0:1, :]
    o_ref[...] = (hs_ref[...] * jax.nn.gelu(gr_ref[...])).astype(BF16)


def _lru_prompt(xr, gr, cw, cb, wa, ba, wi, bi, lam, *, batch, tt):
    rows, d = xr.shape
    seq = rows // batch
    nt = seq // tt
    row_spec = pl.BlockSpec((tt, d), lambda b, j: (b * nt + j, 0))
    vec_spec = pl.BlockSpec((1, d), lambda b, j: (0, 0))
    w_spec = pl.BlockSpec(wa.shape, lambda b, j: (0, 0, 0))
    return pl.pallas_call(
        functools.partial(_lru_prompt_kernel, tt=tt),
        grid=(batch, nt),
        in_specs=[row_spec, row_spec, pl.BlockSpec((CONV_W, d), lambda b, j: (0, 0)), vec_spec,
                  w_spec, vec_spec, w_spec, vec_spec, vec_spec],
        out_specs=[row_spec, pl.BlockSpec((None, 1, d), lambda b, j: (b, 0, 0))],
        out_shape=[jax.ShapeDtypeStruct((rows, d), BF16), jax.ShapeDtypeStruct((batch, 1, d), F32)],
        scratch_shapes=[pltpu.VMEM((tt + SUBLANES, d), F32), pltpu.VMEM((tt, d), F32),
                        pltpu.VMEM((tt, d), F32), pltpu.VMEM((tt, d), F32), pltpu.VMEM((SUBLANES, d), F32)],
        compiler_params=_cparams(("parallel", "arbitrary")),
        name="lru_prompt",
    )(xr, gr, cw, cb, wa, ba, wi, bi, lam)


def _lru_sample_kernel(xp_ref, gr_ref, h0_ref, cw_ref, cb_ref, wa_ref, ba_ref, wi_ref, bi_ref, lam_ref,
                       o_ref, hl_ref, *, nb, nt):
    cw = cw_ref[...]
    xc = cb_ref[...] + cw[0:1, :] * xp_ref[0:nt * nb, :]
    for jj in range(1, CONV_W):
        xc = xc + cw[jj:jj + 1, :] * xp_ref[jj * nb:(jj + nt) * nb, :]
    a, u = _lru_gates(xc, wa_ref, ba_ref[...], wi_ref, bi_ref[...], lam_ref[...])
    gate = jax.nn.gelu(gr_ref[...])
    h = h0_ref[...]
    for t in range(nt):
        sl = slice(t * nb, (t + 1) * nb)
        h = a[sl, :] * h + u[sl, :]
        o_ref[sl, :] = (h * gate[sl, :]).astype(BF16)
    hl_ref[...] = h


def _lru_sample(xp, gr, h0, cw, cb, wa, ba, wi, bi, lam, *, nb, nt):
    d = gr.shape[1]
    return pl.pallas_call(
        functools.partial(_lru_sample_kernel, nb=nb, nt=nt),
        out_shape=[jax.ShapeDtypeStruct((nt * nb, d), BF16), jax.ShapeDtypeStruct((nb, d), F32)],
        compiler_params=pltpu.CompilerParams(vmem_limit_bytes=VMEM_LIMIT),
        name="lru_sample",
    )(xp, gr, h0, cw, cb, wa, ba, wi, bi, lam)


def _rank_to_float(u):
    key = u ^ jnp.int32(INT_MIN)
    return lax.bitcast_convert_type(key ^ ((key >> 31) & jnp.int32(0x7FFFFFFF)), F32)


def _count(sc_ref, nc, rows, ck, thr, strict):
    accs = []
    for r0 in range(0, rows, COUNT_ROWS):
        nr = min(COUNT_ROWS, rows - r0)
        thr_b = jnp.broadcast_to(thr[r0:r0 + nr], (nr, LANES))

        def body(c, acc, r0=r0, nr=nr, thr_b=thr_b):
            for t in range(ck // LANES):
                tile = sc_ref[c, r0:r0 + nr, t * LANES:(t + 1) * LANES]
                hit = (tile > thr_b) if strict else (tile >= thr_b)
                acc = acc + jnp.where(hit, 1.0, 0.0)
            return acc

        accs.append(lax.fori_loop(0, nc, body, jnp.zeros((nr, LANES), F32)))
    acc = accs[0] if len(accs) == 1 else jnp.concatenate(accs, axis=0)
    return jnp.sum(acc, axis=1, keepdims=True)


def _select_topk(sc_ref, tau_ref, nc, rows, ck, k_sel):
    kf = float(k_sel)

    def bit_body(it, carry):
        u, n_ge = carry
        cand = u | jnp.left_shift(jnp.int32(1), 31 - it)
        cnt = _count(sc_ref, nc, rows, ck, _rank_to_float(cand), strict=False)
        take = cnt >= kf
        return jnp.where(take, cand, u), jnp.where(take, cnt, n_ge)

    u, n_ge = lax.fori_loop(0, 32, bit_body, (jnp.zeros((rows, 1), I32), jnp.full((rows, 1), kf, F32)))
    tau = _rank_to_float(u)
    tau_ref[...] = jnp.broadcast_to(tau, (rows, LANES))

    @pl.when(jnp.max(n_ge) > kf)
    def _():
        before = (lax.broadcasted_iota(I32, (ck, ck), 0) < lax.broadcasted_iota(I32, (ck, ck), 1))
        before = jnp.where(before, 1.0, 0.0).astype(BF16)
        tau_w = jnp.broadcast_to(tau, (rows, ck))
        need = kf - _count(sc_ref, nc, rows, ck, tau, strict=True)
        need_w = jnp.broadcast_to(need, (rows, ck))

        def body(c, seen):
            sc = sc_ref[c]
            eq = sc == tau_w
            eqf = jnp.where(eq, 1.0, 0.0)
            rank = _dot(eqf.astype(BF16), before) + seen
            sc_ref[c] = jnp.where(eq, jnp.where(rank >= need_w, -jnp.inf, sc), sc)
            return seen + jnp.sum(eqf, axis=1, keepdims=True)

        lax.fori_loop(0, nc, body, jnp.zeros((rows, 1), F32))


def _prompt_attn_kernel(q_ref, qi_ref, kwq_ref, kwk_ref, kb_ref, vb_ref, o_ref,
                        ki_ref, sc_ref, tau_ref, qs_ref, m_ref, l_ref, acc_ref, *, k_sel, qb):
    i = pl.program_id(1)
    ck = sc_ref.shape[2]
    nc = (i * qb + qb + ck - 1) // ck

    @pl.when(i == 0)
    def _():
        ki_ref[...] = kwk_ref[:, 0:IDX_DIM].astype(BF16)

    w_idx = kwq_ref[:, IDX_DIM:IDX_DIM + IDX_HEADS] * IDX_W_SCALE * IDX_SCALE
    t_pos = i * qb + lax.broadcasted_iota(I32, (qb, ck), 0)
    col = lax.broadcasted_iota(I32, (qb, ck), 1)

    def score_body(c, carry):
        c0 = pl.multiple_of(c * ck, ck)
        kc = ki_ref[pl.ds(c0, ck), :]
        score = jnp.zeros((qb, ck), F32)
        for h in range(IDX_HEADS):
            s = _dot_nt(qi_ref[:, h * IDX_DIM:(h + 1) * IDX_DIM], kc)
            score = score + jnp.maximum(s, 0.0) * w_idx[:, h:h + 1]
        sc_ref[c] = jnp.where(col + c0 <= t_pos, score, -jnp.inf)
        return carry

    lax.fori_loop(0, nc, score_body, 0)

    need_select = (i + 1) * qb > k_sel

    @pl.when(jnp.logical_not(need_select))
    def _():
        tau_ref[...] = jnp.full(tau_ref.shape, FLT_LOWEST, F32)

    @pl.when(need_select)
    def _():
        _select_topk(sc_ref, tau_ref, nc, qb, ck, k_sel)
        row_pos = i * qb + lax.broadcasted_iota(I32, (qb, LANES), 0)
        tau_ref[...] = jnp.where(row_pos < k_sel, FLT_LOWEST, tau_ref[...])

    for n in range(N_KV_HEADS):
        for g in range(GQA_GROUP):
            hq = n * GQA_GROUP + g
            qs_ref[n, g * qb:(g + 1) * qb, :] = q_ref[:, hq * HEAD_DIM:(hq + 1) * HEAD_DIM]
    m_ref[...] = jnp.full(m_ref.shape, NEG_BIG, F32)
    l_ref[...] = jnp.zeros_like(l_ref)
    acc_ref[...] = jnp.zeros_like(acc_ref)
    tau_w = jnp.broadcast_to(tau_ref[:, 0:1], (qb, ck))

    def attn_body(c, carry):
        c0 = pl.multiple_of(c * ck, ck)
        bias = jnp.where(sc_ref[c] >= tau_w, 0.0, NEG_BIG)
        bias = jnp.concatenate([bias] * GQA_GROUP, axis=0)
        for n in range(N_KV_HEADS):
            kn = kb_ref[pl.ds(c0, ck), n * HEAD_DIM:(n + 1) * HEAD_DIM]
            vn = vb_ref[pl.ds(c0, ck), n * HEAD_DIM:(n + 1) * HEAD_DIM]
            s = _dot_nt(qs_ref[n], kn) + bias
            tiles = [s[:, t * LANES:(t + 1) * LANES] for t in range(ck // LANES)]
            m_prev = m_ref[n]
            m_new = jnp.maximum(m_prev, jnp.max(functools.reduce(jnp.maximum, tiles), axis=1, keepdims=True))
            alpha = jnp.exp2((m_prev - m_new) * SOFTMAX_LOG2_SCALE)
            p_tiles = [jnp.exp2((t - m_new) * SOFTMAX_LOG2_SCALE) for t in tiles]
            l_ref[n] = alpha * l_ref[n] + functools.reduce(jnp.add, p_tiles)
            p = jnp.concatenate(p_tiles, axis=1).astype(BF16)
            acc_ref[n] = alpha * acc_ref[n] + _dot(p, vn)
            m_ref[n] = m_new
        return carry

    lax.fori_loop(0, nc, attn_body, 0)

    for n in range(N_KV_HEADS):
        o = acc_ref[n] / jnp.sum(l_ref[n], axis=1, keepdims=True)
        for g in range(GQA_GROUP):
            hq = n * GQA_GROUP + g
            o_ref[:, hq * HEAD_DIM:(hq + 1) * HEAD_DIM] = o[g * qb:(g + 1) * qb, :].astype(BF16)


def _prompt_attn(q, qi, kw, kb, vb, *, batch):
    rows = q.shape[0]
    seq = rows // batch
    ck = min(KEY_CHUNK, seq)
    k_sel = min(TOPK_MAX, seq // 4)
    qb = next(r for r in QUERY_ROWS if seq % r == 0)
    nblk = seq // qb
    gq = GQA_GROUP * qb
    blk = lambda b, i: (b * nblk + i, 0)
    per_b = lambda b, i: (b, 0)
    return pl.pallas_call(
        functools.partial(_prompt_attn_kernel, k_sel=k_sel, qb=qb),
        grid=(batch, nblk),
        in_specs=[
            pl.BlockSpec((qb, q.shape[1]), blk),
            pl.BlockSpec((qb, qi.shape[1]), blk),
            pl.BlockSpec((qb, LANES), blk),
            pl.BlockSpec((seq, LANES), per_b),
            pl.BlockSpec((seq, kb.shape[1]), per_b),
            pl.BlockSpec((seq, vb.shape[1]), per_b),
        ],
        out_specs=pl.BlockSpec((qb, q.shape[1]), blk),
        out_shape=jax.ShapeDtypeStruct(q.shape, BF16),
        scratch_shapes=[
            pltpu.VMEM((seq, IDX_DIM), BF16),
            pltpu.VMEM((seq // ck, qb, ck), F32),
            pltpu.VMEM((qb, LANES), F32),
            pltpu.VMEM((N_KV_HEADS, gq, HEAD_DIM), BF16),
            pltpu.VMEM((N_KV_HEADS, gq, LANES), F32),
            pltpu.VMEM((N_KV_HEADS, gq, LANES), F32),
            pltpu.VMEM((N_KV_HEADS, gq, HEAD_DIM), F32),
        ],
        compiler_params=_cparams(("parallel", "arbitrary")),
        name="prompt_attn",
    )(q, qi, kw, kw, kb, vb)


def _page_copy(cache_ref, layer, page, buf_ref, sem_ref, slot, p):
    return pltpu.make_async_copy(cache_ref.at[layer, page], buf_ref.at[slot, p], sem_ref.at[slot])


def _prefetch_pages(pt_ref, npg, streams):
    b, c = pl.program_id(0), pl.program_id(1)
    n_c = pl.num_programs(1)
    step = b * n_c + c
    slot = lax.rem(step, 2)

    def start(bb, cc, sl):
        for cache_ref, layer, buf_ref, sem_ref in streams:
            for p in range(npg):
                _page_copy(cache_ref, layer, pt_ref[bb, cc * npg + p], buf_ref, sem_ref, sl, p).start()

    @pl.when(step == 0)
    def _():
        start(b, c, slot)

    @pl.when(step + 1 < pl.num_programs(0) * n_c)
    def _():
        wrap = c + 1 == n_c
        start(jnp.where(wrap, b + 1, b), jnp.where(wrap, 0, c + 1), 1 - slot)

    for cache_ref, layer, buf_ref, sem_ref in streams:
        for p in range(npg):
            _page_copy(cache_ref, layer, 0, buf_ref, sem_ref, slot, p).wait()
    return slot


def _sample_scores_kernel(pt_ref, qi_ref, w_ref, kin_ref, cache_ref, past_ref, new_ref, buf_ref, kc_ref, sem_ref,
                          *, layer, nt, npg):
    slot = _prefetch_pages(pt_ref, npg, [(cache_ref, layer, buf_ref, sem_ref)])
    c = pl.program_id(1)
    w = w_ref[...] * IDX_W_SCALE * IDX_SCALE
    qi = qi_ref[...]

    def scores(s):
        width = s.shape[1]
        s = jnp.maximum(s, 0.0) * jnp.concatenate([w] * (width // LANES), axis=1)
        return jnp.sum(s.reshape(nt, IDX_HEADS, width), axis=1)

    for p in range(npg):
        kc_ref[:, p * PAGE_SIZE:(p + 1) * PAGE_SIZE] = buf_ref[slot, p].astype(BF16)
    past_ref[...] = scores(_dot(qi, kc_ref[...]))

    @pl.when(c == 0)
    def _():
        new_ref[...] = scores(_dot_nt(qi, kin_ref[...].astype(BF16)))


def _sample_scores(page_table, qi_b, w_b, ki_new, cache_ki, *, layer, nt):
    nb, n_pages = page_table.shape
    npg = min(IDX_PAGES_PER_STEP, n_pages)
    rows = nt * IDX_HEADS
    grid_spec = pltpu.PrefetchScalarGridSpec(
        num_scalar_prefetch=1,
        grid=(nb, n_pages // npg),
        in_specs=[
            pl.BlockSpec((None, rows, IDX_DIM), lambda b, c, pt: (b, 0, 0)),
            pl.BlockSpec((None, rows, LANES), lambda b, c, pt: (b, 0, 0)),
            pl.BlockSpec((None, PAGE_SIZE, IDX_DIM), lambda b, c, pt: (b, 0, 0)),
            pl.BlockSpec(memory_space=pl.ANY),
        ],
        out_specs=[
            pl.BlockSpec((None, nt, npg * PAGE_SIZE), lambda b, c, pt: (b, 0, c)),
            pl.BlockSpec((None, nt, PAGE_SIZE), lambda b, c, pt: (b, 0, 0)),
        ],
        scratch_shapes=[pltpu.VMEM((2, npg, IDX_DIM, PAGE_SIZE), F32), pltpu.VMEM((IDX_DIM, npg * PAGE_SIZE), BF16),
                        pltpu.SemaphoreType.DMA((2,))],
    )
    return pl.pallas_call(
        functools.partial(_sample_scores_kernel, layer=layer, nt=nt, npg=npg),
        grid_spec=grid_spec,
        out_shape=[jax.ShapeDtypeStruct((nb, nt, n_pages * PAGE_SIZE), F32),
                   jax.ShapeDtypeStruct((nb, nt, PAGE_SIZE), F32)],
        compiler_params=_cparams(("arbitrary", "arbitrary")),
        name="sample_scores",
    )(page_table, qi_b, w_b, ki_new, cache_ki)


def _sample_select_kernel(past_ref, new_ref, bias_ref, biasn_ref, sc_ref, tau_ref, *, nt, k_sel, ck):
    rows, n_past = past_ref.shape
    npc = n_past // ck
    for c in range(npc):
        sc_ref[c] = past_ref[:, c * ck:(c + 1) * ck]
    t_row = lax.broadcasted_iota(I32, (rows, ck), 0) & (nt - 1)
    col = lax.broadcasted_iota(I32, (rows, ck), 1)
    new_sc = jnp.concatenate([new_ref[...]] * (ck // PAGE_SIZE), axis=1)
    sc_ref[npc] = jnp.where(col <= t_row, new_sc, -jnp.inf)
    _select_topk(sc_ref, tau_ref, npc + 1, rows, ck, k_sel)
    tau_b = tau_ref[...]
    d = (lax.broadcasted_iota(I32, (LANES, LANES * N_KV_HEADS), 1)
         - N_KV_HEADS * lax.broadcasted_iota(I32, (LANES, LANES * N_KV_HEADS), 0))
    expand = jnp.where(d >= 0, jnp.where(d < N_KV_HEADS, 1.0, 0.0), 0.0).astype(BF16)
    def bias_of(c, t):
        sel = jnp.where(sc_ref[c][:, t * LANES:(t + 1) * LANES] >= tau_b, 1.0, 0.0).astype(BF16)
        return (_dot(sel, expand) - 1.0) * (-NEG_BIG)

    for c in range(npc):
        for t in range(ck // LANES):
            lo = (c * ck + t * LANES) * N_KV_HEADS
            bias_ref[:, lo:lo + LANES * N_KV_HEADS] = bias_of(c, t)
    biasn_ref[...] = bias_of(npc, 0)[:, 0:LANES]


def _sample_select(past, new, *, nt):
    rows, n_past = past.shape
    assert nt & (nt - 1) == 0 and nt * N_KV_HEADS <= LANES
    ck = min(KEY_CHUNK, n_past)
    k_sel = min(TOPK_MAX, (n_past + nt) // 4)
    return pl.pallas_call(
        functools.partial(_sample_select_kernel, nt=nt, k_sel=k_sel, ck=ck),
        out_shape=[jax.ShapeDtypeStruct((rows, n_past * N_KV_HEADS), F32), jax.ShapeDtypeStruct((rows, LANES), F32)],
        scratch_shapes=[pltpu.VMEM((n_past // ck + 1, rows, ck), F32), pltpu.VMEM((rows, LANES), F32)],
        compiler_params=pltpu.CompilerParams(vmem_limit_bytes=VMEM_LIMIT),
        name="sample_select",
    )(past, new)


def _sample_attn_kernel(pt_ref, q_ref, bias_ref, biasn_ref, kn_ref, vn_ref, ck_ref, cv_ref, o_ref,
                        kbuf_ref, vbuf_ref, kc_ref, vc_ref, m_ref, l_ref, acc_ref, ksem_ref, vsem_ref, *, layer, nt, npg):
    slot = _prefetch_pages(pt_ref, npg, [(ck_ref, layer, kbuf_ref, ksem_ref), (cv_ref, layer, vbuf_ref, vsem_ref)])
    c = pl.program_id(1)
    rows = q_ref.shape[0]
    page_rows = PAGE_SIZE * N_KV_HEADS

    @pl.when(c == 0)
    def _():
        m_ref[...] = jnp.full(m_ref.shape, NEG_BIG, F32)
        l_ref[...] = jnp.zeros_like(l_ref)
        acc_ref[...] = jnp.zeros_like(acc_ref)

    row_head = lax.shift_right_logical(lax.broadcasted_iota(I32, (rows, page_rows), 0), _log2(rows // N_KV_HEADS))
    col_head = lax.broadcasted_iota(I32, (rows, page_rows), 1) & (N_KV_HEADS - 1)
    head_bias = jnp.where(row_head == col_head, 0.0, NEG_BIG)
    first_of_pair = pl.program_id(0) % 2 == 0

    def update(k2, v2, bias_blk):
        s = _dot_nt(q_ref[...], k2)
        bias = jnp.where(first_of_pair, bias_blk[0:nt], bias_blk[nt:2 * nt])
        bias = jnp.concatenate([bias] * (rows // nt), axis=0)
        lanes = lambda a, t: a[:, t * LANES:(t + 1) * LANES]
        per_page = page_rows // LANES
        tiles = [lanes(s, t) + (lanes(bias, t) + lanes(head_bias, t % per_page)) for t in range(k2.shape[0] // LANES)]
        m_prev = m_ref[...]
        m_new = jnp.maximum(m_prev, jnp.max(functools.reduce(jnp.maximum, tiles), axis=1, keepdims=True))
        alpha = jnp.exp2((m_prev - m_new) * SOFTMAX_LOG2_SCALE)
        p_tiles = [jnp.exp2((t - m_new) * SOFTMAX_LOG2_SCALE) for t in tiles]
        l_ref[...] = alpha * l_ref[...] + functools.reduce(jnp.add, p_tiles)
        acc_ref[...] = alpha * acc_ref[...] + _dot(jnp.concatenate(p_tiles, axis=1).astype(BF16), v2)
        m_ref[...] = m_new

    for p in range(npg):
        kc_ref[p * page_rows:(p + 1) * page_rows, :] = kbuf_ref[slot, p].reshape(page_rows, HEAD_DIM).astype(BF16)
        vc_ref[p * page_rows:(p + 1) * page_rows, :] = vbuf_ref[slot, p].reshape(page_rows, HEAD_DIM).astype(BF16)
    update(kc_ref[...], vc_ref[...], bias_ref[...])

    @pl.when(c == pl.num_programs(1) - 1)
    def _():
        def new_rows(ref):
            pad = jnp.zeros((LANES - ref.shape[0], HEAD_DIM), F32)
            return jnp.concatenate([ref[...], pad], axis=0).astype(BF16)

        update(new_rows(kn_ref), new_rows(vn_ref), biasn_ref[...])
        o_ref[...] = acc_ref[...] / jnp.sum(l_ref[...], axis=1, keepdims=True)


def _sample_attn(page_table, q_b, bias, bias_new, k_new, v_new, cache_k, cache_v, *, layer, nt):
    nb, n_pages = page_table.shape
    npg = min(PAGES_PER_STEP, n_pages)
    rows = q_b.shape[1]
    page_rows = PAGE_SIZE * N_KV_HEADS
    assert N_KV_HEADS & (N_KV_HEADS - 1) == 0 and 2 * nt == SUBLANES and nb % 2 == 0

    per_b3 = lambda b, c, pt: (b, 0, 0)
    grid_spec = pltpu.PrefetchScalarGridSpec(
        num_scalar_prefetch=1,
        grid=(nb, n_pages // npg),
        in_specs=[
            pl.BlockSpec((None, rows, HEAD_DIM), per_b3),
            pl.BlockSpec((2 * nt, npg * page_rows), lambda b, c, pt: (b // 2, c)),
            pl.BlockSpec((2 * nt, LANES), lambda b, c, pt: (b // 2, 0)),
            pl.BlockSpec((None, nt * N_KV_HEADS, HEAD_DIM), per_b3),
            pl.BlockSpec((None, nt * N_KV_HEADS, HEAD_DIM), per_b3),
            pl.BlockSpec(memory_space=pl.ANY),
            pl.BlockSpec(memory_space=pl.ANY),
        ],
        out_specs=pl.BlockSpec((None, rows, HEAD_DIM), per_b3),
        scratch_shapes=[
            pltpu.VMEM((2, npg, PAGE_SIZE, N_KV_HEADS, HEAD_DIM), F32),
            pltpu.VMEM((2, npg, PAGE_SIZE, N_KV_HEADS, HEAD_DIM), F32),
            pltpu.VMEM((npg * page_rows, HEAD_DIM), BF16),
            pltpu.VMEM((npg * page_rows, HEAD_DIM), BF16),
            pltpu.VMEM((rows, LANES), F32),
            pltpu.VMEM((rows, LANES), F32),
            pltpu.VMEM((rows, HEAD_DIM), F32),
            pltpu.SemaphoreType.DMA((2,)),
            pltpu.SemaphoreType.DMA((2,)),
        ],
    )
    return pl.pallas_call(
        functools.partial(_sample_attn_kernel, layer=layer, nt=nt, npg=npg),
        grid_spec=grid_spec,
        out_shape=jax.ShapeDtypeStruct((nb, rows, HEAD_DIM), F32),
        compiler_params=_cparams(("arbitrary", "arbitrary")),
        name="sample_attn",
    )(page_table, q_b, bias, bias_new, k_new, v_new, cache_k, cache_v)


def _proj_out_kernel(x_ref, r_ref, a_ref, w_ref, g_ref, b_ref, o_ref, *, alpha):
    d_rnn = r_ref.shape[1]
    mix = _dot(r_ref[...], w_ref[0:d_rnn, :]) + _dot(a_ref[...], w_ref[d_rnn:, :])
    o_ref[...] = _layer_norm(alpha * x_ref[...] + mix, g_ref[...], b_ref[...])


def _proj_out_ln(x, rnn, attn, w, g, b, *, alpha, tm):
    rows, d = x.shape
    return pl.pallas_call(
        functools.partial(_proj_out_kernel, alpha=alpha),
        grid=(rows // tm,),
        in_specs=[
            pl.BlockSpec((tm, d), lambda i: (i, 0)),
            pl.BlockSpec((tm, rnn.shape[1]), lambda i: (i, 0)),
            pl.BlockSpec((tm, attn.shape[1]), lambda i: (i, 0)),
            pl.BlockSpec(w.shape, lambda i: (0, 0)),
            pl.BlockSpec((1, d), lambda i: (0, 0)),
            pl.BlockSpec((1, d), lambda i: (0, 0)),
        ],
        out_specs=pl.BlockSpec((tm, d), lambda i: (i, 0)),
        out_shape=jax.ShapeDtypeStruct((rows, d), F32),
        compiler_params=_cparams(("parallel",)),
        name="proj_out_ln",
    )(x, rnn, attn, w, g, b)


def _row_tile(rows, target):
    tm = min(rows, target)
    while rows % tm:
        tm //= 2
    return tm


def _pad_to(a, axis, size):
    pad = [(0, 0)] * a.ndim
    pad[axis] = (0, size - a.shape[axis])
    return jnp.pad(a, pad)


def kernel(x_prompt, x_sample, cache_k, cache_v, cache_k_idx, state_conv, state_rnn, page_table, ln1_g, ln1_b, ffn1_w_gu, ffn1_w_down, w_in, conv_w, conv_b, lru_w_a, lru_b_a, lru_w_i, lru_b_i, lru_lambda, w_out, ln2_g, ln2_b, ffn2_w_gu, ffn2_w_down, ln3_g, ln3_b):
    depth = w_in.shape[0]
    bp, seq, d_model = x_prompt.shape
    nb, nt, _ = x_sample.shape
    d_rnn = conv_w.shape[2]
    d_q = N_Q_HEADS * HEAD_DIM
    d_kv = N_KV_HEADS * HEAD_DIM
    d_qi = IDX_HEADS * IDX_DIM
    alpha = (2.0 * depth) ** 0.25

    xp = x_prompt.reshape(bp * seq, d_model)
    xs = x_sample.transpose(1, 0, 2).reshape(nt * nb, d_model)
    tm_p = _row_tile(bp * seq, 1024)
    tm_r = _row_tile(bp * seq, 512)
    tm_s = _row_tile(nt * nb, 512)
    vec = lambda a: a.reshape(1, -1)

    outs = [[] for _ in range(10)]
    for l in range(depth):
        d_in = w_in.shape[2]
        w_in_b = _pad_to(w_in[l].astype(BF16), 1, d_in - IDX_DIM - IDX_HEADS + LANES)
        w_out_b = w_out[l].astype(BF16)
        wa_b = lru_w_a[l].astype(BF16)
        wi_b = lru_w_i[l].astype(BF16)
        lru_args = (conv_w[l], vec(conv_b[l]), wa_b, vec(lru_b_a[l]), wi_b, vec(lru_b_i[l]), vec(lru_lambda[l]))
        proj = functools.partial(_proj_in, d_rnn=d_rnn, d_q=d_q, d_kv=d_kv, d_qi=d_qi)

        x1p, x1s = _ffn_ln(xp, xs, ffn1_w_gu[l], ffn1_w_down[l], vec(ln1_g[l]), vec(ln1_b[l]),
                           alpha=alpha, tm=tm_p, tf=MXU_DIM)
        xr_p, gr_p, q_p, k_p, v_p, kb_p, vb_p, qi_p, kw_p = proj(x1p, w_in_b, tm=_row_tile(bp * seq, 256))
        xr_s, gr_s, q_s, k_s, v_s, _, _, qi_s, kw_s = proj(x1s, w_in_b, tm=_row_tile(nt * nb, 256))

        rnn_p, hl_p = _lru_prompt(xr_p, gr_p, *lru_args, batch=bp, tt=_row_tile(seq, 512))
        conv_tm = state_conv[l].transpose(1, 0, 2).reshape((CONV_W - 1) * nb, d_rnn)
        xpad_s = jnp.concatenate([conv_tm, xr_s], axis=0)
        rnn_s, hl_s = _lru_sample(xpad_s, gr_s, state_rnn[l], *lru_args, nb=nb, nt=nt)

        attn_p = _prompt_attn(q_p, qi_p, kw_p, kb_p, vb_p, batch=bp)

        to_b = lambda a: a.reshape(nt, nb, -1).transpose(1, 0, 2)
        qi_b = to_b(qi_s).reshape(nb, nt * IDX_HEADS, IDX_DIM)
        kw_b = to_b(kw_s)
        w_b = jnp.broadcast_to(kw_b[:, :, IDX_DIM:IDX_DIM + IDX_HEADS].reshape(nb, nt * IDX_HEADS, 1),
                               (nb, nt * IDX_HEADS, LANES))
        ki_new = _pad_to(kw_b[:, :, :IDX_DIM], 1, PAGE_SIZE)
        cache_ki_t = jnp.swapaxes(cache_k_idx, 2, 3)
        past_sc, new_sc = _sample_scores(page_table, qi_b, w_b, ki_new, cache_ki_t, layer=l, nt=nt)
        n_past = past_sc.shape[2]
        bias, bias_new = _sample_select(past_sc.reshape(nb * nt, n_past), new_sc.reshape(nb * nt, PAGE_SIZE), nt=nt)
        q_b = to_b(q_s).reshape(nb, nt, N_KV_HEADS, GQA_GROUP, HEAD_DIM)
        q_b = q_b.transpose(0, 2, 3, 1, 4).reshape(nb, N_Q_HEADS * nt, HEAD_DIM)
        new_rows = lambda a: to_b(a).reshape(nb, nt * N_KV_HEADS, HEAD_DIM)
        o_b = _sample_attn(page_table, q_b, bias, bias_new, new_rows(k_s), new_rows(v_s), cache_k, cache_v,
                           layer=l, nt=nt)
        attn_s = o_b.reshape(nb, N_KV_HEADS, GQA_GROUP, nt, HEAD_DIM).transpose(3, 0, 1, 2, 4)
        attn_s = attn_s.reshape(nt * nb, d_q).astype(BF16)

        x2p = _proj_out_ln(x1p, rnn_p, attn_p, w_out_b, vec(ln2_g[l]), vec(ln2_b[l]), alpha=alpha, tm=tm_r)
        x2s = _proj_out_ln(x1s, rnn_s, attn_s, w_out_b, vec(ln2_g[l]), vec(ln2_b[l]), alpha=alpha, tm=tm_s)
        xp, xs = _ffn_ln(x2p, x2s, ffn2_w_gu[l], ffn2_w_down[l], vec(ln3_g[l]), vec(ln3_b[l]),
                         alpha=alpha, tm=tm_p, tf=MXU_DIM)

        xr_p3 = xr_p.reshape(bp, seq, d_rnn)
        conv_p = xr_p3[:, -(CONV_W - 1):]
        conv_s = xpad_s.reshape(CONV_W - 1 + nt, nb, d_rnn)[-(CONV_W - 1):].transpose(1, 0, 2)
        layer_out = (
            k_p.reshape(bp, seq, N_KV_HEADS, HEAD_DIM), v_p.reshape(bp, seq, N_KV_HEADS, HEAD_DIM),
            kw_p[:, :IDX_DIM].reshape(bp, seq, IDX_DIM), conv_p, hl_p.reshape(bp, d_rnn),
            to_b(k_s).reshape(nb, nt, N_KV_HEADS, HEAD_DIM), to_b(v_s).reshape(nb, nt, N_KV_HEADS, HEAD_DIM),
            kw_b[:, :, :IDX_DIM], conv_s, hl_s,
        )
        for acc, val in zip(outs, layer_out):
            acc.append(val)

    y_p = xp.reshape(bp, seq, d_model)
    y_s = xs.reshape(nt, nb, d_model).transpose(1, 0, 2)
    return (y_p, y_s) + tuple(jnp.stack(o) for o in outs)
```

```python
import functools
import math

import jax
import jax.numpy as jnp
from jax import lax
from jax.experimental import pallas as pl
from jax.experimental.pallas import tpu as pltpu

F32 = jnp.float32
BF16 = jnp.bfloat16
I32 = jnp.int32

LRU_BLOCKS = 8
CONV_W = 4
LRU_C = 8.0
HEAD_DIM = 128
N_KV_HEADS = 4
GQA_GROUP = 2
N_Q_HEADS = N_KV_HEADS * GQA_GROUP
IDX_HEADS = 8
IDX_DIM = 64
TOPK_MAX = 256
QUERY_ROWS = (512, 256, 128)
COUNT_ROWS = 128
PAGE_SIZE = 128
LN_EPS = 1e-5
ATTN_SCALE = HEAD_DIM ** -0.5
SOFTMAX_LOG2_SCALE = ATTN_SCALE * 1.4426950408889634
IDX_SCALE = IDX_DIM ** -0.5
IDX_W_SCALE = IDX_HEADS ** -0.5

LANES = 128
SUBLANES = 8
MXU_DIM = 256
VMEM_LIMIT = 56 * 1024 * 1024
FFN_VMEM_LIMIT = 60 * 1024 * 1024

INT_MIN = -2 ** 31
FLT_LOWEST = -3.4028234663852886e38
NEG_BIG = -1e30
KEY_CHUNK = 512
PAGES_PER_STEP = 32
IDX_PAGES_PER_STEP = 64


def _cparams(semantics):
    return pltpu.CompilerParams(dimension_semantics=semantics, vmem_limit_bytes=VMEM_LIMIT)


def _layer_norm(y, g, b):
    mu = jnp.mean(y, axis=-1, keepdims=True)
    d = y - mu
    var = jnp.mean(d * d, axis=-1, keepdims=True)
    return d * lax.rsqrt(var + LN_EPS) * g + b


def _dot(a, b):
    return jnp.dot(a, b, preferred_element_type=F32)


def _dot_nt(a, b):
    return lax.dot_general(a, b, (((1,), (1,)), ((), ())), preferred_element_type=F32)


def _log2(n):
    assert n > 0 and n & (n - 1) == 0, n
    return n.bit_length() - 1


def _ffn_kernel(x_ref, xs_ref, wg_ref, *rest, alpha, tail, n_up):
    wu_refs = rest[:n_up]
    wd_ref, g_ref, b_ref, o_ref, os_ref, xb_ref, xsb_ref = rest[n_up:]
    i, j = pl.program_id(0), pl.program_id(1)
    tf = wg_ref.shape[1]
    up_w = wu_refs[0].shape[1]
    last = pl.num_programs(1) - 1

    @pl.when(j == 0)
    def _():
        xb_ref[...] = x_ref[...].astype(BF16)
        o_ref[...] = jnp.zeros_like(o_ref)

    @pl.when((j == 0) & (i == 0))
    def _():
        xsb_ref[...] = xs_ref[...].astype(BF16)
        os_ref[...] = jnp.zeros_like(os_ref)

    def chunk(width):
        w_gate = wg_ref[:, 0:width].astype(BF16)
        w_up = [r[...].astype(BF16) for r in wu_refs[:width // up_w]]
        w_up = w_up[0] if len(w_up) == 1 else jnp.concatenate(w_up, axis=1)
        w_down = wd_ref[0:width, :].astype(BF16)

        def apply(rows_bf16, acc_ref):
            gate = _dot(rows_bf16, w_gate)
            act = (gate * jax.nn.sigmoid(gate) * _dot(rows_bf16, w_up)).astype(BF16)
            acc_ref[...] += _dot(act, w_down)

        apply(xb_ref[...], o_ref)
        pl.when(i == 0)(lambda: apply(xsb_ref[...], os_ref))

    if tail == tf:
        chunk(tf)
    else:
        pl.when(j < last)(lambda: chunk(tf))
        pl.when(j == last)(lambda: chunk(tail))

    @pl.when(j == last)
    def _():
        o_ref[...] = _layer_norm(alpha * x_ref[...] + 0.5 * o_ref[...], g_ref[...], b_ref[...])

    @pl.when((j == last) & (i == 0))
    def _():
        os_ref[...] = _layer_norm(alpha * xs_ref[...] + 0.5 * os_ref[...], g_ref[...], b_ref[...])


def _ffn_ln(x, xs, w_gu, w_down, g, b, *, alpha, tm, tf):
    rows, d = x.shape
    d_ff = w_down.shape[0]
    nj = pl.cdiv(d_ff, tf)
    up_w = math.gcd(d_ff, tf)
    n_up = tf // up_w
    last_up = 2 * d_ff // up_w - 1
    up_specs = [
        pl.BlockSpec((d, up_w), functools.partial(
            lambda i, j, p: (0, jnp.minimum((d_ff + j * tf) // up_w + p, last_up)), p=p))
        for p in range(n_up)
    ]
    const = lambda i, j: (0, 0)
    return pl.pallas_call(
        functools.partial(_ffn_kernel, alpha=alpha, tail=d_ff - (nj - 1) * tf, n_up=n_up),
        grid=(rows // tm, nj),
        in_specs=[
            pl.BlockSpec((tm, d), lambda i, j: (i, 0)),
            pl.BlockSpec(xs.shape, const, pipeline_mode=pl.Buffered(1)),
            pl.BlockSpec((d, tf), lambda i, j: (0, j)),
        ] + up_specs + [
            pl.BlockSpec((tf, d), lambda i, j: (j, 0)),
            pl.BlockSpec((1, d), const),
            pl.BlockSpec((1, d), const),
        ],
        out_specs=[pl.BlockSpec((tm, d), lambda i, j: (i, 0)), pl.BlockSpec(xs.shape, const)],
        out_shape=[jax.ShapeDtypeStruct((rows, d), F32), jax.ShapeDtypeStruct(xs.shape, F32)],
        scratch_shapes=[pltpu.VMEM((tm, d), BF16), pltpu.VMEM(xs.shape, BF16)],
        compiler_params=pltpu.CompilerParams(dimension_semantics=("arbitrary", "arbitrary"),
                                             vmem_limit_bytes=FFN_VMEM_LIMIT),
        name="ffn_ln",
    )(x, xs, w_gu, *([w_gu] * n_up), w_down, g, b)


def _proj_in_kernel(x_ref, w_ref, xr_ref, gr_ref, q_ref, k_ref, v_ref, kb_ref, vb_ref, qi_ref, kw_ref,
                    *, d_rnn, d_q, d_kv, d_qi):
    xb = x_ref[...].astype(BF16)
    off = [0]

    def seg(width):
        lo = off[0]
        off[0] = lo + width
        return _dot(xb, w_ref[:, lo:lo + width])

    xr_ref[...] = seg(d_rnn)
    gr_ref[...] = seg(d_rnn)
    q_ref[...] = seg(d_q).astype(BF16)
    for full_ref, bf16_ref in ((k_ref, kb_ref), (v_ref, vb_ref)):
        kv = seg(d_kv)
        bf16_ref[...] = kv.astype(BF16)
        for n in range(N_KV_HEADS):
            full_ref[:, n, :] = kv[:, n * HEAD_DIM:(n + 1) * HEAD_DIM]
    qi_ref[...] = seg(d_qi).astype(BF16)
    kw_ref[...] = seg(LANES)


def _proj_in(x, w, *, tm, d_rnn, d_q, d_kv, d_qi):
    rows, d = x.shape
    heads = (d_kv // HEAD_DIM, HEAD_DIM)
    widths = ((d_rnn,), (d_rnn,), (d_q,), heads, heads, (d_kv,), (d_kv,), (d_qi,), (LANES,))
    dtypes = (F32, F32, BF16, F32, F32, BF16, BF16, BF16, F32)
    return pl.pallas_call(
        functools.partial(_proj_in_kernel, d_rnn=d_rnn, d_q=d_q, d_kv=d_kv, d_qi=d_qi),
        grid=(rows // tm,),
        in_specs=[
            pl.BlockSpec((tm, d), lambda i: (i, 0)),
            pl.BlockSpec(w.shape, lambda i: (0, 0), pipeline_mode=pl.Buffered(1)),
        ],
        out_specs=[pl.BlockSpec((tm,) + wd, lambda i, nd=len(wd): (i,) + (0,) * nd) for wd in widths],
        out_shape=[jax.ShapeDtypeStruct((rows,) + wd, dt) for wd, dt in zip(widths, dtypes)],
        compiler_params=_cparams(("parallel",)),
        name="proj_in",
    )(x, w)


def _softplus(z):
    return jnp.maximum(z, 0.0) + jnp.log1p(jnp.exp(-jnp.abs(z)))


def _lru_gates(xc, wa_ref, ba, wi_ref, bi, lam):
    bw = xc.shape[1] // LRU_BLOCKS
    sp = _softplus(-lam)
    a_parts, u_parts = [], []
    for n in range(LRU_BLOCKS):
        sl = slice(n * bw, (n + 1) * bw)
        xn = xc[:, sl]
        xb = xn.astype(BF16)
        r = jax.nn.sigmoid(_dot(xb, wa_ref[n]) + ba[:, sl])
        i = jax.nn.sigmoid(_dot(xb, wi_ref[n]) + bi[:, sl])
        log_a = -LRU_C * r * sp[:, sl]
        a = jnp.exp(log_a)
        a_parts.append(a)
        u_parts.append(jnp.sqrt(-jnp.tanh(log_a) * (a * a + 1.0)) * i * xn)
    return jnp.concatenate(a_parts, axis=1), jnp.concatenate(u_parts, axis=1)


def _lru_prompt_kernel(xr_ref, gr_ref, cw_ref, cb_ref, wa_ref, ba_ref, wi_ref, bi_ref, lam_ref,
                       o_ref, hl_ref, xp_ref, a_ref, u_ref, hs_ref, hc_ref, *, tt):
    j = pl.program_id(1)
    d = xr_ref.shape[1]

    @pl.when(j == 0)
    def _():
        xp_ref[0:SUBLANES, :] = jnp.zeros((SUBLANES, d), F32)
        hc_ref[...] = jnp.zeros_like(hc_ref)

    @pl.when(j > 0)
    def _():
        xp_ref[0:SUBLANES, :] = xp_ref[tt:tt + SUBLANES, :]

    xp_ref[SUBLANES:SUBLANES + tt, :] = xr_ref[...]
    cw = cw_ref[...]
    xc = cb_ref[...]
    for jj in range(CONV_W):
        lo = SUBLANES - (CONV_W - 1) + jj
        xc = xc + cw[jj:jj + 1, :] * xp_ref[lo:lo + tt, :]

    a, u = _lru_gates(xc, wa_ref, ba_ref[...], wi_ref, bi_ref[...], lam_ref[...])
    a_ref[...] = a
    u_ref[...] = u

    row = lax.broadcasted_iota(I32, (SUBLANES, d), 0)

    def group(g, h):
        r0 = pl.multiple_of(g * SUBLANES, SUBLANES)
        a8 = a_ref[pl.ds(r0, SUBLANES), :]
        u8 = u_ref[pl.ds(r0, SUBLANES), :]
        out = jnp.zeros((SUBLANES, d), F32)
        for jj in range(SUBLANES):
            aj = jnp.broadcast_to(a8[jj:jj + 1, :], (SUBLANES, d))
            uj = jnp.broadcast_to(u8[jj:jj + 1, :], (SUBLANES, d))
            h = aj * h + uj
            out = jnp.where(row == jj, h, out)
        hs_ref[pl.ds(r0, SUBLANES), :] = out
        return h

    h = lax.fori_loop(0, tt // SUBLANES, group, hc_ref[...])
    hc_ref[...] = h
    hl_ref[...] = h[0:1, :]
    o_ref[...] = (hs_ref[...] * jax.nn.gelu(gr_ref[...])).astype(BF16)


def _lru_prompt(xr, gr, cw, cb, wa, ba, wi, bi, lam, *, batch, tt):
    rows, d = xr.shape
    seq = rows // batch
    nt = seq // tt
    row_spec = pl.BlockSpec((tt, d), lambda b, j: (b * nt + j, 0))
    vec_spec = pl.BlockSpec((1, d), lambda b, j: (0, 0))
    w_spec = pl.BlockSpec(wa.shape, lambda b, j: (0, 0, 0))
    return pl.pallas_call(
        functools.partial(_lru_prompt_kernel, tt=tt),
        grid=(batch, nt),
        in_specs=[row_spec, row_spec, pl.BlockSpec((CONV_W, d), lambda b, j: (0, 0)), vec_spec,
                  w_spec, vec_spec, w_spec, vec_spec, vec_spec],
        out_specs=[row_spec, pl.BlockSpec((None, 1, d), lambda b, j: (b, 0, 0))],
        out_shape=[jax.ShapeDtypeStruct((rows, d), BF16), jax.ShapeDtypeStruct((batch, 1, d), F32)],
        scratch_shapes=[pltpu.VMEM((tt + SUBLANES, d), F32), pltpu.VMEM((tt, d), F32),
                        pltpu.VMEM((tt, d), F32), pltpu.VMEM((tt, d), F32), pltpu.VMEM((SUBLANES, d), F32)],
        compiler_params=_cparams(("parallel", "arbitrary")),
        name="lru_prompt",
    )(xr, gr, cw, cb, wa, ba, wi, bi, lam)


def _lru_sample_kernel(xp_ref, gr_ref, h0_ref, cw_ref, cb_ref, wa_ref, ba_ref, wi_ref, bi_ref, lam_ref,
                       o_ref, hl_ref, *, nb, nt):
    cw = cw_ref[...]
    xc = cb_ref[...] + cw[0:1, :] * xp_ref[0:nt * nb, :]
    for jj in range(1, CONV_W):
        xc = xc + cw[jj:jj + 1, :] * xp_ref[jj * nb:(jj + nt) * nb, :]
    a, u = _lru_gates(xc, wa_ref, ba_ref[...], wi_ref, bi_ref[...], lam_ref[...])
    gate = jax.nn.gelu(gr_ref[...])
    h = h0_ref[...]
    for t in range(nt):
        sl = slice(t * nb, (t + 1) * nb)
        h = a[sl, :] * h + u[sl, :]
        o_ref[sl, :] = (h * gate[sl, :]).astype(BF16)
    hl_ref[...] = h


def _lru_sample(xp, gr, h0, cw, cb, wa, ba, wi, bi, lam, *, nb, nt):
    d = gr.shape[1]
    return pl.pallas_call(
        functools.partial(_lru_sample_kernel, nb=nb, nt=nt),
        out_shape=[jax.ShapeDtypeStruct((nt * nb, d), BF16), jax.ShapeDtypeStruct((nb, d), F32)],
        compiler_params=pltpu.CompilerParams(vmem_limit_bytes=VMEM_LIMIT),
        name="lru_sample",
    )(xp, gr, h0, cw, cb, wa, ba, wi, bi, lam)


def _rank_to_float(u):
    key = u ^ jnp.int32(INT_MIN)
    return lax.bitcast_convert_type(key ^ ((key >> 31) & jnp.int32(0x7FFFFFFF)), F32)


def _count(sc_ref, nc, rows, ck, thr, strict):
    accs = []
    for r0 in range(0, rows, COUNT_ROWS):
        nr = min(COUNT_ROWS, rows - r0)
        thr_b = jnp.broadcast_to(thr[r0:r0 + nr], (nr, LANES))

        def body(c, acc, r0=r0, nr=nr, thr_b=thr_b):
            for t in range(ck // LANES):
                tile = sc_ref[c, r0:r0 + nr, t * LANES:(t + 1) * LANES]
                hit = (tile > thr_b) if strict else (tile >= thr_b)
                acc = acc + jnp.where(hit, 1.0, 0.0)
            return acc

        accs.append(lax.fori_loop(0, nc, body, jnp.zeros((nr, LANES), F32)))
    acc = accs[0] if len(accs) == 1 else jnp.concatenate(accs, axis=0)
    return jnp.sum(acc, axis=1, keepdims=True)


def _select_topk(sc_ref, tau_ref, nc, rows, ck, k_sel):
    kf = float(k_sel)

    def bit_body(it, carry):
        u, n_ge = carry
        cand = u | jnp.left_shift(jnp.int32(1), 31 - it)
        cnt = _count(sc_ref, nc, rows, ck, _rank_to_float(cand), strict=False)
        take = cnt >= kf
        return jnp.where(take, cand, u), jnp.where(take, cnt, n_ge)

    u, n_ge = lax.fori_loop(0, 32, bit_body, (jnp.zeros((rows, 1), I32), jnp.full((rows, 1), kf, F32)))
    tau = _rank_to_float(u)
    tau_ref[...] = jnp.broadcast_to(tau, (rows, LANES))

    @pl.when(jnp.max(n_ge) > kf)
    def _():
        before = (lax.broadcasted_iota(I32, (ck, ck), 0) < lax.broadcasted_iota(I32, (ck, ck), 1))
        before = jnp.where(before, 1.0, 0.0).astype(BF16)
        tau_w = jnp.broadcast_to(tau, (rows, ck))
        need = kf - _count(sc_ref, nc, rows, ck, tau, strict=True)
        need_w = jnp.broadcast_to(need, (rows, ck))

        def body(c, seen):
            sc = sc_ref[c]
            eq = sc == tau_w
            eqf = jnp.where(eq, 1.0, 0.0)
            rank = _dot(eqf.astype(BF16), before) + seen
            sc_ref[c] = jnp.where(eq, jnp.where(rank >= need_w, -jnp.inf, sc), sc)
            return seen + jnp.sum(eqf, axis=1, keepdims=True)

        lax.fori_loop(0, nc, body, jnp.zeros((rows, 1), F32))


def _prompt_attn_kernel(q_ref, qi_ref, kwq_ref, kwk_ref, kb_ref, vb_ref, o_ref,
                        ki_ref, sc_ref, tau_ref, qs_ref, m_ref, l_ref, acc_ref, *, k_sel, qb):
    i = pl.program_id(1)
    ck = sc_ref.shape[2]
    nc = (i * qb + qb + ck - 1) // ck

    @pl.when(i == 0)
    def _():
        ki_ref[...] = kwk_ref[:, 0:IDX_DIM].astype(BF16)

    w_idx = kwq_ref[:, IDX_DIM:IDX_DIM + IDX_HEADS] * IDX_W_SCALE * IDX_SCALE
    t_pos = i * qb + lax.broadcasted_iota(I32, (qb, ck), 0)
    col = lax.broadcasted_iota(I32, (qb, ck), 1)

    def score_body(c, carry):
        c0 = pl.multiple_of(c * ck, ck)
        kc = ki_ref[pl.ds(c0, ck), :]
        score = jnp.zeros((qb, ck), F32)
        for h in range(IDX_HEADS):
            s = _dot_nt(qi_ref[:, h * IDX_DIM:(h + 1) * IDX_DIM], kc)
            score = score + jnp.maximum(s, 0.0) * w_idx[:, h:h + 1]
        sc_ref[c] = jnp.where(col + c0 <= t_pos, score, -jnp.inf)
        return carry

    lax.fori_loop(0, nc, score_body, 0)

    need_select = (i + 1) * qb > k_sel

    @pl.when(jnp.logical_not(need_select))
    def _():
        tau_ref[...] = jnp.full(tau_ref.shape, FLT_LOWEST, F32)

    @pl.when(need_select)
    def _():
        _select_topk(sc_ref, tau_ref, nc, qb, ck, k_sel)
        row_pos = i * qb + lax.broadcasted_iota(I32, (qb, LANES), 0)
        tau_ref[...] = jnp.where(row_pos < k_sel, FLT_LOWEST, tau_ref[...])

    for n in range(N_KV_HEADS):
        for g in range(GQA_GROUP):
            hq = n * GQA_GROUP + g
            qs_ref[n, g * qb:(g + 1) * qb, :] = q_ref[:, hq * HEAD_DIM:(hq + 1) * HEAD_DIM]
    m_ref[...] = jnp.full(m_ref.shape, NEG_BIG, F32)
    l_ref[...] = jnp.zeros_like(l_ref)
    acc_ref[...] = jnp.zeros_like(acc_ref)
    tau_w = jnp.broadcast_to(tau_ref[:, 0:1], (qb, ck))

    def attn_body(c, carry):
        c0 = pl.multiple_of(c * ck, ck)
        bias = jnp.where(sc_ref[c] >= tau_w, 0.0, NEG_BIG)
        bias = jnp.concatenate([bias] * GQA_GROUP, axis=0)
        for n in range(N_KV_HEADS):
            kn = kb_ref[pl.ds(c0, ck), n * HEAD_DIM:(n + 1) * HEAD_DIM]
            vn = vb_ref[pl.ds(c0, ck), n * HEAD_DIM:(n + 1) * HEAD_DIM]
            s = _dot_nt(qs_ref[n], kn) + bias
            tiles = [s[:, t * LANES:(t + 1) * LANES] for t in range(ck // LANES)]
            m_prev = m_ref[n]
            m_new = jnp.maximum(m_prev, jnp.max(functools.reduce(jnp.maximum, tiles), axis=1, keepdims=True))
            alpha = jnp.exp2((m_prev - m_new) * SOFTMAX_LOG2_SCALE)
            p_tiles = [jnp.exp2((t - m_new) * SOFTMAX_LOG2_SCALE) for t in tiles]
            l_ref[n] = alpha * l_ref[n] + functools.reduce(jnp.add, p_tiles)
            p = jnp.concatenate(p_tiles, axis=1).astype(BF16)
            acc_ref[n] = alpha * acc_ref[n] + _dot(p, vn)
            m_ref[n] = m_new
        return carry

    lax.fori_loop(0, nc, attn_body, 0)

    for n in range(N_KV_HEADS):
        o = acc_ref[n] / jnp.sum(l_ref[n], axis=1, keepdims=True)
        for g in range(GQA_GROUP):
            hq = n * GQA_GROUP + g
            o_ref[:, hq * HEAD_DIM:(hq + 1) * HEAD_DIM] = o[g * qb:(g + 1) * qb, :].astype(BF16)


def _prompt_attn(q, qi, kw, kb, vb, *, batch):
    rows = q.shape[0]
    seq = rows // batch
    ck = min(KEY_CHUNK, seq)
    k_sel = min(TOPK_MAX, seq // 4)
    qb = next(r for r in QUERY_ROWS if seq % r == 0)
    nblk = seq // qb
    gq = GQA_GROUP * qb
    blk = lambda b, i: (b * nblk + i, 0)
    per_b = lambda b, i: (b, 0)
    return pl.pallas_call(
        functools.partial(_prompt_attn_kernel, k_sel=k_sel, qb=qb),
        grid=(batch, nblk),
        in_specs=[
            pl.BlockSpec((qb, q.shape[1]), blk),
            pl.BlockSpec((qb, qi.shape[1]), blk),
            pl.BlockSpec((qb, LANES), blk),
            pl.BlockSpec((seq, LANES), per_b),
            pl.BlockSpec((seq, kb.shape[1]), per_b),
            pl.BlockSpec((seq, vb.shape[1]), per_b),
        ],
        out_specs=pl.BlockSpec((qb, q.shape[1]), blk),
        out_shape=jax.ShapeDtypeStruct(q.shape, BF16),
        scratch_shapes=[
            pltpu.VMEM((seq, IDX_DIM), BF16),
            pltpu.VMEM((seq // ck, qb, ck), F32),
            pltpu.VMEM((qb, LANES), F32),
            pltpu.VMEM((N_KV_HEADS, gq, HEAD_DIM), BF16),
            pltpu.VMEM((N_KV_HEADS, gq, LANES), F32),
            pltpu.VMEM((N_KV_HEADS, gq, LANES), F32),
            pltpu.VMEM((N_KV_HEADS, gq, HEAD_DIM), F32),
        ],
        compiler_params=_cparams(("parallel", "arbitrary")),
        name="prompt_attn",
    )(q, qi, kw, kw, kb, vb)


def _page_copy(cache_ref, layer, page, buf_ref, sem_ref, slot, p):
    return pltpu.make_async_copy(cache_ref.at[layer, page], buf_ref.at[slot, p], sem_ref.at[slot])


def _prefetch_pages(pt_ref, npg, streams):
    b, c = pl.program_id(0), pl.program_id(1)
    n_c = pl.num_programs(1)
    step = b * n_c + c
    slot = lax.rem(step, 2)

    def start(bb, cc, sl):
        for cache_ref, layer, buf_ref, sem_ref in streams:
            for p in range(npg):
                _page_copy(cache_ref, layer, pt_ref[bb, cc * npg + p], buf_ref, sem_ref, sl, p).start()

    @pl.when(step == 0)
    def _():
        start(b, c, slot)

    @pl.when(step + 1 < pl.num_programs(0) * n_c)
    def _():
        wrap = c + 1 == n_c
        start(jnp.where(wrap, b + 1, b), jnp.where(wrap, 0, c + 1), 1 - slot)

    for cache_ref, layer, buf_ref, sem_ref in streams:
        for p in range(npg):
            _page_copy(cache_ref, layer, 0, buf_ref, sem_ref, slot, p).wait()
    return slot


def _sample_scores_kernel(pt_ref, qi_ref, w_ref, kin_ref, cache_ref, past_ref, new_ref, buf_ref, kc_ref, sem_ref,
                          *, layer, nt, npg):
    slot = _prefetch_pages(pt_ref, npg, [(cache_ref, layer, buf_ref, sem_ref)])
    c = pl.program_id(1)
    w = w_ref[...] * IDX_W_SCALE * IDX_SCALE
    qi = qi_ref[...]

    def scores(s):
        width = s.shape[1]
        s = jnp.maximum(s, 0.0) * jnp.concatenate([w] * (width // LANES), axis=1)
        return jnp.sum(s.reshape(nt, IDX_HEADS, width), axis=1)

    for p in range(npg):
        kc_ref[:, p * PAGE_SIZE:(p + 1) * PAGE_SIZE] = buf_ref[slot, p].astype(BF16)
    past_ref[...] = scores(_dot(qi, kc_ref[...]))

    @pl.when(c == 0)
    def _():
        new_ref[...] = scores(_dot_nt(qi, kin_ref[...].astype(BF16)))


def _sample_scores(page_table, qi_b, w_b, ki_new, cache_ki, *, layer, nt):
    nb, n_pages = page_table.shape
    npg = min(IDX_PAGES_PER_STEP, n_pages)
    rows = nt * IDX_HEADS
    grid_spec = pltpu.PrefetchScalarGridSpec(
        num_scalar_prefetch=1,
        grid=(nb, n_pages // npg),
        in_specs=[
            pl.BlockSpec((None, rows, IDX_DIM), lambda b, c, pt: (b, 0, 0)),
            pl.BlockSpec((None, rows, LANES), lambda b, c, pt: (b, 0, 0)),
            pl.BlockSpec((None, PAGE_SIZE, IDX_DIM), lambda b, c, pt: (b, 0, 0)),
            pl.BlockSpec(memory_space=pl.ANY),
        ],
        out_specs=[
            pl.BlockSpec((None, nt, npg * PAGE_SIZE), lambda b, c, pt: (b, 0, c)),
            pl.BlockSpec((None, nt, PAGE_SIZE), lambda b, c, pt: (b, 0, 0)),
        ],
        scratch_shapes=[pltpu.VMEM((2, npg, IDX_DIM, PAGE_SIZE), F32), pltpu.VMEM((IDX_DIM, npg * PAGE_SIZE), BF16),
                        pltpu.SemaphoreType.DMA((2,))],
    )
    return pl.pallas_call(
        functools.partial(_sample_scores_kernel, layer=layer, nt=nt, npg=npg),
        grid_spec=grid_spec,
        out_shape=[jax.ShapeDtypeStruct((nb, nt, n_pages * PAGE_SIZE), F32),
                   jax.ShapeDtypeStruct((nb, nt, PAGE_SIZE), F32)],
        compiler_params=_cparams(("arbitrary", "arbitrary")),
        name="sample_scores",
    )(page_table, qi_b, w_b, ki_new, cache_ki)


def _sample_select_kernel(past_ref, new_ref, bias_ref, biasn_ref, sc_ref, tau_ref, *, nt, k_sel, ck):
    rows, n_past = past_ref.shape
    npc = n_past // ck
    for c in range(npc):
        sc_ref[c] = past_ref[:, c * ck:(c + 1) * ck]
    t_row = lax.broadcasted_iota(I32, (rows, ck), 0) & (nt - 1)
    col = lax.broadcasted_iota(I32, (rows, ck), 1)
    new_sc = jnp.concatenate([new_ref[...]] * (ck // PAGE_SIZE), axis=1)
    sc_ref[npc] = jnp.where(col <= t_row, new_sc, -jnp.inf)
    _select_topk(sc_ref, tau_ref, npc + 1, rows, ck, k_sel)
    tau_b = tau_ref[...]
    d = (lax.broadcasted_iota(I32, (LANES, LANES * N_KV_HEADS), 1)
         - N_KV_HEADS * lax.broadcasted_iota(I32, (LANES, LANES * N_KV_HEADS), 0))
    expand = jnp.where(d >= 0, jnp.where(d < N_KV_HEADS, 1.0, 0.0), 0.0).astype(BF16)
    def bias_of(c, t):
        sel = jnp.where(sc_ref[c][:, t * LANES:(t + 1) * LANES] >= tau_b, 1.0, 0.0).astype(BF16)
        return (_dot(sel, expand) - 1.0) * (-NEG_BIG)

    for c in range(npc):
        for t in range(ck // LANES):
            lo = (c * ck + t * LANES) * N_KV_HEADS
            bias_ref[:, lo:lo + LANES * N_KV_HEADS] = bias_of(c, t)
    biasn_ref[...] = bias_of(npc, 0)[:, 0:LANES]


def _sample_select(past, new, *, nt):
    rows, n_past = past.shape
    assert nt & (nt - 1) == 0 and nt * N_KV_HEADS <= LANES
    ck = min(KEY_CHUNK, n_past)
    k_sel = min(TOPK_MAX, (n_past + nt) // 4)
    return pl.pallas_call(
        functools.partial(_sample_select_kernel, nt=nt, k_sel=k_sel, ck=ck),
        out_shape=[jax.ShapeDtypeStruct((rows, n_past * N_KV_HEADS), F32), jax.ShapeDtypeStruct((rows, LANES), F32)],
        scratch_shapes=[pltpu.VMEM((n_past // ck + 1, rows, ck), F32), pltpu.VMEM((rows, LANES), F32)],
        compiler_params=pltpu.CompilerParams(vmem_limit_bytes=VMEM_LIMIT),
        name="sample_select",
    )(past, new)


def _sample_attn_kernel(pt_ref, q_ref, bias_ref, biasn_ref, kn_ref, vn_ref, ck_ref, cv_ref, o_ref,
                        kbuf_ref, vbuf_ref, kc_ref, vc_ref, m_ref, l_ref, acc_ref, ksem_ref, vsem_ref, *, layer, nt, npg):
    slot = _prefetch_pages(pt_ref, npg, [(ck_ref, layer, kbuf_ref, ksem_ref), (cv_ref, layer, vbuf_ref, vsem_ref)])
    c = pl.program_id(1)
    rows = q_ref.shape[0]
    page_rows = PAGE_SIZE * N_KV_HEADS

    @pl.when(c == 0)
    def _():
        m_ref[...] = jnp.full(m_ref.shape, NEG_BIG, F32)
        l_ref[...] = jnp.zeros_like(l_ref)
        acc_ref[...] = jnp.zeros_like(acc_ref)

    row_head = lax.shift_right_logical(lax.broadcasted_iota(I32, (rows, page_rows), 0), _log2(rows // N_KV_HEADS))
    col_head = lax.broadcasted_iota(I32, (rows, page_rows), 1) & (N_KV_HEADS - 1)
    head_bias = jnp.where(row_head == col_head, 0.0, NEG_BIG)
    first_of_pair = pl.program_id(0) % 2 == 0

    def update(k2, v2, bias_blk):
        s = _dot_nt(q_ref[...], k2)
        bias = jnp.where(first_of_pair, bias_blk[0:nt], bias_blk[nt:2 * nt])
        bias = jnp.concatenate([bias] * (rows // nt), axis=0)
        lanes = lambda a, t: a[:, t * LANES:(t + 1) * LANES]
        per_page = page_rows // LANES
        tiles = [lanes(s, t) + (lanes(bias, t) + lanes(head_bias, t % per_page)) for t in range(k2.shape[0] // LANES)]
        m_prev = m_ref[...]
        m_new = jnp.maximum(m_prev, jnp.max(functools.reduce(jnp.maximum, tiles), axis=1, keepdims=True))
        alpha = jnp.exp2((m_prev - m_new) * SOFTMAX_LOG2_SCALE)
        p_tiles = [jnp.exp2((t - m_new) * SOFTMAX_LOG2_SCALE) for t in tiles]
        l_ref[...] = alpha * l_ref[...] + functools.reduce(jnp.add, p_tiles)
        acc_ref[...] = alpha * acc_ref[...] + _dot(jnp.concatenate(p_tiles, axis=1).astype(BF16), v2)
        m_ref[...] = m_new

    for p in range(npg):
        kc_ref[p * page_rows:(p + 1) * page_rows, :] = kbuf_ref[slot, p].reshape(page_rows, HEAD_DIM).astype(BF16)
        vc_ref[p * page_rows:(p + 1) * page_rows, :] = vbuf_ref[slot, p].reshape(page_rows, HEAD_DIM).astype(BF16)
    update(kc_ref[...], vc_ref[...], bias_ref[...])

    @pl.when(c == pl.num_programs(1) - 1)
    def _():
        def new_rows(ref):
            pad = jnp.zeros((LANES - ref.shape[0], HEAD_DIM), F32)
            return jnp.concatenate([ref[...], pad], axis=0).astype(BF16)

        update(new_rows(kn_ref), new_rows(vn_ref), biasn_ref[...])
        o_ref[...] = acc_ref[...] / jnp.sum(l_ref[...], axis=1, keepdims=True)


def _sample_attn(page_table, q_b, bias, bias_new, k_new, v_new, cache_k, cache_v, *, layer, nt):
    nb, n_pages = page_table.shape
    npg = min(PAGES_PER_STEP, n_pages)
    rows = q_b.shape[1]
    page_rows = PAGE_SIZE * N_KV_HEADS
    assert N_KV_HEADS & (N_KV_HEADS - 1) == 0 and 2 * nt == SUBLANES and nb % 2 == 0

    per_b3 = lambda b, c, pt: (b, 0, 0)
    grid_spec = pltpu.PrefetchScalarGridSpec(
        num_scalar_prefetch=1,
        grid=(nb, n_pages // npg),
        in_specs=[
            pl.BlockSpec((None, rows, HEAD_DIM), per_b3),
            pl.BlockSpec((2 * nt, npg * page_rows), lambda b, c, pt: (b // 2, c)),
            pl.BlockSpec((2 * nt, LANES), lambda b, c, pt: (b // 2, 0)),
            pl.BlockSpec((None, nt * N_KV_HEADS, HEAD_DIM), per_b3),
            pl.BlockSpec((None, nt * N_KV_HEADS, HEAD_DIM), per_b3),
            pl.BlockSpec(memory_space=pl.ANY),
            pl.BlockSpec(memory_space=pl.ANY),
        ],
        out_specs=pl.BlockSpec((None, rows, HEAD_DIM), per_b3),
        scratch_shapes=[
            pltpu.VMEM((2, npg, PAGE_SIZE, N_KV_HEADS, HEAD_DIM), F32),
            pltpu.VMEM((2, npg, PAGE_SIZE, N_KV_HEADS, HEAD_DIM), F32),
            pltpu.VMEM((npg * page_rows, HEAD_DIM), BF16),
            pltpu.VMEM((npg * page_rows, HEAD_DIM), BF16),
            pltpu.VMEM((rows, LANES), F32),
            pltpu.VMEM((rows, LANES), F32),
            pltpu.VMEM((rows, HEAD_DIM), F32),
            pltpu.SemaphoreType.DMA((2,)),
            pltpu.SemaphoreType.DMA((2,)),
        ],
    )
    return pl.pallas_call(
        functools.partial(_sample_attn_kernel, layer=layer, nt=nt, npg=npg),
        grid_spec=grid_spec,
        out_shape=jax.ShapeDtypeStruct((nb, rows, HEAD_DIM), F32),
        compiler_params=_cparams(("arbitrary", "arbitrary")),
        name="sample_attn",
    )(page_table, q_b, bias, bias_new, k_new, v_new, cache_k, cache_v)


def _proj_out_kernel(x_ref, r_ref, a_ref, w_ref, g_ref, b_ref, o_ref, *, alpha):
    d_rnn = r_ref.shape[1]
    mix = _dot(r_ref[...], w_ref[0:d_rnn, :]) + _dot(a_ref[...], w_ref[d_rnn:, :])
    o_ref[...] = _layer_norm(alpha * x_ref[...] + mix, g_ref[...], b_ref[...])


def _proj_out_ln(x, rnn, attn, w, g, b, *, alpha, tm):
    rows, d = x.shape
    return pl.pallas_call(
        functools.partial(_proj_out_kernel, alpha=alpha),
        grid=(rows // tm,),
        in_specs=[
            pl.BlockSpec((tm, d), lambda i: (i, 0)),
            pl.BlockSpec((tm, rnn.shape[1]), lambda i: (i, 0)),
            pl.BlockSpec((tm, attn.shape[1]), lambda i: (i, 0)),
            pl.BlockSpec(w.shape, lambda i: (0, 0)),
            pl.BlockSpec((1, d), lambda i: (0, 0)),
            pl.BlockSpec((1, d), lambda i: (0, 0)),
        ],
        out_specs=pl.BlockSpec((tm, d), lambda i: (i, 0)),
        out_shape=jax.ShapeDtypeStruct((rows, d), F32),
        compiler_params=_cparams(("parallel",)),
        name="proj_out_ln",
    )(x, rnn, attn, w, g, b)


def _row_tile(rows, target):
    tm = min(rows, target)
    while rows % tm:
        tm //= 2
    return tm


def _pad_to(a, axis, size):
    pad = [(0, 0)] * a.ndim
    pad[axis] = (0, size - a.shape[axis])
    return jnp.pad(a, pad)


def kernel(x_prompt, x_sample, cache_k, cache_v, cache_k_idx, state_conv, state_rnn, page_table, ln1_g, ln1_b, ffn1_w_gu, ffn1_w_down, w_in, conv_w, conv_b, lru_w_a, lru_b_a, lru_w_i, lru_b_i, lru_lambda, w_out, ln2_g, ln2_b, ffn2_w_gu, ffn2_w_down, ln3_g, ln3_b):
    depth = w_in.shape[0]
    bp, seq, d_model = x_prompt.shape
    nb, nt, _ = x_sample.shape
    d_rnn = conv_w.shape[2]
    d_q = N_Q_HEADS * HEAD_DIM
    d_kv = N_KV_HEADS * HEAD_DIM
    d_qi = IDX_HEADS * IDX_DIM
    alpha = (2.0 * depth) ** 0.25

    xp = x_prompt.reshape(bp * seq, d_model)
    xs = x_sample.transpose(1, 0, 2).reshape(nt * nb, d_model)
    tm_p = _row_tile(bp * seq, 1024)
    tm_r = _row_tile(bp * seq, 512)
    tm_s = _row_tile(nt * nb, 512)
    vec = lambda a: a.reshape(1, -1)

    outs = [[] for _ in range(10)]
    for l in range(depth):
        d_in = w_in.shape[2]
        w_in_b = _pad_to(w_in[l].astype(BF16), 1, d_in - IDX_DIM - IDX_HEADS + LANES)
        w_out_b = w_out[l].astype(BF16)
        wa_b = lru_w_a[l].astype(BF16)
        wi_b = lru_w_i[l].astype(BF16)
        lru_args = (conv_w[l], vec(conv_b[l]), wa_b, vec(lru_b_a[l]), wi_b, vec(lru_b_i[l]), vec(lru_lambda[l]))
        proj = functools.partial(_proj_in, d_rnn=d_rnn, d_q=d_q, d_kv=d_kv, d_qi=d_qi)

        x1p, x1s = _ffn_ln(xp, xs, ffn1_w_gu[l], ffn1_w_down[l], vec(ln1_g[l]), vec(ln1_b[l]),
                           alpha=alpha, tm=tm_p, tf=MXU_DIM)
        xr_p, gr_p, q_p, k_p, v_p, kb_p, vb_p, qi_p, kw_p = proj(x1p, w_in_b, tm=_row_tile(bp * seq, 256))
        xr_s, gr_s, q_s, k_s, v_s, _, _, qi_s, kw_s = proj(x1s, w_in_b, tm=_row_tile(nt * nb, 256))

        rnn_p, hl_p = _lru_prompt(xr_p, gr_p, *lru_args, batch=bp, tt=_row_tile(seq, 512))
        conv_tm = state_conv[l].transpose(1, 0, 2).reshape((CONV_W - 1) * nb, d_rnn)
        xpad_s = jnp.concatenate([conv_tm, xr_s], axis=0)
        rnn_s, hl_s = _lru_sample(xpad_s, gr_s, state_rnn[l], *lru_args, nb=nb, nt=nt)

        attn_p = _prompt_attn(q_p, qi_p, kw_p, kb_p, vb_p, batch=bp)

        to_b = lambda a: a.reshape(nt, nb, -1).transpose(1, 0, 2)
        qi_b = to_b(qi_s).reshape(nb, nt * IDX_HEADS, IDX_DIM)
        kw_b = to_b(kw_s)
        w_b = jnp.broadcast_to(kw_b[:, :, IDX_DIM:IDX_DIM + IDX_HEADS].reshape(nb, nt * IDX_HEADS, 1),
                               (nb, nt * IDX_HEADS, LANES))
        ki_new = _pad_to(kw_b[:, :, :IDX_DIM], 1, PAGE_SIZE)
        cache_ki_t = jnp.swapaxes(cache_k_idx, 2, 3)
        past_sc, new_sc = _sample_scores(page_table, qi_b, w_b, ki_new, cache_ki_t, layer=l, nt=nt)
        n_past = past_sc.shape[2]
        bias, bias_new = _sample_select(past_sc.reshape(nb * nt, n_past), new_sc.reshape(nb * nt, PAGE_SIZE), nt=nt)
        q_b = to_b(q_s).reshape(nb, nt, N_KV_HEADS, GQA_GROUP, HEAD_DIM)
        q_b = q_b.transpose(0, 2, 3, 1, 4).reshape(nb, N_Q_HEADS * nt, HEAD_DIM)
        new_rows = lambda a: to_b(a).reshape(nb, nt * N_KV_HEADS, HEAD_DIM)
        o_b = _sample_attn(page_table, q_b, bias, bias_new, new_rows(k_s), new_rows(v_s), cache_k, cache_v,
                           layer=l, nt=nt)
        attn_s = o_b.reshape(nb, N_KV_HEADS, GQA_GROUP, nt, HEAD_DIM).transpose(3, 0, 1, 2, 4)
        attn_s = attn_s.reshape(nt * nb, d_q).astype(BF16)

        x2p = _proj_out_ln(x1p, rnn_p, attn_p, w_out_b, vec(ln2_g[l]), vec(ln2_b[l]), alpha=alpha, tm=tm_r)
        x2s = _proj_out_ln(x1s, rnn_s, attn_s, w_out_b, vec(ln2_g[l]), vec(ln2_b[l]), alpha=alpha, tm=tm_s)
        xp, xs = _ffn_ln(x2p, x2s, ffn2_w_gu[l], ffn2_w_down[l], vec(ln3_g[l]), vec(ln3_b[l]),
                         alpha=alpha, tm=tm_p, tf=MXU_DIM)

        xr_p3 = xr_p.reshape(bp, seq, d_rnn)
        conv_p = xr_p3[:, -(CONV_W - 1):]
        conv_s = xpad_s.reshape(CONV_W - 1 + nt, nb, d_rnn)[-(CONV_W - 1):].transpose(1, 0, 2)
        layer_out = (
            k_p.reshape(bp, seq, N_KV_HEADS, HEAD_DIM), v_p.reshape(bp, seq, N_KV_HEADS, HEAD_DIM),
            kw_p[:, :IDX_DIM].reshape(bp, seq, IDX_DIM), conv_p, hl_p.reshape(bp, d_rnn),
            to_b(k_s).reshape(nb, nt, N_KV_HEADS, HEAD_DIM), to_b(v_s).reshape(nb, nt, N_KV_HEADS, HEAD_DIM),
            kw_b[:, :, :IDX_DIM], conv_s, hl_s,
        )
        for acc, val in zip(outs, layer_out):
            acc.append(val)

    y_p = xp.reshape(bp, seq, d_model)
    y_s = xs.reshape(nt, nb, d_model).transpose(1, 0, 2)
    return (y_p, y_s) + tuple(jnp.stack(o) for o in outs)
```
